```python
import jax, jax.numpy as jnp
from jax import lax
import numpy as np

D_MODEL = 1024
BATCH = 4
SEQ = 4096
DEPTH = 2
DEC_BATCH = 8
DEC_SEQ = 64
PAST_LEN = 4096

CHUNK = 64
P_DIM = 256
EPS = 1e-6
NEG = -1e30
HA = 4
DK_A = 128
DV_A = 256
GATE_RANK = 16
GATE_TAU = 16.0
WA_K = HA * DK_A
WA_V = HA * DV_A
HB = 16
DH_B = 64
LEFT_CHUNKS = 8
WINDOW = LEFT_CHUNKS * CHUNK
BAND = (LEFT_CHUNKS + 1) * CHUNK
MAX_REL = 128
WB = HB * DH_B
IN_SIZES = (WA_K, WA_K, WA_V, GATE_RANK, WA_V, WB, WB, WB, WB, D_MODEL, D_MODEL)
N_IN = 2 * WA_K + 2 * WA_V + GATE_RANK + 4 * WB + 2 * D_MODEL

kernel_name = 'hybrid_gla_chunkband_stream_step'


def _rmsnorm(x, g):
    xf = x.astype(jnp.float32)
    y = xf * lax.rsqrt(jnp.mean(xf * xf, axis=-1, keepdims=True) + EPS)
    return (y * g.astype(jnp.float32)).astype(x.dtype)


def _split_cols(z):
    offsets = [int(o) for o in np.cumsum(IN_SIZES)[:-1]]
    return jnp.split(z, offsets, axis=-1)


def _gla_block(q, k, v, lg, s0):
    q, k, v, lg = (a.astype(jnp.float32) for a in (q, k, v, lg))
    s0 = s0.astype(jnp.float32)
    L = q.shape[1]
    b = jnp.cumsum(lg, axis=1)
    causal = jnp.tril(jnp.ones((L, L), dtype=bool))
    diff = b[:, :, None] - b[:, None, :]
    decay = jnp.exp(jnp.where(causal[None, :, :, None, None], diff, -jnp.inf))
    attn = jnp.einsum('bihd,bjhd,bijhd->bhij', q, k, decay)
    o = jnp.einsum('bhij,bjhv->bihv', attn, v)
    o = o + jnp.einsum('bihd,bhdv->bihv', q * jnp.exp(b), s0)
    b_last = b[:, -1]
    k_dec = k * jnp.exp(b_last[:, None] - b)
    s1 = jnp.exp(b_last)[..., None] * s0 + jnp.einsum('bjhd,bjhv->bhdv', k_dec, v)
    return o, s1


def _gla_prompt(q, k, v, lg):
    B, T, H, DK = q.shape
    DV = v.shape[-1]
    NC = T // CHUNK

    def to_chunks(a):
        return jnp.moveaxis(a.reshape(B, NC, CHUNK, H, a.shape[-1]), 1, 0)

    def step(s, blk):
        qc, kc, vc, lc = blk
        o, s = _gla_block(qc, kc, vc, lc, s)
        return s, o

    s0 = jnp.zeros((B, H, DK, DV), jnp.float32)
    s, o = lax.scan(step, s0, (to_chunks(q), to_chunks(k), to_chunks(v), to_chunks(lg)))
    return jnp.moveaxis(o, 0, 1).reshape(B, T, H, DV), s


def _band_prompt(q, k, v, bias_tab):
    B, T, H, D = q.shape
    NC = T // CHUNK
    pad = jnp.zeros((B, WINDOW, H, D), k.dtype)
    kp = jnp.concatenate([pad, k], axis=1)
    vp = jnp.concatenate([pad, v], axis=1)
    qc = jnp.moveaxis(q.reshape(B, NC, CHUNK, H, D), 1, 0)
    rel = WINDOW + jnp.arange(CHUNK)[:, None] - jnp.arange(BAND)[None, :]
    bias = bias_tab[:, jnp.clip(rel, -MAX_REL, MAX_REL) + MAX_REL].astype(jnp.float32)
    scale = D ** -0.5

    def one_chunk(args):
        n, qn = args
        kn = lax.dynamic_slice_in_dim(kp, n * CHUNK, BAND, axis=1)
        vn = lax.dynamic_slice_in_dim(vp, n * CHUNK, BAND, axis=1)
        s = jnp.einsum('bihd,bjhd->bhij', qn, kn).astype(jnp.float32) * scale + bias
        valid = n * CHUNK + jnp.arange(BAND) >= WINDOW
        s = jnp.where(valid[None, None, None, :], s, NEG)
        pr = jax.nn.softmax(s, axis=-1).astype(vn.dtype)
        return jnp.einsum('bhij,bjhd->bihd', pr, vn)

    o = lax.map(one_chunk, (jnp.arange(NC), qc))
    return jnp.moveaxis(o, 0, 1).reshape(B, T, H, D)


def _band_sample(q, k_new, v_new, k_cache, v_cache, bias_tab):
    S = q.shape[1]
    Lc = k_cache.shape[1]
    kk = jnp.concatenate([k_cache.astype(k_new.dtype), k_new], axis=1)
    vv = jnp.concatenate([v_cache.astype(v_new.dtype), v_new], axis=1)
    qpos = PAST_LEN + jnp.arange(S)
    kpos = jnp.concatenate([PAST_LEN - Lc + jnp.arange(Lc), PAST_LEN + jnp.arange(S)])
    rel = qpos[:, None] - kpos[None, :]
    bias = bias_tab[:, jnp.clip(rel, -MAX_REL, MAX_REL) + MAX_REL].astype(jnp.float32)
    s = jnp.einsum('bihd,bjhd->bhij', q, kk).astype(jnp.float32) * (q.shape[-1] ** -0.5) + bias
    pr = jax.nn.softmax(s, axis=-1).astype(vv.dtype)
    return jnp.einsum('bhij,bjhd->bihd', pr, vv)


def _layer(x, p, s0, kc, vc, lw, prompt):
    (norm_g, w_in, w_gate_up, b_gate, gla_norm_g, q_norm_g, k_norm_g, rel_bias,
     w_branch_a, w_branch_b, w_out, ple_norm_g, w_ple_gate, w_ple) = lw
    B, T, _ = x.shape
    h = _rmsnorm(x, norm_g)
    z = h @ w_in
    qa, ka, va, ra, ga, qb, kb, vb, gb, mga, mgb = _split_cols(z)

    qa = qa.reshape(B, T, HA, DK_A) * (DK_A ** -0.5)
    ka = ka.reshape(B, T, HA, DK_A)
    va = va.reshape(B, T, HA, DV_A)
    lg = jax.nn.log_sigmoid((ra @ w_gate_up + b_gate).astype(jnp.float32)) / GATE_TAU
    lg = lg.reshape(B, T, HA, DK_A)
    if prompt:
        oa, sa = _gla_prompt(qa, ka, va, lg)
    else:
        oa, sa = _gla_block(qa, ka, va, lg, s0)
    oa = _rmsnorm(oa.astype(x.dtype), gla_norm_g).reshape(B, T, WA_V)
    ya = (oa * jax.nn.silu(ga)) @ w_branch_a

    qb = _rmsnorm(qb.reshape(B, T, HB, DH_B), q_norm_g)
    kb = _rmsnorm(kb.reshape(B, T, HB, DH_B), k_norm_g)
    vb = vb.reshape(B, T, HB, DH_B)
    if prompt:
        ob = _band_prompt(qb, kb, vb, rel_bias)
        keep = min(WINDOW, T)
        kbuf = kb[:, T - keep:]
        vbuf = vb[:, T - keep:]
    else:
        ob = _band_sample(qb, kb, vb, kc, vc, rel_bias)
        Lc = kc.shape[1]
        kbuf = jnp.concatenate([kc.astype(kb.dtype), kb], axis=1)[:, -Lc:]
        vbuf = jnp.concatenate([vc.astype(vb.dtype), vb], axis=1)[:, -Lc:]
    yb = (ob.reshape(B, T, WB) * jax.nn.silu(gb)) @ w_branch_b

    m = jax.nn.sigmoid(mga) * ya + jax.nn.sigmoid(mgb) * yb
    x = x + m @ w_out
    x = x + jax.nn.sigmoid(_rmsnorm(x, ple_norm_g) @ w_ple_gate) * (p @ w_ple)
    return x, sa.astype(x.dtype), kbuf, vbuf


def setup_inputs(seed: int = 0) -> dict:
    key = jax.random.key(seed)
    ks = jax.random.split(key, 24)
    L_WIN = min(WINDOW, PAST_LEN)
    f32 = jnp.float32
    nrm = lambda k, shape, s=1.0: (jax.random.normal(k, shape, f32) * s)
    return {
        'x_prompt': nrm(ks[0], (BATCH, SEQ, D_MODEL)),
        'x_sample': nrm(ks[1], (DEC_BATCH, DEC_SEQ, D_MODEL)),
        'state_gla': nrm(ks[2], (DEPTH, DEC_BATCH, HA, DK_A, DV_A)),
        'cache_band_k': nrm(ks[3], (DEPTH, DEC_BATCH, L_WIN, HB, DH_B)),
        'cache_band_v': nrm(ks[4], (DEPTH, DEC_BATCH, L_WIN, HB, DH_B)),
        'p_prompt': nrm(ks[5], (DEPTH, BATCH, SEQ, P_DIM)),
        'p_sample': nrm(ks[6], (DEPTH, DEC_BATCH, DEC_SEQ, P_DIM)),
        'norm_g': 1.0 + nrm(ks[7], (DEPTH, D_MODEL), 0.02),
        'w_in': nrm(ks[8], (DEPTH, D_MODEL, N_IN), D_MODEL ** -0.5),
        'w_gate_up': nrm(ks[9], (DEPTH, GATE_RANK, WA_K), GATE_RANK ** -0.5),
        'b_gate': nrm(ks[10], (DEPTH, WA_K), 0.01),
        'gla_norm_g': 1.0 + nrm(ks[11], (DEPTH, DV_A), 0.02),
        'q_norm_g': 1.0 + nrm(ks[12], (DEPTH, DH_B), 0.02),
        'k_norm_g': 1.0 + nrm(ks[13], (DEPTH, DH_B), 0.02),
        'rel_bias': nrm(ks[14], (DEPTH, HB, 2 * MAX_REL + 1), 0.1),
        'w_branch_a': nrm(ks[15], (DEPTH, WA_V, D_MODEL), WA_V ** -0.5),
        'w_branch_b': nrm(ks[16], (DEPTH, WB, D_MODEL), WB ** -0.5),
        'w_out': nrm(ks[17], (DEPTH, D_MODEL, D_MODEL), D_MODEL ** -0.5),
        'ple_norm_g': 1.0 + nrm(ks[18], (DEPTH, D_MODEL), 0.02),
        'w_ple_gate': nrm(ks[19], (DEPTH, D_MODEL, D_MODEL), D_MODEL ** -0.5),
        'w_ple': nrm(ks[20], (DEPTH, P_DIM, D_MODEL), P_DIM ** -0.5),
    }


def reference(x_prompt, x_sample, state_gla, cache_band_k, cache_band_v, p_prompt, p_sample,
              norm_g, w_in, w_gate_up, b_gate, gla_norm_g, q_norm_g, k_norm_g, rel_bias,
              w_branch_a, w_branch_b, w_out, ple_norm_g, w_ple_gate, w_ple):
    xp = x_prompt
    xs = x_sample
    sp_list, kp_list, vp_list = [], [], []
    ss_list, ksl, vsl = [], [], []
    for i in range(DEPTH):
        lw = (norm_g[i], w_in[i], w_gate_up[i], b_gate[i], gla_norm_g[i], q_norm_g[i],
              k_norm_g[i], rel_bias[i], w_branch_a[i], w_branch_b[i], w_out[i],
              ple_norm_g[i], w_ple_gate[i], w_ple[i])
        xp, sp, kbp, vbp = _layer(xp, p_prompt[i], None, None, None, lw, True)
        xs, ss, kbs, vbs = _layer(xs, p_sample[i], state_gla[i], cache_band_k[i],
                                  cache_band_v[i], lw, False)
        sp_list.append(sp); kp_list.append(kbp); vp_list.append(vbp)
        ss_list.append(ss); ksl.append(kbs); vsl.append(vbs)
    new_state_gla_prompt = jnp.stack(sp_list)
    new_band_k_prompt = jnp.stack(kp_list)
    new_band_v_prompt = jnp.stack(vp_list)
    new_state_gla_sample = jnp.stack(ss_list)
    new_band_k_sample = jnp.stack(ksl)
    new_band_v_sample = jnp.stack(vsl)
    return (xp, xs, new_state_gla_prompt, new_band_k_prompt, new_band_v_prompt,
            new_state_gla_sample, new_band_k_sample, new_band_v_sample)
```

```python
import functools

import numpy as np
import jax
import jax.numpy as jnp
from jax import lax
from jax.experimental import pallas as pl
from jax.experimental.pallas import tpu as pltpu

D_MODEL = 1024
CHUNK = 64
P_DIM = 256
EPS = 1e-6
NEG = -1e30
HA = 4
DK_A = 128
DV_A = 256
GATE_RANK = 16
GATE_TAU = 16.0
WA_K = HA * DK_A
WA_V = HA * DV_A
HB = 16
DH_B = 64
LEFT_CHUNKS = 8
WINDOW = LEFT_CHUNKS * CHUNK
MAX_REL = 128
WB = HB * DH_B

LANES = 128
GATE_PAD = LANES
SUB = 16
VMEM_LIMIT = 56 * 1024 * 1024

F32 = jnp.float32
BF16 = jnp.bfloat16


def _dot(a, b):
    return jnp.dot(a, b, preferred_element_type=F32)


def _dot_nt(a, b):
    return lax.dot_general(a, b, (((1,), (1,)), ((), ())), preferred_element_type=F32)


def _resident(shape):
    return pl.BlockSpec(shape, lambda *_: (0,) * len(shape), pipeline_mode=pl.Buffered(1))


def _params(sem):
    return pltpu.CompilerParams(dimension_semantics=sem, vmem_limit_bytes=VMEM_LIMIT)


def _in_proj_kernel(x_ref, ng_ref, wqa, wka, wva, wra, wga, wqb, wkb, wvb, wgb, wmga, wmgb,
                    wgu, bg_ref, hsum, qg_ref, kg_ref,
                    qa_o, ka_o, va_o, lg_o, sga_o, qb_o, kb_o, vb_o, sgb_o, sma_o, smb_o):
    x = x_ref[...]
    ms = jnp.mean(x * x, axis=-1, keepdims=True)
    h = (x * lax.rsqrt(ms + EPS) * ng_ref[...]).astype(BF16)

    qa_o[...] = _dot(h, wqa[...]) * (DK_A ** -0.5)
    ka_o[...] = _dot(h, wka[...])
    va_o[...] = _dot(h, wva[...])

    ra = _dot(h, wra[...])
    gl = _dot(ra.astype(BF16), wgu[...]) + bg_ref[...]
    lg_o[...] = (jnp.minimum(gl, 0.0) - jnp.log(1.0 + jnp.exp(-jnp.abs(gl)))) * (1.0 / GATE_TAU)

    ga = _dot(h, wga[...])
    sga_o[...] = ga * jax.nn.sigmoid(ga)

    def head_norm(z, g_ref):
        ss = _dot((z * z).astype(BF16), hsum[...])
        return z * lax.rsqrt(ss * (1.0 / DH_B) + EPS) * g_ref[...]

    qb_o[...] = head_norm(_dot(h, wqb[...]), qg_ref) * (DH_B ** -0.5)
    kb_o[...] = head_norm(_dot(h, wkb[...]), kg_ref)
    vb_o[...] = _dot(h, wvb[...])
    gb = _dot(h, wgb[...])
    sgb_o[...] = gb * jax.nn.sigmoid(gb)
    sma_o[...] = jax.nn.sigmoid(_dot(h, wmga[...]))
    smb_o[...] = jax.nn.sigmoid(_dot(h, wmgb[...]))


def _in_proj(x2d, lw, tm):
    rows = x2d.shape[0]
    row = lambda w: pl.BlockSpec((tm, w), lambda i: (i, 0))
    widths = (WA_K, WA_K, WA_V, WA_K, WA_V, WB, WB, WB, WB, D_MODEL, D_MODEL)
    weights = (lw['ng'], lw['wqa'], lw['wka'], lw['wva'], lw['wra'], lw['wga'], lw['wqb'], lw['wkb'],
               lw['wvb'], lw['wgb'], lw['wmga'], lw['wmgb'], lw['wgu'], lw['bg'], lw['hsum'],
               lw['qg'], lw['kg'])
    return pl.pallas_call(
        _in_proj_kernel,
        grid=(rows // tm,),
        in_specs=[row(D_MODEL)] + [_resident(w.shape) for w in weights],
        out_specs=[row(w) for w in widths],
        out_shape=[jax.ShapeDtypeStruct((rows, w), F32) for w in widths],
        compiler_params=_params(("parallel",)),
        name="in_proj",
    )(x2d, *weights)


def _gla_kernel(q_ref, k_ref, lg_ref, v_ref, s0_ref, o_ref, s_out_ref, st_ref, *, n_chunks):
    c = pl.program_id(1)

    @pl.when(c == 0)
    def _():
        st_ref[...] = s0_ref[...]

    r = lax.broadcasted_iota(jnp.int32, (CHUNK, CHUNK), 0)
    cc = lax.broadcasted_iota(jnp.int32, (CHUNK, CHUNK), 1)
    causal = cc <= r
    ltri = jnp.where(causal, 1.0, 0.0).astype(BF16)
    krow = lax.broadcasted_iota(jnp.int32, (CHUNK, DK_A), 0)

    def chunk_body(ci, carry):
        rows = pl.ds(pl.multiple_of(ci * CHUNK, CHUNK), CHUNK)
        lg = lg_ref[rows, :]
        lg_hi = lg.astype(BF16)
        lg_lo = (lg - lg_hi.astype(F32)).astype(BF16)
        b_all = _dot(ltri, lg_hi) + _dot(ltri, lg_lo)
        for hh in range(HA):
            ks = slice(hh * DK_A, (hh + 1) * DK_A)
            vs = slice(hh * DV_A, (hh + 1) * DV_A)
            b = b_all[:, ks]
            q = q_ref[rows, ks]
            k = k_ref[rows, ks]
            v = v_ref[rows, vs]
            v16 = v.astype(BF16)
            b_last = b[CHUNK - 1:CHUNK, :]
            st = st_ref[hh]
            o = _dot_nt((q * jnp.exp(b)).astype(BF16), st.astype(BF16))
            blocks = []
            for sb in range(CHUNK // SUB):
                lo, hi = sb * SUB, (sb + 1) * SUB
                ref_b = b[lo:lo + 1, :]
                qt = q[lo:hi] * jnp.exp(b[lo:hi] - ref_b)
                kt = k * jnp.exp(jnp.where(krow < hi, ref_b - b, 0.0))
                blocks.append(_dot_nt(qt.astype(BF16), kt.astype(BF16)))
            a = jnp.where(causal, jnp.concatenate(blocks, axis=0), 0.0)
            o = o + _dot(a.astype(BF16), v16)
            o_ref[rows, vs] = o
            k_dec = (k * jnp.exp(b_last - b)).astype(BF16)
            st_ref[hh] = st * jnp.exp(b_last) + _dot(v.T.astype(BF16), k_dec)
        return carry

    lax.fori_loop(0, n_chunks, chunk_body, 0)

    @pl.when(c == pl.num_programs(1) - 1)
    def _():
        s_out_ref[...] = st_ref[...]


def _gla(q, k, lg, v, s0t, tb):
    bsz, t, _ = q.shape
    kspec = pl.BlockSpec((None, tb, WA_K), lambda b, c: (b, c, 0))
    vspec = pl.BlockSpec((None, tb, WA_V), lambda b, c: (b, c, 0))
    sspec = pl.BlockSpec((None, HA, DV_A, DK_A), lambda b, c: (b, 0, 0, 0))
    return pl.pallas_call(
        functools.partial(_gla_kernel, n_chunks=tb // CHUNK),
        grid=(bsz, t // tb),
        in_specs=[kspec, kspec, kspec, vspec, sspec],
        out_specs=[vspec, sspec],
        out_shape=[jax.ShapeDtypeStruct((bsz, t, WA_V), F32),
                   jax.ShapeDtypeStruct((bsz, HA, DV_A, DK_A), F32)],
        scratch_shapes=[pltpu.VMEM((HA, DV_A, DK_A), F32)],
        compiler_params=_params(("parallel", "arbitrary")),
        name="gla",
    )(q, k, lg, v, s0t)


def _band_kernel(*refs, n_pieces, piece_w, first_block):
    q_ref = refs[0]
    k_refs = refs[1:1 + n_pieces]
    v_refs = refs[1 + n_pieces:1 + 2 * n_pieces]
    bias_ref = refs[1 + 2 * n_pieces]
    o_ref = refs[2 + 2 * n_pieces]
    tq = q_ref.shape[0]
    i = pl.program_id(1)
    low = lax.broadcasted_iota(jnp.int32, (tq, LANES), 1) < DH_B

    for hp in range(HB // 2):
        ls = slice(hp * LANES, (hp + 1) * LANES)
        q2 = q_ref[:, ls]
        k2 = [kr[:, ls].astype(BF16) for kr in k_refs]
        v2 = [vr[:, ls].astype(BF16) for vr in v_refs]
        outs = []
        for e in range(2):
            hd = 2 * hp + e
            qm = jnp.where(low if e == 0 else jnp.logical_not(low), q2, 0.0).astype(BF16)
            s = []
            for p in range(n_pieces):
                sp = _dot_nt(qm, k2[p]) + bias_ref[hd, :, p * piece_w:(p + 1) * piece_w]
                if first_block is not None and p < n_pieces - 1:
                    sp = jnp.where(i + (p - (n_pieces - 1)) >= 0, sp, NEG)
                s.append(sp)
            m = functools.reduce(jnp.maximum, [jnp.max(sp, axis=-1, keepdims=True) for sp in s])
            ex = [jnp.exp(sp - m) for sp in s]
            den = functools.reduce(jnp.add, [jnp.sum(e_, axis=-1, keepdims=True) for e_ in ex])
            acc = functools.reduce(jnp.add, [_dot(ex[p].astype(BF16), v2[p]) for p in range(n_pieces)])
            outs.append(acc / den)
        o_ref[:, ls] = jnp.where(low, outs[0], outs[1])


def _band_prompt(q, k, v, bias, tq):
    bsz, t, _ = q.shape
    n_pieces = WINDOW // tq + 1
    qspec = pl.BlockSpec((None, tq, WB), lambda b, i: (b, i, 0))

    def kspec(p):
        back = n_pieces - 1 - p
        return pl.BlockSpec((None, tq, WB), lambda b, i: (b, jnp.maximum(i - back, 0), 0))

    kspecs = [kspec(p) for p in range(n_pieces)]
    return pl.pallas_call(
        functools.partial(_band_kernel, n_pieces=n_pieces, piece_w=tq, first_block=True),
        grid=(bsz, t // tq),
        in_specs=[qspec] + kspecs + kspecs + [_resident(bias.shape)],
        out_specs=qspec,
        out_shape=jax.ShapeDtypeStruct((bsz, t, WB), F32),
        compiler_params=_params(("parallel", "arbitrary")),
        name="band_prompt",
    )(q, *([k] * n_pieces), *([v] * n_pieces), bias)


def _band_sample(q, kk, vv, bias):
    bsz, s, _ = q.shape
    lk = kk.shape[1]
    qspec = pl.BlockSpec((None, s, WB), lambda b, i: (b, 0, 0))
    kspec = pl.BlockSpec((None, lk, WB), lambda b, i: (b, 0, 0))
    return pl.pallas_call(
        functools.partial(_band_kernel, n_pieces=1, piece_w=lk, first_block=None),
        grid=(bsz, 1),
        in_specs=[qspec, kspec, kspec, _resident(bias.shape)],
        out_specs=qspec,
        out_shape=jax.ShapeDtypeStruct((bsz, s, WB), F32),
        compiler_params=_params(("parallel", "arbitrary")),
        name="band_sample",
    )(q, kk, vv, bias)


def _post_kernel(oa_ref, sga_ref, ob_ref, sgb_ref, sma_ref, smb_ref, x_ref, p_ref,
                 gg_ref, wa, wb, wo, pg_ref, wpg, wp, out_ref):
    oa = oa_ref[...]
    segs = []
    for hh in range(HA):
        seg = oa[:, hh * DV_A:(hh + 1) * DV_A]
        ms = jnp.mean(seg * seg, axis=-1, keepdims=True)
        segs.append(seg * lax.rsqrt(ms + EPS) * gg_ref[...])
    oan = jnp.concatenate(segs, axis=-1)
    ya = _dot((oan * sga_ref[...]).astype(BF16), wa[...])
    yb = _dot((ob_ref[...] * sgb_ref[...]).astype(BF16), wb[...])
    m = sma_ref[...] * ya + smb_ref[...] * yb
    x1 = x_ref[...] + _dot(m.astype(BF16), wo[...])
    ms = jnp.mean(x1 * x1, axis=-1, keepdims=True)
    hn = (x1 * lax.rsqrt(ms + EPS) * pg_ref[...]).astype(BF16)
    gate = jax.nn.sigmoid(_dot(hn, wpg[...]))
    out_ref[...] = x1 + gate * _dot(p_ref[...].astype(BF16), wp[...])


def _post(oa, sga, ob, sgb, sma, smb, x2d, p2d, lw, tm):
    rows = x2d.shape[0]
    row = lambda w: pl.BlockSpec((tm, w), lambda i: (i, 0))
    weights = (lw['gg'], lw['wa'], lw['wb'], lw['wo'], lw['pg'], lw['wpg'], lw['wp'])
    return pl.pallas_call(
        _post_kernel,
        grid=(rows // tm,),
        in_specs=[row(D_MODEL)] * 7 + [row(P_DIM)] + [_resident(w.shape) for w in weights],
        out_specs=row(D_MODEL),
        out_shape=jax.ShapeDtypeStruct((rows, D_MODEL), F32),
        compiler_params=_params(("parallel",)),
        name="post",
    )(oa, sga, ob, sgb, sma, smb, x2d, p2d, *weights)


def _layer_weights(i, norm_g, w_in, w_gate_up, b_gate, gla_norm_g, q_norm_g, k_norm_g, rel_bias,
                   w_branch_a, w_branch_b, w_out, ple_norm_g, w_ple_gate, w_ple, tq):
    sizes = (WA_K, WA_K, WA_V, GATE_RANK, WA_V, WB, WB, WB, WB, D_MODEL, D_MODEL)
    offs = np.concatenate([[0], np.cumsum(sizes)])
    names = ('wqa', 'wka', 'wva', 'wra', 'wga', 'wqb', 'wkb', 'wvb', 'wgb', 'wmga', 'wmgb')
    w16 = w_in[i].astype(BF16)
    lw = {n: w16[:, int(offs[j]):int(offs[j + 1])] for j, n in enumerate(names)}
    lw['wra'] = jnp.pad(lw['wra'], ((0, 0), (0, GATE_PAD - GATE_RANK)))
    lw['wgu'] = jnp.pad(w_gate_up[i].astype(BF16), ((0, GATE_PAD - GATE_RANK), (0, 0)))
    lw['bg'] = b_gate[i].reshape(1, WA_K)
    lw['ng'] = norm_g[i].reshape(1, D_MODEL)
    head = np.arange(WB) // DH_B
    lw['hsum'] = jnp.asarray(head[:, None] == head[None, :], dtype=BF16)
    lw['qg'] = jnp.tile(q_norm_g[i], HB).reshape(1, WB)
    lw['kg'] = jnp.tile(k_norm_g[i], HB).reshape(1, WB)
    lw['gg'] = gla_norm_g[i].reshape(1, DV_A)
    lw['wa'] = w_branch_a[i].astype(BF16)
    lw['wb'] = w_branch_b[i].astype(BF16)
    lw['wo'] = w_out[i].astype(BF16)
    lw['pg'] = ple_norm_g[i].reshape(1, D_MODEL)
    lw['wpg'] = w_ple_gate[i].astype(BF16)
    lw['wp'] = w_ple[i].astype(BF16)
    qpos = WINDOW + np.arange(tq)[:, None]
    kpos = np.arange(WINDOW + tq)[None, :]
    dchunk = qpos // CHUNK - kpos // CHUNK
    visible = (dchunk >= 0) & (dchunk <= LEFT_CHUNKS)
    idx = np.clip(qpos - kpos, -MAX_REL, MAX_REL) + MAX_REL
    lw['bias'] = jnp.where(jnp.asarray(visible)[None], rel_bias[i][:, idx], NEG).astype(F32)
    return lw


def _layer(x, p, s0, kc, vc, lw, tm, tb, tq):
    bsz, t, _ = x.shape
    x2d = x.reshape(bsz * t, D_MODEL)
    qa, ka, va, lg, sga, qb, kb, vb, sgb, sma, smb = _in_proj(x2d, lw, tm)
    r3 = lambda a: a.reshape(bsz, t, a.shape[-1])
    if s0 is None:
        s0t = jnp.zeros((bsz, HA, DV_A, DK_A), F32)
    else:
        s0t = jnp.swapaxes(s0, -1, -2)
    oa, st = _gla(r3(qa), r3(ka), r3(lg), r3(va), s0t, tb)
    kb3, vb3 = r3(kb), r3(vb)
    if kc is None:
        ob = _band_prompt(r3(qb), kb3, vb3, lw['bias'], tq)
        keep = min(WINDOW, t)
        kbuf, vbuf = kb3[:, t - keep:], vb3[:, t - keep:]
    else:
        lc = kc.shape[1]
        kk = jnp.concatenate([kc.reshape(bsz, lc, WB), kb3], axis=1)
        vv = jnp.concatenate([vc.reshape(bsz, lc, WB), vb3], axis=1)
        ob = _band_sample(r3(qb), kk, vv, lw['bias'][:, :t, :lc + t])
        kbuf, vbuf = kk[:, -lc:], vv[:, -lc:]
    x_new = _post(oa.reshape(bsz * t, WA_V), sga, ob.reshape(bsz * t, WB), sgb, sma, smb, x2d,
                  p.reshape(bsz * t, P_DIM), lw, tm)
    heads = lambda a: a.reshape(bsz, a.shape[1], HB, DH_B)
    return x_new.reshape(bsz, t, D_MODEL), jnp.swapaxes(st, -1, -2), heads(kbuf), heads(vbuf)


def kernel(x_prompt, x_sample, state_gla, cache_band_k, cache_band_v, p_prompt, p_sample,
           norm_g, w_in, w_gate_up, b_gate, gla_norm_g, q_norm_g, k_norm_g, rel_bias,
           w_branch_a, w_branch_b, w_out, ple_norm_g, w_ple_gate, w_ple):
    depth = w_in.shape[0]
    tq = 256
    xp, xs = x_prompt, x_sample
    outs = [[] for _ in range(6)]
    for i in range(depth):
        lw = _layer_weights(i, norm_g, w_in, w_gate_up, b_gate, gla_norm_g, q_norm_g, k_norm_g, rel_bias,
                            w_branch_a, w_branch_b, w_out, ple_norm_g, w_ple_gate, w_ple, tq)
        xp, sp, kbp, vbp = _layer(xp, p_prompt[i], None, None, None, lw, tm=256, tb=512, tq=tq)
        xs, ss, kbs, vbs = _layer(xs, p_sample[i], state_gla[i], cache_band_k[i], cache_band_v[i], lw,
                                  tm=256, tb=CHUNK, tq=tq)
        for lst, a in zip(outs, (sp, kbp, vbp, ss, kbs, vbs)):
            lst.append(a)
    return (xp, xs) + tuple(jnp.stack(lst) for lst in outs)
```

```python
import functools

import numpy as np
import jax
import jax.numpy as jnp
from jax import lax
from jax.experimental import pallas as pl
from jax.experimental.pallas import tpu as pltpu

D_MODEL = 1024
CHUNK = 64
P_DIM = 256
EPS = 1e-6
NEG = -1e30
HA = 4
DK_A = 128
DV_A = 256
GATE_RANK = 16
GATE_TAU = 16.0
WA_K = HA * DK_A
WA_V = HA * DV_A
HB = 16
DH_B = 64
LEFT_CHUNKS = 8
WINDOW = LEFT_CHUNKS * CHUNK
MAX_REL = 128
WB = HB * DH_B

LANES = 128
GATE_PAD = LANES
SUB = 16
VMEM_LIMIT = 56 * 1024 * 1024
LOG2E = 1.4426950408889634

F32 = jnp.float32
BF16 = jnp.bfloat16


def _dot(a, b):
    return jnp.dot(a, b, preferred_element_type=F32)


def _dot_nt(a, b):
    return lax.dot_general(a, b, (((1,), (1,)), ((), ())), preferred_element_type=F32)


def _resident(shape):
    return pl.BlockSpec(shape, lambda *_: (0,) * len(shape), pipeline_mode=pl.Buffered(1))


def _params(sem):
    return pltpu.CompilerParams(dimension_semantics=sem, vmem_limit_bytes=VMEM_LIMIT)


def _in_proj_kernel(x_ref, ng_ref, wqa, wka, wva, wra, wga, wqb, wkb, wvb, wgb, wmga, wmgb,
                    wgu, bg_ref, hsum, qg_ref, kg_ref,
                    qa_o, ka_o, va_o, lg_o, sga_o, qb_o, kb_o, vb_o, sgb_o, sma_o, smb_o, kb16_o, vb16_o):
    x = x_ref[...]
    ms = jnp.mean(x * x, axis=-1, keepdims=True)
    h = (x * lax.rsqrt(ms + EPS) * ng_ref[...]).astype(BF16)

    qa_o[...] = _dot(h, wqa[...]) * (DK_A ** -0.5)
    ka_o[...] = _dot(h, wka[...])
    va_o[...] = _dot(h, wva[...])

    ra = _dot(h, wra[...])
    gl = _dot(ra.astype(BF16), wgu[...]) + bg_ref[...]
    lg_o[...] = (jnp.minimum(gl, 0.0) - jnp.log(1.0 + jnp.exp(-jnp.abs(gl)))) * (1.0 / GATE_TAU)

    ga = _dot(h, wga[...])
    sga_o[...] = ga * jax.nn.sigmoid(ga)

    def head_norm(z, g_ref):
        ss = _dot((z * z).astype(BF16), hsum[...])
        return z * lax.rsqrt(ss * (1.0 / DH_B) + EPS) * g_ref[...]

    qb_o[...] = (head_norm(_dot(h, wqb[...]), qg_ref) * (DH_B ** -0.5 * LOG2E)).astype(BF16)
    kb = head_norm(_dot(h, wkb[...]), kg_ref)
    kb_o[...] = kb
    kb16_o[...] = kb.astype(BF16)
    vb = _dot(h, wvb[...])
    vb_o[...] = vb
    vb16_o[...] = vb.astype(BF16)
    gb = _dot(h, wgb[...])
    sgb_o[...] = gb * jax.nn.sigmoid(gb)
    sma_o[...] = jax.nn.sigmoid(_dot(h, wmga[...]))
    smb_o[...] = jax.nn.sigmoid(_dot(h, wmgb[...]))


def _in_proj(x2d, lw, tm):
    rows = x2d.shape[0]
    row = lambda w: pl.BlockSpec((tm, w), lambda i: (i, 0))
    widths = (WA_K, WA_K, WA_V, WA_K, WA_V, WB, WB, WB, WB, D_MODEL, D_MODEL, WB, WB)
    dtypes = (F32, F32, F32, F32, F32, BF16, F32, F32, F32, F32, F32, BF16, BF16)
    weights = (lw['ng'], lw['wqa'], lw['wka'], lw['wva'], lw['wra'], lw['wga'], lw['wqb'], lw['wkb'],
               lw['wvb'], lw['wgb'], lw['wmga'], lw['wmgb'], lw['wgu'], lw['bg'], lw['hsum'],
               lw['qg'], lw['kg'])
    return pl.pallas_call(
        _in_proj_kernel,
        grid=(rows // tm,),
        in_specs=[row(D_MODEL)] + [_resident(w.shape) for w in weights],
        out_specs=[row(w) for w in widths],
        out_shape=[jax.ShapeDtypeStruct((rows, w), dt) for w, dt in zip(widths, dtypes)],
        compiler_params=_params(("parallel",)),
        name="in_proj",
    )(x2d, *weights)


def _gla_kernel(q_ref, k_ref, lg_ref, v_ref, s0_ref, o_ref, s_out_ref, st_ref, *, n_chunks):
    c = pl.program_id(1)

    @pl.when(c == 0)
    def _():
        st_ref[...] = s0_ref[...]

    r = lax.broadcasted_iota(jnp.int32, (CHUNK, CHUNK), 0)
    cc = lax.broadcasted_iota(jnp.int32, (CHUNK, CHUNK), 1)
    causal = cc <= r
    ltri = jnp.where(causal, 1.0, 0.0).astype(BF16)
    krow = lax.broadcasted_iota(jnp.int32, (CHUNK, DK_A), 0)

    def chunk_body(ci, carry):
        rows = pl.ds(pl.multiple_of(ci * CHUNK, CHUNK), CHUNK)
        lg = lg_ref[rows, :]
        lg_hi = lg.astype(BF16)
        lg_lo = (lg - lg_hi.astype(F32)).astype(BF16)
        b_all = _dot(ltri, lg_hi) + _dot(ltri, lg_lo)
        for hh in range(HA):
            ks = slice(hh * DK_A, (hh + 1) * DK_A)
            vs = slice(hh * DV_A, (hh + 1) * DV_A)
            b = b_all[:, ks]
            q = q_ref[rows, ks]
            k = k_ref[rows, ks]
            v = v_ref[rows, vs]
            v16 = v.astype(BF16)
            b_last = b[CHUNK - 1:CHUNK, :]
            st = st_ref[hh]
            o = _dot_nt((q * jnp.exp(b)).astype(BF16), st.astype(BF16))
            blocks = []
            for sb in range(CHUNK // SUB):
                lo, hi = sb * SUB, (sb + 1) * SUB
                ref_b = b[lo:lo + 1, :]
                qt = q[lo:hi] * jnp.exp(b[lo:hi] - ref_b)
                kt = k * jnp.exp(jnp.where(krow < hi, ref_b - b, 0.0))
                blocks.append(_dot_nt(qt.astype(BF16), kt.astype(BF16)))
            a = jnp.where(causal, jnp.concatenate(blocks, axis=0), 0.0)
            o = o + _dot(a.astype(BF16), v16)
            o_ref[rows, vs] = o
            k_dec = (k * jnp.exp(b_last - b)).astype(BF16)
            st_ref[hh] = st * jnp.exp(b_last) + _dot(v.T.astype(BF16), k_dec)
        return carry

    lax.fori_loop(0, n_chunks, chunk_body, 0)

    @pl.when(c == pl.num_programs(1) - 1)
    def _():
        s_out_ref[...] = st_ref[...]


def _gla(q, k, lg, v, s0t, tb):
    bsz, t, _ = q.shape
    kspec = pl.BlockSpec((None, tb, WA_K), lambda b, c: (b, c, 0))
    vspec = pl.BlockSpec((None, tb, WA_V), lambda b, c: (b, c, 0))
    sspec = pl.BlockSpec((None, HA, DV_A, DK_A), lambda b, c: (b, 0, 0, 0))
    return pl.pallas_call(
        functools.partial(_gla_kernel, n_chunks=tb // CHUNK),
        grid=(bsz, t // tb),
        in_specs=[kspec, kspec, kspec, vspec, sspec],
        out_specs=[vspec, sspec],
        out_shape=[jax.ShapeDtypeStruct((bsz, t, WA_V), F32),
                   jax.ShapeDtypeStruct((bsz, HA, DV_A, DK_A), F32)],
        scratch_shapes=[pltpu.VMEM((HA, DV_A, DK_A), F32)],
        compiler_params=_params(("parallel", "arbitrary")),
        name="gla",
    )(q, k, lg, v, s0t)


def _band_kernel(*refs, n_pieces, piece_w):
    q_ref = refs[0]
    k_refs = refs[1:1 + n_pieces]
    v_refs = refs[1 + n_pieces:1 + 2 * n_pieces]
    bias_ref = refs[1 + 2 * n_pieces]
    o_ref = refs[2 + 2 * n_pieces]
    tq = q_ref.shape[0]

    def half_masks(rows):
        lane = lax.broadcasted_iota(jnp.int32, (rows, LANES), 1)
        lo = jnp.where(lane < DH_B, 1.0, 0.0)
        return lo.astype(BF16), (1.0 - lo).astype(BF16)

    q_sel = half_masks(tq)
    k_sel = half_masks(piece_w)

    for hp in range(HB // 2):
        ls = slice(hp * LANES, (hp + 1) * LANES)
        q2 = q_ref[:, ls]
        k2 = [kr[:, ls] for kr in k_refs]
        v2 = [vr[:, ls] for vr in v_refs]
        acc = None
        for e in range(2):
            hd = 2 * hp + e
            qm = q2 * q_sel[e]
            s = [_dot_nt(qm, k2[p]) + bias_ref[hd, :, p * piece_w:(p + 1) * piece_w] for p in range(n_pieces)]
            m = jnp.max(functools.reduce(jnp.maximum, s), axis=-1, keepdims=True)
            for p in range(n_pieces):
                ex = jnp.exp2(s[p] - m).astype(BF16)
                v_aug = jnp.concatenate([v2[p] * k_sel[e], k_sel[e]], axis=1)
                part = _dot(ex, v_aug)
                acc = part if acc is None else acc + part
        o_ref[:, ls] = acc[:, :LANES] / acc[:, LANES:]


def _band_prompt(q, k, v, bias, tq):
    bsz, t, _ = q.shape
    n_pieces = WINDOW // tq + 1
    qspec = pl.BlockSpec((None, tq, WB), lambda i, b: (b, i, 0))

    def kspec(p):
        back = n_pieces - 1 - p
        return pl.BlockSpec((None, tq, WB), lambda i, b: (b, jnp.maximum(i - back, 0), 0))

    kspecs = [kspec(p) for p in range(n_pieces)]
    bspec = pl.BlockSpec((None,) + bias.shape[1:], lambda i, b: (jnp.minimum(i, n_pieces - 1), 0, 0, 0))
    return pl.pallas_call(
        functools.partial(_band_kernel, n_pieces=n_pieces, piece_w=tq),
        grid=(t // tq, bsz),
        in_specs=[qspec] + kspecs + kspecs + [bspec],
        out_specs=qspec,
        out_shape=jax.ShapeDtypeStruct((bsz, t, WB), F32),
        compiler_params=_params(("arbitrary", "arbitrary")),
        name="band_prompt",
    )(q, *([k] * n_pieces), *([v] * n_pieces), bias)


def _band_sample(q, kk, vv, bias):
    bsz, s, _ = q.shape
    lk = kk.shape[1]
    qspec = pl.BlockSpec((None, s, WB), lambda b: (b, 0, 0))
    kspec = pl.BlockSpec((None, lk, WB), lambda b: (b, 0, 0))
    return pl.pallas_call(
        functools.partial(_band_kernel, n_pieces=1, piece_w=lk),
        grid=(bsz,),
        in_specs=[qspec, kspec, kspec, _resident(bias.shape)],
        out_specs=qspec,
        out_shape=jax.ShapeDtypeStruct((bsz, s, WB), F32),
        compiler_params=_params(("arbitrary",)),
        name="band_sample",
    )(q, kk, vv, bias)


def _post_kernel(oa_ref, sga_ref, ob_ref, sgb_ref, sma_ref, smb_ref, x_ref, p_ref,
                 gg_ref, wa, wb, wo, pg_ref, wpg, wp, out_ref):
    oa = oa_ref[...]
    segs = []
    for hh in range(HA):
        seg = oa[:, hh * DV_A:(hh + 1) * DV_A]
        ms = jnp.mean(seg * seg, axis=-1, keepdims=True)
        segs.append(seg * lax.rsqrt(ms + EPS) * gg_ref[...])
    oan = jnp.concatenate(segs, axis=-1)
    ya = _dot((oan * sga_ref[...]).astype(BF16), wa[...])
    yb = _dot((ob_ref[...] * sgb_ref[...]).astype(BF16), wb[...])
    m = sma_ref[...] * ya + smb_ref[...] * yb
    x1 = x_ref[...] + _dot(m.astype(BF16), wo[...])
    ms = jnp.mean(x1 * x1, axis=-1, keepdims=True)
    hn = (x1 * lax.rsqrt(ms + EPS) * pg_ref[...]).astype(BF16)
    gate = jax.nn.sigmoid(_dot(hn, wpg[...]))
    out_ref[...] = x1 + gate * _dot(p_ref[...].astype(BF16), wp[...])


def _post(oa, sga, ob, sgb, sma, smb, x2d, p2d, lw, tm):
    rows = x2d.shape[0]
    row = lambda w: pl.BlockSpec((tm, w), lambda i: (i, 0))
    weights = (lw['gg'], lw['wa'], lw['wb'], lw['wo'], lw['pg'], lw['wpg'], lw['wp'])
    return pl.pallas_call(
        _post_kernel,
        grid=(rows // tm,),
        in_specs=[row(D_MODEL)] * 7 + [row(P_DIM)] + [_resident(w.shape) for w in weights],
        out_specs=row(D_MODEL),
        out_shape=jax.ShapeDtypeStruct((rows, D_MODEL), F32),
        compiler_params=_params(("parallel",)),
        name="post",
    )(oa, sga, ob, sgb, sma, smb, x2d, p2d, *weights)


def _layer_weights(i, norm_g, w_in, w_gate_up, b_gate, gla_norm_g, q_norm_g, k_norm_g, rel_bias,
                   w_branch_a, w_branch_b, w_out, ple_norm_g, w_ple_gate, w_ple, tq):
    sizes = (WA_K, WA_K, WA_V, GATE_RANK, WA_V, WB, WB, WB, WB, D_MODEL, D_MODEL)
    offs = np.concatenate([[0], np.cumsum(sizes)])
    names = ('wqa', 'wka', 'wva', 'wra', 'wga', 'wqb', 'wkb', 'wvb', 'wgb', 'wmga', 'wmgb')
    w16 = w_in[i].astype(BF16)
    lw = {n: w16[:, int(offs[j]):int(offs[j + 1])] for j, n in enumerate(names)}
    lw['wra'] = jnp.pad(lw['wra'], ((0, 0), (0, GATE_PAD - GATE_RANK)))
    lw['wgu'] = jnp.pad(w_gate_up[i].astype(BF16), ((0, GATE_PAD - GATE_RANK), (0, 0)))
    lw['bg'] = b_gate[i].reshape(1, WA_K)
    lw['ng'] = norm_g[i].reshape(1, D_MODEL)
    head = np.arange(WB) // DH_B
    lw['hsum'] = jnp.asarray(head[:, None] == head[None, :], dtype=BF16)
    lw['qg'] = jnp.tile(q_norm_g[i], HB).reshape(1, WB)
    lw['kg'] = jnp.tile(k_norm_g[i], HB).reshape(1, WB)
    lw['gg'] = gla_norm_g[i].reshape(1, DV_A)
    lw['wa'] = w_branch_a[i].astype(BF16)
    lw['wb'] = w_branch_b[i].astype(BF16)
    lw['wo'] = w_out[i].astype(BF16)
    lw['pg'] = ple_norm_g[i].reshape(1, D_MODEL)
    lw['wpg'] = w_ple_gate[i].astype(BF16)
    lw['wp'] = w_ple[i].astype(BF16)
    lw['bias'] = _band_bias(rel_bias[i], tq)
    return lw


def _band_bias(tab, tq):
    nk = WINDOW + tq
    period = nk + tq
    n_hi = WINDOW - MAX_REL
    far = jnp.broadcast_to(tab[:, -1:], (HB, n_hi))
    near = jnp.broadcast_to(tab[:, :1], (HB, max(nk - n_hi - (2 * MAX_REL + 1), 0)))
    wrap = jnp.broadcast_to(tab[:, -1:], (HB, period - nk))
    body = jnp.concatenate([far, tab[:, ::-1], near], axis=1)[:, :nk]
    vec = jnp.concatenate([body, wrap], axis=1)
    toep = jnp.tile(vec, (1, tq))[:, :tq * (period - 1)].reshape(HB, tq, period - 1)[:, :, :nk]
    qpos = WINDOW + np.arange(tq)[:, None]
    kpos = np.arange(nk)[None, :]
    dchunk = qpos // CHUNK - kpos // CHUNK
    visible = (dchunk >= 0) & (dchunk <= LEFT_CHUNKS)
    n_blocks = nk // tq
    variants = np.stack([visible & (kpos >= (n_blocks - 1 - j) * tq) for j in range(n_blocks)])
    return jnp.where(jnp.asarray(variants)[:, None], (toep * LOG2E)[None], NEG).astype(F32)


def _layer(x, p, s0, kc, vc, lw, tm, tb, tq):
    bsz, t, _ = x.shape
    x2d = x.reshape(bsz * t, D_MODEL)
    qa, ka, va, lg, sga, qb, kb, vb, sgb, sma, smb, kb16, vb16 = _in_proj(x2d, lw, tm)
    r3 = lambda a: a.reshape(bsz, t, a.shape[-1])
    if s0 is None:
        s0t = jnp.zeros((bsz, HA, DV_A, DK_A), F32)
    else:
        s0t = jnp.swapaxes(s0, -1, -2)
    oa, st = _gla(r3(qa), r3(ka), r3(lg), r3(va), s0t, tb)
    kb3, vb3 = r3(kb), r3(vb)
    if kc is None:
        ob = _band_prompt(r3(qb), r3(kb16), r3(vb16), lw['bias'], tq)
        keep = min(WINDOW, t)
        kbuf, vbuf = kb3[:, t - keep:], vb3[:, t - keep:]
    else:
        lc = kc.shape[1]
        kc2, vc2 = kc.reshape(bsz, lc, WB), vc.reshape(bsz, lc, WB)
        kk16 = jnp.concatenate([kc2.astype(BF16), r3(kb16)], axis=1)
        vv16 = jnp.concatenate([vc2.astype(BF16), r3(vb16)], axis=1)
        ob = _band_sample(r3(qb), kk16, vv16, lw['bias'][-1, :, :t, :lc + t])
        kbuf = jnp.concatenate([kc2, kb3], axis=1)[:, -lc:]
        vbuf = jnp.concatenate([vc2, vb3], axis=1)[:, -lc:]
    x_new = _post(oa.reshape(bsz * t, WA_V), sga, ob.reshape(bsz * t, WB), sgb, sma, smb, x2d,
                  p.reshape(bsz * t, P_DIM), lw, tm)
    heads = lambda a: a.reshape(bsz, a.shape[1], HB, DH_B)
    return x_new.reshape(bsz, t, D_MODEL), jnp.swapaxes(st, -1, -2), heads(kbuf), heads(vbuf)


def kernel(x_prompt, x_sample, state_gla, cache_band_k, cache_band_v, p_prompt, p_sample,
           norm_g, w_in, w_gate_up, b_gate, gla_norm_g, q_norm_g, k_norm_g, rel_bias,
           w_branch_a, w_branch_b, w_out, ple_norm_g, w_ple_gate, w_ple):
    depth = w_in.shape[0]
    tq = 256
    xp, xs = x_prompt, x_sample
    outs = [[] for _ in range(6)]
    for i in range(depth):
        lw = _layer_weights(i, norm_g, w_in, w_gate_up, b_gate, gla_norm_g, q_norm_g, k_norm_g, rel_bias,
                            w_branch_a, w_branch_b, w_out, ple_norm_g, w_ple_gate, w_ple, tq)
        xp, sp, kbp, vbp = _layer(xp, p_prompt[i], None, None, None, lw, tm=256, tb=512, tq=tq)
        xs, ss, kbs, vbs = _layer(xs, p_sample[i], state_gla[i], cache_band_k[i], cache_band_v[i], lw,
                                  tm=256, tb=CHUNK, tq=tq)
        for lst, a in zip(outs, (sp, kbp, vbp, ss, kbs, vbs)):
            lst.append(a)
    return (xp, xs) + tuple(jnp.stack(lst) for lst in outs)
```

```python
import functools

import numpy as np
import jax
import jax.numpy as jnp
from jax import lax
from jax.experimental import pallas as pl
from jax.experimental.pallas import tpu as pltpu

D_MODEL = 1024
CHUNK = 64
P_DIM = 256
EPS = 1e-6
NEG = -1e30
HA = 4
DK_A = 128
DV_A = 256
GATE_RANK = 16
GATE_TAU = 16.0
WA_K = HA * DK_A
WA_V = HA * DV_A
HB = 16
DH_B = 64
LEFT_CHUNKS = 8
WINDOW = LEFT_CHUNKS * CHUNK
MAX_REL = 128
WB = HB * DH_B

LANES = 128
GATE_PAD = LANES
SUB = 16
VMEM_LIMIT = 56 * 1024 * 1024
LOG2E = 1.4426950408889634

F32 = jnp.float32
BF16 = jnp.bfloat16


def _dot(a, b):
    return jnp.dot(a, b, preferred_element_type=F32)


def _dot_nt(a, b):
    return lax.dot_general(a, b, (((1,), (1,)), ((), ())), preferred_element_type=F32)


def _resident(shape):
    return pl.BlockSpec(shape, lambda *_: (0,) * len(shape), pipeline_mode=pl.Buffered(1))


def _params(sem):
    return pltpu.CompilerParams(dimension_semantics=sem, vmem_limit_bytes=VMEM_LIMIT)


def _in_proj_kernel(x_ref, ng_ref, wqa, wka, wva, wra, wga, wqb, wkb, wvb, wgb, wmga, wmgb,
                    wgu, bg_ref, qg_ref, kg_ref,
                    qa_o, ka_o, va_o, lg_o, sga_o, qb_o, kb_o, vb_o, sgb_o, sma_o, smb_o, kt_o, vt_o):
    x = x_ref[...]
    ms = jnp.mean(x * x, axis=-1, keepdims=True)
    h = (x * lax.rsqrt(ms + EPS) * ng_ref[...]).astype(BF16)

    qa_o[...] = _dot(h, wqa[...]) * (DK_A ** -0.5)
    ka_o[...] = _dot(h, wka[...])
    va_o[...] = _dot(h, wva[...]).astype(BF16)

    ra = _dot(h, wra[...])
    gl = _dot(ra.astype(BF16), wgu[...]) + bg_ref[...]
    lg_o[...] = (jnp.minimum(gl, 0.0) - jnp.log(1.0 + jnp.exp(-jnp.abs(gl)))) * (1.0 / GATE_TAU)

    ga = _dot(h, wga[...])
    sga_o[...] = (ga * jax.nn.sigmoid(ga)).astype(BF16)

    low = lax.broadcasted_iota(jnp.int32, (x.shape[0], LANES), 1) < DH_B

    def head_norm(z, g_ref):
        zz = z * z
        scales = []
        for c in range(WB // LANES):
            t = zz[:, c * LANES:(c + 1) * LANES]
            ss_lo = jnp.sum(jnp.where(low, t, 0.0), axis=-1, keepdims=True)
            ss_hi = jnp.sum(jnp.where(low, 0.0, t), axis=-1, keepdims=True)
            scales.append(jnp.where(low, lax.rsqrt(ss_lo * (1.0 / DH_B) + EPS),
                                    lax.rsqrt(ss_hi * (1.0 / DH_B) + EPS)))
        return z * jnp.concatenate(scales, axis=1) * g_ref[...]

    qb_o[...] = (head_norm(_dot(h, wqb[...]), qg_ref) * (DH_B ** -0.5 * LOG2E)).astype(BF16)
    kb = head_norm(_dot(h, wkb[...]), kg_ref)
    kb_o[...] = kb.astype(BF16)
    kt_o[...] = kb
    vb = _dot(h, wvb[...])
    vb_o[...] = vb.astype(BF16)
    vt_o[...] = vb
    gb = _dot(h, wgb[...])
    sgb_o[...] = (gb * jax.nn.sigmoid(gb)).astype(BF16)
    sma_o[...] = jax.nn.sigmoid(_dot(h, wmga[...])).astype(BF16)
    smb_o[...] = jax.nn.sigmoid(_dot(h, wmgb[...])).astype(BF16)


def _in_proj(x2d, lw, tm, seq, keep):
    rows = x2d.shape[0]
    row = lambda w: pl.BlockSpec((tm, w), lambda i: (i, 0))
    widths = (WA_K, WA_K, WA_V, WA_K, WA_V, WB, WB, WB, WB, D_MODEL, D_MODEL)
    dtypes = (F32, F32, BF16, F32, BF16, BF16, BF16, BF16, BF16, BF16, BF16)
    if keep == seq:
        tail = row(WB)
    else:
        assert keep % tm == 0 and seq % tm == 0
        per_seq, per_tail = seq // tm, keep // tm
        tail = pl.BlockSpec((tm, WB), lambda i: ((i // per_seq) * per_tail
                                                 + jnp.maximum(i % per_seq - (per_seq - per_tail), 0), 0))
    tail_shape = jax.ShapeDtypeStruct((rows // seq * keep, WB), F32)
    weights = (lw['ng'], lw['wqa'], lw['wka'], lw['wva'], lw['wra'], lw['wga'], lw['wqb'], lw['wkb'],
               lw['wvb'], lw['wgb'], lw['wmga'], lw['wmgb'], lw['wgu'], lw['bg'], lw['qg'], lw['kg'])
    return pl.pallas_call(
        _in_proj_kernel,
        grid=(rows // tm,),
        in_specs=[row(D_MODEL)] + [_resident(w.shape) for w in weights],
        out_specs=[row(w) for w in widths] + [tail, tail],
        out_shape=[jax.ShapeDtypeStruct((rows, w), dt) for w, dt in zip(widths, dtypes)] + [tail_shape] * 2,
        compiler_params=_params(("arbitrary",)),
        name="in_proj",
    )(x2d, *weights)


def _gla_kernel(q_ref, k_ref, lg_ref, v_ref, s0_ref, o_ref, s_out_ref, st_ref, *, n_chunks):
    c = pl.program_id(1)

    @pl.when(c == 0)
    def _():
        st_ref[...] = s0_ref[...]

    tb = n_chunks * CHUNK
    n_sub = CHUNK // SUB
    shift = CHUNK.bit_length() - 1
    r = lax.broadcasted_iota(jnp.int32, (tb, tb), 0)
    cc = lax.broadcasted_iota(jnp.int32, (tb, tb), 1)
    same_chunk = lax.shift_right_logical(r, shift) == lax.shift_right_logical(cc, shift)
    ltri = jnp.where(same_chunk & (cc <= r), 1.0, 0.0).astype(BF16)
    lg = lg_ref[...]
    lg_hi = lg.astype(BF16)
    lg_lo = (lg - lg_hi.astype(F32)).astype(BF16)
    b_all = _dot(ltri, lg_hi) + _dot(ltri, lg_lo)

    ar = lax.broadcasted_iota(jnp.int32, (CHUNK, n_sub * CHUNK), 0)
    ac = lax.broadcasted_iota(jnp.int32, (CHUNK, n_sub * CHUNK), 1)
    sub_shift = SUB.bit_length() - 1
    a_keep = (lax.shift_right_logical(ar, sub_shift) == lax.shift_right_logical(ac, shift)) \
        & ((ac & (CHUNK - 1)) <= ar)
    krow = lax.broadcasted_iota(jnp.int32, (CHUNK, DK_A), 0)

    states = [st_ref[hh] for hh in range(HA)]
    for ci in range(n_chunks):
        rows = slice(ci * CHUNK, (ci + 1) * CHUNK)
        for hh in range(HA):
            ks = slice(hh * DK_A, (hh + 1) * DK_A)
            vs = slice(hh * DV_A, (hh + 1) * DV_A)
            b = b_all[rows, ks]
            q = q_ref[rows, ks]
            k = k_ref[rows, ks]
            v16 = v_ref[rows, vs]
            b_last = b[CHUNK - 1:CHUNK, :]
            o = _dot((q * jnp.exp(b)).astype(BF16), states[hh].astype(BF16))
            refs_b = [b[sb * SUB:sb * SUB + 1, :] for sb in range(n_sub)]
            ref_rows = jnp.concatenate([jnp.broadcast_to(rb, (SUB, DK_A)) for rb in refs_b], axis=0)
            qt = (q * jnp.exp(b - ref_rows)).astype(BF16)
            kt = jnp.concatenate(
                [(k * jnp.exp(jnp.where(krow < (sb + 1) * SUB, refs_b[sb] - b, 0.0))).astype(BF16)
                 for sb in range(n_sub)], axis=0)
            a = jnp.where(a_keep, _dot_nt(qt, kt), 0.0).astype(BF16)
            o_ref[rows, vs] = o + _dot(a, jnp.concatenate([v16] * n_sub, axis=0))
            k_dec_t = (k * jnp.exp(b_last - b)).T
            decay = jnp.exp(b.T[:, CHUNK - 1:CHUNK])
            states[hh] = states[hh] * decay + _dot(k_dec_t.astype(BF16), v16)
    for hh in range(HA):
        st_ref[hh] = states[hh]

    @pl.when(c == pl.num_programs(1) - 1)
    def _():
        s_out_ref[...] = st_ref[...]


def _gla(q, k, lg, v, s0, tb):
    bsz, t, _ = q.shape
    kspec = pl.BlockSpec((None, tb, WA_K), lambda b, c: (b, c, 0))
    vspec = pl.BlockSpec((None, tb, WA_V), lambda b, c: (b, c, 0))
    sspec = pl.BlockSpec((None, HA, DK_A, DV_A), lambda b, c: (b, 0, 0, 0))
    return pl.pallas_call(
        functools.partial(_gla_kernel, n_chunks=tb // CHUNK),
        grid=(bsz, t // tb),
        in_specs=[kspec, kspec, kspec, vspec, sspec],
        out_specs=[vspec, sspec],
        out_shape=[jax.ShapeDtypeStruct((bsz, t, WA_V), F32),
                   jax.ShapeDtypeStruct((bsz, HA, DK_A, DV_A), F32)],
        scratch_shapes=[pltpu.VMEM((HA, DK_A, DV_A), F32)],
        compiler_params=_params(("arbitrary", "arbitrary")),
        name="gla",
    )(q, k, lg, v, s0)


def _band_kernel(*refs, n_pieces, piece_w):
    q_ref = refs[0]
    k_refs = refs[1:1 + n_pieces]
    v_refs = refs[1 + n_pieces:1 + 2 * n_pieces]
    bias_ref = refs[1 + 2 * n_pieces]
    o_ref = refs[2 + 2 * n_pieces]
    tq = q_ref.shape[0]

    def half_masks(rows):
        lane = lax.broadcasted_iota(jnp.int32, (rows, LANES), 1)
        lo = jnp.where(lane < DH_B, 1.0, 0.0)
        return lo.astype(BF16), (1.0 - lo).astype(BF16)

    q_sel = half_masks(tq)
    k_sel = half_masks(piece_w)

    for hp in range(HB // 2):
        ls = slice(hp * LANES, (hp + 1) * LANES)
        q2 = q_ref[:, ls]
        k2 = [kr[:, ls] for kr in k_refs]
        v2 = [vr[:, ls] for vr in v_refs]
        acc = None
        for e in range(2):
            hd = 2 * hp + e
            qm = q2 * q_sel[e]
            s = [_dot_nt(qm, k2[p]) + bias_ref[hd, :, p * piece_w:(p + 1) * piece_w] for p in range(n_pieces)]
            m = jnp.max(functools.reduce(jnp.maximum, s), axis=-1, keepdims=True)
            for p in range(n_pieces):
                ex = jnp.exp2(s[p] - m).astype(BF16)
                v_aug = jnp.concatenate([v2[p] * k_sel[e], k_sel[e]], axis=1)
                part = _dot(ex, v_aug)
                acc = part if acc is None else acc + part
        o_ref[:, ls] = acc[:, :LANES] / acc[:, LANES:]


def _band_prompt(q, k, v, bias, tq):
    bsz, t, _ = q.shape
    n_pieces = WINDOW // tq + 1
    qspec = pl.BlockSpec((None, tq, WB), lambda i, b: (b, i, 0))

    def kspec(p):
        back = n_pieces - 1 - p
        return pl.BlockSpec((None, tq, WB), lambda i, b: (b, jnp.maximum(i - back, 0), 0))

    kspecs = [kspec(p) for p in range(n_pieces)]
    bspec = pl.BlockSpec((None,) + bias.shape[1:], lambda i, b: (jnp.minimum(i, n_pieces - 1), 0, 0, 0))
    return pl.pallas_call(
        functools.partial(_band_kernel, n_pieces=n_pieces, piece_w=tq),
        grid=(t // tq, bsz),
        in_specs=[qspec] + kspecs + kspecs + [bspec],
        out_specs=qspec,
        out_shape=jax.ShapeDtypeStruct((bsz, t, WB), F32),
        compiler_params=_params(("arbitrary", "arbitrary")),
        name="band_prompt",
    )(q, *([k] * n_pieces), *([v] * n_pieces), bias)


def _band_sample(q, kk, vv, bias):
    bsz, s, _ = q.shape
    lk = kk.shape[1]
    qspec = pl.BlockSpec((None, s, WB), lambda b: (b, 0, 0))
    kspec = pl.BlockSpec((None, lk, WB), lambda b: (b, 0, 0))
    return pl.pallas_call(
        functools.partial(_band_kernel, n_pieces=1, piece_w=lk),
        grid=(bsz,),
        in_specs=[qspec, kspec, kspec, _resident(bias.shape)],
        out_specs=qspec,
        out_shape=jax.ShapeDtypeStruct((bsz, s, WB), F32),
        compiler_params=_params(("arbitrary",)),
        name="band_sample",
    )(q, kk, vv, bias)


def _post_kernel(oa_ref, sga_ref, ob_ref, sgb_ref, sma_ref, smb_ref, x_ref, p_ref,
                 gg_ref, wa, wb, wo, pg_ref, wpg, wp, out_ref):
    oa = oa_ref[...]
    segs = []
    for hh in range(HA):
        seg = oa[:, hh * DV_A:(hh + 1) * DV_A]
        ms = jnp.mean(seg * seg, axis=-1, keepdims=True)
        segs.append(seg * lax.rsqrt(ms + EPS) * gg_ref[...])
    oan = jnp.concatenate(segs, axis=-1)
    ya = _dot((oan * sga_ref[...]).astype(BF16), wa[...])
    yb = _dot((ob_ref[...] * sgb_ref[...]).astype(BF16), wb[...])
    m = sma_ref[...] * ya + smb_ref[...] * yb
    x1 = x_ref[...] + _dot(m.astype(BF16), wo[...])
    ms = jnp.mean(x1 * x1, axis=-1, keepdims=True)
    hn = (x1 * lax.rsqrt(ms + EPS) * pg_ref[...]).astype(BF16)
    gate = jax.nn.sigmoid(_dot(hn, wpg[...]))
    out_ref[...] = x1 + gate * _dot(p_ref[...].astype(BF16), wp[...])


def _post(oa, sga, ob, sgb, sma, smb, x2d, p2d, lw, tm):
    rows = x2d.shape[0]
    row = lambda w: pl.BlockSpec((tm, w), lambda i: (i, 0))
    weights = (lw['gg'], lw['wa'], lw['wb'], lw['wo'], lw['pg'], lw['wpg'], lw['wp'])
    return pl.pallas_call(
        _post_kernel,
        grid=(rows // tm,),
        in_specs=[row(D_MODEL)] * 7 + [row(P_DIM)] + [_resident(w.shape) for w in weights],
        out_specs=row(D_MODEL),
        out_shape=jax.ShapeDtypeStruct((rows, D_MODEL), F32),
        compiler_params=_params(("parallel",)),
        name="post",
    )(oa, sga, ob, sgb, sma, smb, x2d, p2d, *weights)


def _layer_weights(i, norm_g, w_in, w_gate_up, b_gate, gla_norm_g, q_norm_g, k_norm_g, rel_bias,
                   w_branch_a, w_branch_b, w_out, ple_norm_g, w_ple_gate, w_ple, tq):
    sizes = (WA_K, WA_K, WA_V, GATE_RANK, WA_V, WB, WB, WB, WB, D_MODEL, D_MODEL)
    offs = np.concatenate([[0], np.cumsum(sizes)])
    names = ('wqa', 'wka', 'wva', 'wra', 'wga', 'wqb', 'wkb', 'wvb', 'wgb', 'wmga', 'wmgb')
    w16 = w_in[i].astype(BF16)
    lw = {n: w16[:, int(offs[j]):int(offs[j + 1])] for j, n in enumerate(names)}
    lw['wra'] = jnp.pad(lw['wra'], ((0, 0), (0, GATE_PAD - GATE_RANK)))
    lw['wgu'] = jnp.pad(w_gate_up[i].astype(BF16), ((0, GATE_PAD - GATE_RANK), (0, 0)))
    lw['bg'] = b_gate[i].reshape(1, WA_K)
    lw['ng'] = norm_g[i].reshape(1, D_MODEL)
    lw['qg'] = jnp.tile(q_norm_g[i], HB).reshape(1, WB)
    lw['kg'] = jnp.tile(k_norm_g[i], HB).reshape(1, WB)
    lw['gg'] = gla_norm_g[i].reshape(1, DV_A)
    lw['wa'] = w_branch_a[i].astype(BF16)
    lw['wb'] = w_branch_b[i].astype(BF16)
    lw['wo'] = w_out[i].astype(BF16)
    lw['pg'] = ple_norm_g[i].reshape(1, D_MODEL)
    lw['wpg'] = w_ple_gate[i].astype(BF16)
    lw['wp'] = w_ple[i].astype(BF16)
    lw['bias'] = _band_bias(rel_bias[i], tq)
    return lw


def _band_bias(tab, tq):
    nk = WINDOW + tq
    period = nk + tq
    n_hi = WINDOW - MAX_REL
    far = jnp.broadcast_to(tab[:, -1:], (HB, n_hi))
    near = jnp.broadcast_to(tab[:, :1], (HB, max(nk - n_hi - (2 * MAX_REL + 1), 0)))
    wrap = jnp.broadcast_to(tab[:, -1:], (HB, period - nk))
    body = jnp.concatenate([far, tab[:, ::-1], near], axis=1)[:, :nk]
    vec = jnp.concatenate([body, wrap], axis=1)
    toep = jnp.tile(vec, (1, tq))[:, :tq * (period - 1)].reshape(HB, tq, period - 1)[:, :, :nk]
    qpos = WINDOW + np.arange(tq)[:, None]
    kpos = np.arange(nk)[None, :]
    dchunk = qpos // CHUNK - kpos // CHUNK
    visible = (dchunk >= 0) & (dchunk <= LEFT_CHUNKS)
    n_blocks = nk // tq
    variants = np.stack([visible & (kpos >= (n_blocks - 1 - j) * tq) for j in range(n_blocks)])
    return jnp.where(jnp.asarray(variants)[:, None], (toep * LOG2E)[None], NEG).astype(F32)


def _layer(x, p, s0, kc, vc, lw, tm, tb, tq):
    bsz, t, _ = x.shape
    x2d = x.reshape(bsz * t, D_MODEL)
    keep = min(WINDOW, t)
    qa, ka, va, lg, sga, qb, kb, vb, sgb, sma, smb, ktail, vtail = _in_proj(x2d, lw, tm, t, keep)
    r3 = lambda a: a.reshape(bsz, -1, a.shape[-1])
    if s0 is None:
        s0 = jnp.zeros((bsz, HA, DK_A, DV_A), F32)
    oa, st = _gla(r3(qa), r3(ka), r3(lg), r3(va), s0, tb)
    if kc is None:
        ob = _band_prompt(r3(qb), r3(kb), r3(vb), lw['bias'], tq)
        kbuf, vbuf = r3(ktail), r3(vtail)
    else:
        lc = kc.shape[1]
        kc2, vc2 = kc.reshape(bsz, lc, WB), vc.reshape(bsz, lc, WB)
        kk16 = jnp.concatenate([kc2.astype(BF16), r3(kb)], axis=1)
        vv16 = jnp.concatenate([vc2.astype(BF16), r3(vb)], axis=1)
        ob = _band_sample(r3(qb), kk16, vv16, lw['bias'][-1, :, :t, :lc + t])
        kbuf = jnp.concatenate([kc2, r3(ktail)], axis=1)[:, -lc:]
        vbuf = jnp.concatenate([vc2, r3(vtail)], axis=1)[:, -lc:]
    x_new = _post(oa.reshape(bsz * t, WA_V), sga, ob.reshape(bsz * t, WB), sgb, sma, smb, x2d,
                  p.reshape(bsz * t, P_DIM), lw, tm)
    heads = lambda a: a.reshape(bsz, a.shape[1], HB, DH_B)
    return x_new.reshape(bsz, t, D_MODEL), st, heads(kbuf), heads(vbuf)


def kernel(x_prompt, x_sample, state_gla, cache_band_k, cache_band_v, p_prompt, p_sample,
           norm_g, w_in, w_gate_up, b_gate, gla_norm_g, q_norm_g, k_norm_g, rel_bias,
           w_branch_a, w_branch_b, w_out, ple_norm_g, w_ple_gate, w_ple):
    depth = w_in.shape[0]
    tq = 256
    xp, xs = x_prompt, x_sample
    outs = [[] for _ in range(6)]
    for i in range(depth):
        lw = _layer_weights(i, norm_g, w_in, w_gate_up, b_gate, gla_norm_g, q_norm_g, k_norm_g, rel_bias,
                            w_branch_a, w_branch_b, w_out, ple_norm_g, w_ple_gate, w_ple, tq)
        xp, sp, kbp, vbp = _layer(xp, p_prompt[i], None, None, None, lw, tm=512, tb=512, tq=tq)
        xs, ss, kbs, vbs = _layer(xs, p_sample[i], state_gla[i], cache_band_k[i], cache_band_v[i], lw,
                                  tm=512, tb=CHUNK, tq=tq)
        for lst, a in zip(outs, (sp, kbp, vbp, ss, kbs, vbs)):
            lst.append(a)
    return (xp, xs) + tuple(jnp.stack(lst) for lst in outs)
```

```python
import functools

import numpy as np
import jax
import jax.numpy as jnp
from jax import lax
from jax.experimental import pallas as pl
from jax.experimental.pallas import tpu as pltpu

D_MODEL = 1024
CHUNK = 64
P_DIM = 256
EPS = 1e-6
NEG = -1e30
HA = 4
DK_A = 128
DV_A = 256
GATE_RANK = 16
GATE_TAU = 16.0
WA_K = HA * DK_A
WA_V = HA * DV_A
HB = 16
DH_B = 64
LEFT_CHUNKS = 8
WINDOW = LEFT_CHUNKS * CHUNK
MAX_REL = 128
WB = HB * DH_B

LANES = 128
GATE_PAD = LANES
SUB = 16
VMEM_LIMIT = 56 * 1024 * 1024
LOG2E = 1.4426950408889634

F32 = jnp.float32
BF16 = jnp.bfloat16


def _dot(a, b):
    return jnp.dot(a, b, preferred_element_type=F32)


def _dot_nt(a, b):
    return lax.dot_general(a, b, (((1,), (1,)), ((), ())), preferred_element_type=F32)


def _resident(shape):
    return pl.BlockSpec(shape, lambda *_: (0,) * len(shape), pipeline_mode=pl.Buffered(1))


def _params(sem):
    return pltpu.CompilerParams(dimension_semantics=sem, vmem_limit_bytes=VMEM_LIMIT)


def _in_proj_kernel(x_ref, ng_ref, wqa, wka, wva, wra, wga, wqb, wkb, wvb, wgb, wmga, wmgb,
                    wgu, bg_ref, qg_ref, kg_ref,
                    qa_o, ka_o, va_o, lg_o, sga_o, qb_o, kb_o, vb_o, sgb_o, sma_o, smb_o, kt_o, vt_o):
    x = x_ref[...]
    ms = jnp.mean(x * x, axis=-1, keepdims=True)
    h = (x * lax.rsqrt(ms + EPS) * ng_ref[...]).astype(BF16)

    qa_o[...] = _dot(h, wqa[...]) * (DK_A ** -0.5)
    ka_o[...] = _dot(h, wka[...])
    va_o[...] = _dot(h, wva[...]).astype(BF16)

    ra = _dot(h, wra[...])
    gl = _dot(ra.astype(BF16), wgu[...]) + bg_ref[...]
    lg_o[...] = (jnp.minimum(gl, 0.0) - jnp.log(1.0 + jnp.exp(-jnp.abs(gl)))) * (1.0 / GATE_TAU)

    ga = _dot(h, wga[...])
    sga_o[...] = (ga * jax.nn.sigmoid(ga)).astype(BF16)

    low = lax.broadcasted_iota(jnp.int32, (x.shape[0], LANES), 1) < DH_B

    def head_norm(z, g_ref):
        zz = z * z
        scales = []
        for c in range(WB // LANES):
            t = zz[:, c * LANES:(c + 1) * LANES]
            ss_lo = jnp.sum(jnp.where(low, t, 0.0), axis=-1, keepdims=True)
            ss_hi = jnp.sum(jnp.where(low, 0.0, t), axis=-1, keepdims=True)
            scales.append(jnp.where(low, lax.rsqrt(ss_lo * (1.0 / DH_B) + EPS),
                                    lax.rsqrt(ss_hi * (1.0 / DH_B) + EPS)))
        return z * jnp.concatenate(scales, axis=1) * g_ref[...]

    qb_o[...] = (head_norm(_dot(h, wqb[...]), qg_ref) * (DH_B ** -0.5 * LOG2E)).astype(BF16)
    kb = head_norm(_dot(h, wkb[...]), kg_ref)
    kb_o[...] = kb.astype(BF16)
    kt_o[...] = kb
    vb = _dot(h, wvb[...])
    vb_o[...] = vb.astype(BF16)
    vt_o[...] = vb
    gb = _dot(h, wgb[...])
    sgb_o[...] = (gb * jax.nn.sigmoid(gb)).astype(BF16)
    sma_o[...] = jax.nn.sigmoid(_dot(h, wmga[...])).astype(BF16)
    smb_o[...] = jax.nn.sigmoid(_dot(h, wmgb[...])).astype(BF16)


def _in_proj(x2d, lw, tm, seq, keep):
    rows = x2d.shape[0]
    row = lambda w: pl.BlockSpec((tm, w), lambda i: (i, 0))
    widths = (WA_K, WA_K, WA_V, WA_K, WA_V, WB, WB, WB, WB, D_MODEL, D_MODEL)
    dtypes = (F32, F32, BF16, F32, BF16, BF16, BF16, BF16, BF16, BF16, BF16)
    if keep == seq:
        tail = row(WB)
    else:
        assert keep % tm == 0 and seq % tm == 0
        per_seq, per_tail = seq // tm, keep // tm
        tail = pl.BlockSpec((tm, WB), lambda i: ((i // per_seq) * per_tail
                                                 + jnp.maximum(i % per_seq - (per_seq - per_tail), 0), 0))
    tail_shape = jax.ShapeDtypeStruct((rows // seq * keep, WB), F32)
    weights = (lw['ng'], lw['wqa'], lw['wka'], lw['wva'], lw['wra'], lw['wga'], lw['wqb'], lw['wkb'],
               lw['wvb'], lw['wgb'], lw['wmga'], lw['wmgb'], lw['wgu'], lw['bg'], lw['qg'], lw['kg'])
    return pl.pallas_call(
        _in_proj_kernel,
        grid=(rows // tm,),
        in_specs=[row(D_MODEL)] + [_resident(w.shape) for w in weights],
        out_specs=[row(w) for w in widths] + [tail, tail],
        out_shape=[jax.ShapeDtypeStruct((rows, w), dt) for w, dt in zip(widths, dtypes)] + [tail_shape] * 2,
        compiler_params=_params(("arbitrary",)),
        name="in_proj",
    )(x2d, *weights)


def _gla_kernel(q_ref, k_ref, lg_ref, v_ref, s0_ref, o_ref, s_out_ref, st_ref, *, n_chunks):
    c = pl.program_id(1)

    @pl.when(c == 0)
    def _():
        st_ref[...] = s0_ref[...]

    tb = n_chunks * CHUNK
    n_sub = CHUNK // SUB
    shift = CHUNK.bit_length() - 1
    r = lax.broadcasted_iota(jnp.int32, (tb, tb), 0)
    cc = lax.broadcasted_iota(jnp.int32, (tb, tb), 1)
    same_chunk = lax.shift_right_logical(r, shift) == lax.shift_right_logical(cc, shift)
    ltri = jnp.where(same_chunk & (cc <= r), 1.0, 0.0).astype(BF16)
    lg = lg_ref[...]
    lg_hi = lg.astype(BF16)
    lg_lo = (lg - lg_hi.astype(F32)).astype(BF16)
    b_all = _dot(ltri, lg_hi) + _dot(ltri, lg_lo)

    ar = lax.broadcasted_iota(jnp.int32, (CHUNK, n_sub * CHUNK), 0)
    ac = lax.broadcasted_iota(jnp.int32, (CHUNK, n_sub * CHUNK), 1)
    sub_shift = SUB.bit_length() - 1
    a_keep = (lax.shift_right_logical(ar, sub_shift) == lax.shift_right_logical(ac, shift)) \
        & ((ac & (CHUNK - 1)) <= ar)
    krow = lax.broadcasted_iota(jnp.int32, (CHUNK, DK_A), 0)

    states = [st_ref[hh] for hh in range(HA)]
    for ci in range(n_chunks):
        rows = slice(ci * CHUNK, (ci + 1) * CHUNK)
        for hh in range(HA):
            ks = slice(hh * DK_A, (hh + 1) * DK_A)
            vs = slice(hh * DV_A, (hh + 1) * DV_A)
            b = b_all[rows, ks]
            q = q_ref[rows, ks]
            k = k_ref[rows, ks]
            v16 = v_ref[rows, vs]
            b_last = b[CHUNK - 1:CHUNK, :]
            o = _dot((q * jnp.exp(b)).astype(BF16), states[hh].astype(BF16))
            refs_b = [b[sb * SUB:sb * SUB + 1, :] for sb in range(n_sub)]
            ref_rows = jnp.concatenate([jnp.broadcast_to(rb, (SUB, DK_A)) for rb in refs_b], axis=0)
            qt = (q * jnp.exp(b - ref_rows)).astype(BF16)
            kt = jnp.concatenate(
                [(k * jnp.exp(jnp.where(krow < (sb + 1) * SUB, refs_b[sb] - b, 0.0))).astype(BF16)
                 for sb in range(n_sub)], axis=0)
            a = jnp.where(a_keep, _dot_nt(qt, kt), 0.0).astype(BF16)
            o_ref[rows, vs] = o + _dot(a, jnp.concatenate([v16] * n_sub, axis=0))
            k_dec_t = (k * jnp.exp(b_last - b)).T
            decay = jnp.exp(b.T[:, CHUNK - 1:CHUNK])
            states[hh] = states[hh] * decay + _dot(k_dec_t.astype(BF16), v16)
    for hh in range(HA):
        st_ref[hh] = states[hh]

    @pl.when(c == pl.num_programs(1) - 1)
    def _():
        s_out_ref[...] = st_ref[...]


def _gla(q, k, lg, v, s0, layer, tb):
    bsz, t, _ = q.shape
    kspec = pl.BlockSpec((None, tb, WA_K), lambda b, c: (b, c, 0))
    vspec = pl.BlockSpec((None, tb, WA_V), lambda b, c: (b, c, 0))
    sspec = pl.BlockSpec((None, HA, DK_A, DV_A), lambda b, c: (b, 0, 0, 0))
    s0spec = pl.BlockSpec((None, None, HA, DK_A, DV_A), lambda b, c: (layer, b, 0, 0, 0))
    return pl.pallas_call(
        functools.partial(_gla_kernel, n_chunks=tb // CHUNK),
        grid=(bsz, t // tb),
        in_specs=[kspec, kspec, kspec, vspec, s0spec],
        out_specs=[vspec, sspec],
        out_shape=[jax.ShapeDtypeStruct((bsz, t, WA_V), F32),
                   jax.ShapeDtypeStruct((bsz, HA, DK_A, DV_A), F32)],
        scratch_shapes=[pltpu.VMEM((HA, DK_A, DV_A), F32)],
        compiler_params=_params(("arbitrary", "arbitrary")),
        name="gla",
    )(q, k, lg, v, s0)


def _band_core(q_ref, k_refs, v_refs, bias_ref, o_ref):
    tq = q_ref.shape[0]
    widths = [kr.shape[0] for kr in k_refs]
    offs = [sum(widths[:p]) for p in range(len(widths) + 1)]
    n_pieces = len(widths)

    def half_masks(rows):
        lane = lax.broadcasted_iota(jnp.int32, (rows, LANES), 1)
        lo = jnp.where(lane < DH_B, 1.0, 0.0)
        return lo.astype(BF16), (1.0 - lo).astype(BF16)

    sel = {w: half_masks(w) for w in set(widths) | {tq}}

    for hp in range(HB // 2):
        ls = slice(hp * LANES, (hp + 1) * LANES)
        q2 = q_ref[:, ls]
        k2 = [kr[:, ls].astype(BF16) for kr in k_refs]
        v2 = [vr[:, ls].astype(BF16) for vr in v_refs]
        acc = None
        for e in range(2):
            hd = 2 * hp + e
            qm = q2 * sel[tq][e]
            s = [_dot_nt(qm, k2[p]) + bias_ref[hd, :, offs[p]:offs[p + 1]] for p in range(n_pieces)]
            if len(set(widths)) == 1:
                m = jnp.max(functools.reduce(jnp.maximum, s), axis=-1, keepdims=True)
            else:
                m = functools.reduce(jnp.maximum, [jnp.max(sp, axis=-1, keepdims=True) for sp in s])
            for p in range(n_pieces):
                ex = jnp.exp2(s[p] - m).astype(BF16)
                keep = sel[widths[p]][e]
                part = _dot(ex, jnp.concatenate([v2[p] * keep, keep], axis=1))
                acc = part if acc is None else acc + part
        o_ref[:, ls] = acc[:, :LANES] / acc[:, LANES:]


def _band_prompt_kernel(*refs, n_pieces):
    _band_core(refs[0], refs[1:1 + n_pieces], refs[1 + n_pieces:1 + 2 * n_pieces],
               refs[1 + 2 * n_pieces], refs[2 + 2 * n_pieces])


def _band_sample_kernel(q_ref, kc_ref, kn_ref, vc_ref, vn_ref, kt_ref, vt_ref, bias_ref, o_ref, ko_ref, vo_ref):
    _band_core(q_ref, [kc_ref, kn_ref], [vc_ref, vn_ref], bias_ref, o_ref)
    lc, s = kc_ref.shape[0], kt_ref.shape[0]
    for c_ref, t_ref, out_ref in ((kc_ref, kt_ref, ko_ref), (vc_ref, vt_ref, vo_ref)):
        out_ref[0:lc - s, :] = c_ref[s:lc, :]
        out_ref[lc - s:lc, :] = t_ref[...]


def _band_prompt(q, k, v, bias, layer, tq):
    bsz, t, _ = q.shape
    n_pieces = WINDOW // tq + 1
    qspec = pl.BlockSpec((None, tq, WB), lambda i, b: (b, i, 0))

    def kspec(p):
        back = n_pieces - 1 - p
        return pl.BlockSpec((None, tq, WB), lambda i, b: (b, jnp.maximum(i - back, 0), 0))

    kspecs = [kspec(p) for p in range(n_pieces)]
    bspec = pl.BlockSpec((None, None) + bias.shape[2:],
                         lambda i, b: (layer, jnp.minimum(i, n_pieces - 1), 0, 0, 0))
    return pl.pallas_call(
        functools.partial(_band_prompt_kernel, n_pieces=n_pieces),
        grid=(t // tq, bsz),
        in_specs=[qspec] + kspecs + kspecs + [bspec],
        out_specs=qspec,
        out_shape=jax.ShapeDtypeStruct((bsz, t, WB), F32),
        compiler_params=_params(("arbitrary", "arbitrary")),
        name="band_prompt",
    )(q, *([k] * n_pieces), *([v] * n_pieces), bias)


def _band_sample(q, kc_all, vc_all, layer, kn, vn, kt, vt, bias):
    bsz, s, _ = q.shape
    lc = kc_all.shape[2]
    assert lc > s
    nspec = pl.BlockSpec((None, s, WB), lambda b: (b, 0, 0))
    cspec = pl.BlockSpec((None, None, lc, WB), lambda b: (layer, b, 0, 0))
    ospec = pl.BlockSpec((None, lc, WB), lambda b: (b, 0, 0))
    buf = jax.ShapeDtypeStruct((bsz, lc, WB), F32)
    return pl.pallas_call(
        _band_sample_kernel,
        grid=(bsz,),
        in_specs=[nspec, cspec, nspec, cspec, nspec, nspec, nspec, _resident(bias.shape)],
        out_specs=[nspec, ospec, ospec],
        out_shape=[jax.ShapeDtypeStruct((bsz, s, WB), F32), buf, buf],
        compiler_params=_params(("arbitrary",)),
        name="band_sample",
    )(q, kc_all, kn, vc_all, vn, kt, vt, bias)


def _post_kernel(oa_ref, sga_ref, ob_ref, sgb_ref, sma_ref, smb_ref, x_ref, p_ref,
                 gg_ref, wa, wb, wo, pg_ref, wpg, wp, out_ref):
    oa = oa_ref[...]
    segs = []
    for hh in range(HA):
        seg = oa[:, hh * DV_A:(hh + 1) * DV_A]
        ms = jnp.mean(seg * seg, axis=-1, keepdims=True)
        segs.append(seg * lax.rsqrt(ms + EPS) * gg_ref[...])
    oan = jnp.concatenate(segs, axis=-1)
    ya = _dot((oan * sga_ref[...]).astype(BF16), wa[...])
    yb = _dot((ob_ref[...] * sgb_ref[...]).astype(BF16), wb[...])
    m = sma_ref[...] * ya + smb_ref[...] * yb
    x1 = x_ref[...] + _dot(m.astype(BF16), wo[...])
    ms = jnp.mean(x1 * x1, axis=-1, keepdims=True)
    hn = (x1 * lax.rsqrt(ms + EPS) * pg_ref[...]).astype(BF16)
    gate = jax.nn.sigmoid(_dot(hn, wpg[...]))
    out_ref[...] = x1 + gate * _dot(p_ref[...].astype(BF16), wp[...])


def _post(oa, sga, ob, sgb, sma, smb, x2d, p_all, layer, lw, tm):
    rows = x2d.shape[0]
    row = lambda w: pl.BlockSpec((tm, w), lambda i: (i, 0))
    pspec = pl.BlockSpec((None, tm, P_DIM), lambda i: (layer, i, 0))
    weights = (lw['gg'], lw['wa'], lw['wb'], lw['wo'], lw['pg'], lw['wpg'], lw['wp'])
    return pl.pallas_call(
        _post_kernel,
        grid=(rows // tm,),
        in_specs=[row(D_MODEL)] * 7 + [pspec] + [_resident(w.shape) for w in weights],
        out_specs=row(D_MODEL),
        out_shape=jax.ShapeDtypeStruct((rows, D_MODEL), F32),
        compiler_params=_params(("parallel",)),
        name="post",
    )(oa, sga, ob, sgb, sma, smb, x2d, p_all, *weights)


def _layer_weights(i, norm_g, w_in, w_gate_up, b_gate, gla_norm_g, q_norm_g, k_norm_g,
                   w_branch_a, w_branch_b, w_out, ple_norm_g, w_ple_gate, w_ple):
    sizes = (WA_K, WA_K, WA_V, GATE_RANK, WA_V, WB, WB, WB, WB, D_MODEL, D_MODEL)
    offs = np.concatenate([[0], np.cumsum(sizes)])
    names = ('wqa', 'wka', 'wva', 'wra', 'wga', 'wqb', 'wkb', 'wvb', 'wgb', 'wmga', 'wmgb')
    lw = {n: w_in[i, :, int(offs[j]):int(offs[j + 1])].astype(BF16) for j, n in enumerate(names)}
    lw['wra'] = jnp.pad(lw['wra'], ((0, 0), (0, GATE_PAD - GATE_RANK)))
    lw['wgu'] = jnp.pad(w_gate_up[i].astype(BF16), ((0, GATE_PAD - GATE_RANK), (0, 0)))
    lw['bg'] = b_gate[i].reshape(1, WA_K)
    lw['ng'] = norm_g[i].reshape(1, D_MODEL)
    lw['qg'] = jnp.tile(q_norm_g[i], HB).reshape(1, WB)
    lw['kg'] = jnp.tile(k_norm_g[i], HB).reshape(1, WB)
    lw['gg'] = gla_norm_g[i].reshape(1, DV_A)
    lw['wa'] = w_branch_a[i].astype(BF16)
    lw['wb'] = w_branch_b[i].astype(BF16)
    lw['wo'] = w_out[i].astype(BF16)
    lw['pg'] = ple_norm_g[i].reshape(1, D_MODEL)
    lw['wpg'] = w_ple_gate[i].astype(BF16)
    lw['wp'] = w_ple[i].astype(BF16)
    return lw


def _bias_kernel(vec_ref, o_ref, *, tq, nk):
    n_var = o_ref.shape[0]
    period = vec_ref.shape[-1]
    x = jnp.broadcast_to(vec_ref[...], (tq, period))
    toep = pltpu.roll(x, 0, 1, stride=1, stride_axis=0)[:, :nk]
    r = lax.broadcasted_iota(jnp.int32, (tq, nk), 0)
    c = lax.broadcasted_iota(jnp.int32, (tq, nk), 1)
    shift = CHUNK.bit_length() - 1
    dchunk = lax.shift_right_logical(r + WINDOW, shift) - lax.shift_right_logical(c, shift)
    visible = (dchunk >= 0) & (dchunk <= LEFT_CHUNKS)
    for j in range(n_var):
        o_ref[j] = jnp.where(visible & (c >= (n_var - 1 - j) * tq), toep, NEG)


def _band_bias(rel_bias, tq):
    depth = rel_bias.shape[0]
    nk = WINDOW + tq
    n_var = nk // tq
    period = nk + tq
    tab = rel_bias * LOG2E
    n_hi = WINDOW - MAX_REL
    far = jnp.broadcast_to(tab[..., -1:], (depth, HB, n_hi))
    near = jnp.broadcast_to(tab[..., :1], (depth, HB, max(nk - n_hi - (2 * MAX_REL + 1), 0)))
    wrap = jnp.broadcast_to(tab[..., -1:], (depth, HB, period - nk))
    body = jnp.concatenate([far, tab[..., ::-1], near], axis=-1)[..., :nk]
    vec = jnp.concatenate([body, wrap], axis=-1).reshape(depth, HB, 1, period)
    return pl.pallas_call(
        functools.partial(_bias_kernel, tq=tq, nk=nk),
        grid=(depth, HB),
        in_specs=[pl.BlockSpec((None, None, 1, period), lambda l, h: (l, h, 0, 0))],
        out_specs=pl.BlockSpec((None, n_var, None, tq, nk), lambda l, h: (l, 0, h, 0, 0)),
        out_shape=jax.ShapeDtypeStruct((depth, n_var, HB, tq, nk), F32),
        compiler_params=_params(("parallel", "parallel")),
        name="band_bias",
    )(vec)


def _layer(x, p_all, s0_all, kc, vc, bias, layer, lw, tm, tb, tq):
    bsz, t, _ = x.shape
    x2d = x.reshape(bsz * t, D_MODEL)
    keep = min(WINDOW, t)
    qa, ka, va, lg, sga, qb, kb, vb, sgb, sma, smb, ktail, vtail = _in_proj(x2d, lw, tm, t, keep)
    r3 = lambda a: a.reshape(bsz, -1, a.shape[-1])
    if s0_all is None:
        s0_all, s_layer = jnp.zeros((1, bsz, HA, DK_A, DV_A), F32), 0
    else:
        s_layer = layer
    oa, st = _gla(r3(qa), r3(ka), r3(lg), r3(va), s0_all, s_layer, tb)
    if kc is None:
        ob = _band_prompt(r3(qb), r3(kb), r3(vb), bias, layer, tq)
        kbuf, vbuf = r3(ktail), r3(vtail)
    else:
        lc = kc.shape[2]
        ob, kbuf, vbuf = _band_sample(r3(qb), kc, vc, layer, r3(kb), r3(vb), r3(ktail), r3(vtail),
                                      bias[layer, -1, :, :t, :lc + t])
    x_new = _post(oa.reshape(bsz * t, WA_V), sga, ob.reshape(bsz * t, WB), sgb, sma, smb, x2d,
                  p_all.reshape(p_all.shape[0], bsz * t, P_DIM), layer, lw, tm)
    return x_new.reshape(bsz, t, D_MODEL), st, kbuf, vbuf


def kernel(x_prompt, x_sample, state_gla, cache_band_k, cache_band_v, p_prompt, p_sample,
           norm_g, w_in, w_gate_up, b_gate, gla_norm_g, q_norm_g, k_norm_g, rel_bias,
           w_branch_a, w_branch_b, w_out, ple_norm_g, w_ple_gate, w_ple):
    depth = w_in.shape[0]
    tq = 256
    bias = _band_bias(rel_bias, tq)
    xp, xs = x_prompt, x_sample
    kc = cache_band_k.reshape(cache_band_k.shape[:3] + (WB,))
    vc = cache_band_v.reshape(cache_band_v.shape[:3] + (WB,))
    outs = [[] for _ in range(6)]
    for i in range(depth):
        lw = _layer_weights(i, norm_g, w_in, w_gate_up, b_gate, gla_norm_g, q_norm_g, k_norm_g,
                            w_branch_a, w_branch_b, w_out, ple_norm_g, w_ple_gate, w_ple)
        xp, sp, kbp, vbp = _layer(xp, p_prompt, None, None, None, bias, i, lw, tm=512, tb=512, tq=tq)
        xs, ss, kbs, vbs = _layer(xs, p_sample, state_gla, kc, vc, bias, i, lw, tm=512, tb=CHUNK, tq=tq)
        for lst, a in zip(outs, (sp, kbp, vbp, ss, kbs, vbs)):
            lst.append(a)
    sp, kbp, vbp, ss, kbs, vbs = (jnp.stack(lst) for lst in outs)
    heads = lambda a: a.reshape(a.shape[:-1] + (HB, DH_B))
    return (xp, xs, sp, heads(kbp), heads(vbp), ss, heads(kbs), heads(vbs))
```

```python
import functools

import numpy as np
import jax
import jax.numpy as jnp
from jax import lax
from jax.experimental import pallas as pl
from jax.experimental.pallas import tpu as pltpu

D_MODEL = 1024
CHUNK = 64
P_DIM = 256
EPS = 1e-6
NEG = -1e30
HA = 4
DK_A = 128
DV_A = 256
GATE_RANK = 16
GATE_TAU = 16.0
WA_K = HA * DK_A
WA_V = HA * DV_A
HB = 16
DH_B = 64
LEFT_CHUNKS = 8
WINDOW = LEFT_CHUNKS * CHUNK
MAX_REL = 128
WB = HB * DH_B

LANES = 128
GATE_PAD = LANES
SUB = 16
VMEM_LIMIT = 60 * 1024 * 1024
LOG2E = 1.4426950408889634

F32 = jnp.float32
BF16 = jnp.bfloat16


def _dot(a, b):
    return jnp.dot(a, b, preferred_element_type=F32)


def _dot_nt(a, b):
    return lax.dot_general(a, b, (((1,), (1,)), ((), ())), preferred_element_type=F32)


def _resident(shape):
    return pl.BlockSpec(shape, lambda *_: (0,) * len(shape), pipeline_mode=pl.Buffered(1))


def _params(sem):
    return pltpu.CompilerParams(dimension_semantics=sem, vmem_limit_bytes=VMEM_LIMIT)


_W_COLS = {}
_off = 0
for _name, _width in (('qa', WA_K), ('ka', WA_K), ('va', WA_V), ('ga', WA_V), ('qb', WB), ('kb', WB), ('vb', WB),
                      ('gb', WB), ('mga', D_MODEL), ('mgb', D_MODEL), ('ra', GATE_PAD)):
    _W_COLS[_name] = slice(_off, _off + _width)
    _off += _width
W_IN_COLS = _off


def _in_proj_kernel(x_ref, ng_ref, w_ref, wgu, bg_ref, qg_ref, kg_ref,
                    qa_o, ka_o, va_o, lg_o, sga_o, qb_o, kb_o, vb_o, sgb_o, sma_o, smb_o, kt_o, vt_o,
                    *, tail_steps):
    x = x_ref[...]
    ms = jnp.mean(x * x, axis=-1, keepdims=True)
    h = (x * lax.rsqrt(ms + EPS) * ng_ref[...]).astype(BF16)
    proj = lambda name: _dot(h, w_ref[:, _W_COLS[name]])

    qa_o[...] = proj('qa') * (DK_A ** -0.5)
    ka_o[...] = proj('ka')
    va_o[...] = proj('va').astype(BF16)

    ra = proj('ra')
    gl = _dot(ra.astype(BF16), wgu[...]) + bg_ref[...]
    lg_o[...] = (jnp.minimum(gl, 0.0) - jnp.log(1.0 + jnp.exp(-jnp.abs(gl)))) * (1.0 / GATE_TAU)

    ga = proj('ga')
    sga_o[...] = (ga * jax.nn.sigmoid(ga)).astype(BF16)

    low = lax.broadcasted_iota(jnp.int32, (x.shape[0], LANES), 1) < DH_B

    def head_norm(z, g_ref):
        zz = z * z
        scales = []
        for c in range(WB // LANES):
            t = zz[:, c * LANES:(c + 1) * LANES]
            ss_lo = jnp.sum(jnp.where(low, t, 0.0), axis=-1, keepdims=True)
            ss_hi = jnp.sum(jnp.where(low, 0.0, t), axis=-1, keepdims=True)
            scales.append(jnp.where(low, lax.rsqrt(ss_lo * (1.0 / DH_B) + EPS),
                                    lax.rsqrt(ss_hi * (1.0 / DH_B) + EPS)))
        return z * jnp.concatenate(scales, axis=1) * g_ref[...]

    qb_o[...] = (head_norm(proj('qb'), qg_ref) * (DH_B ** -0.5 * LOG2E)).astype(BF16)
    kb = head_norm(proj('kb'), kg_ref)
    kb_o[...] = kb.astype(BF16)
    vb = proj('vb')
    vb_o[...] = vb.astype(BF16)
    gb = proj('gb')
    sgb_o[...] = (gb * jax.nn.sigmoid(gb)).astype(BF16)
    sma_o[...] = jax.nn.sigmoid(proj('mga')).astype(BF16)
    smb_o[...] = jax.nn.sigmoid(proj('mgb')).astype(BF16)

    if tail_steps is None:
        kt_o[...] = kb
        vt_o[...] = vb
    else:
        per_seq, per_tail = tail_steps

        @pl.when(pl.program_id(0) % per_seq >= per_seq - per_tail)
        def _():
            kt_o[...] = kb.T
            vt_o[...] = vb.T


def _in_proj(x2d, w_al, layer, lw, tm, seq, keep):
    rows = x2d.shape[0]
    row = lambda w: pl.BlockSpec((tm, w), lambda i: (i, 0))
    widths = (WA_K, WA_K, WA_V, WA_K, WA_V, WB, WB, WB, WB, D_MODEL, D_MODEL)
    dtypes = (F32, F32, BF16, F32, BF16, BF16, BF16, BF16, BF16, BF16, BF16)
    if keep == seq:
        tail_steps = None
        tail = row(WB)
        tail_shape = jax.ShapeDtypeStruct((rows, WB), F32)
    else:
        assert keep % tm == 0 and seq % tm == 0
        per_seq, per_tail = seq // tm, keep // tm
        tail_steps = (per_seq, per_tail)
        tail = pl.BlockSpec((None, WB, tm), lambda i: (i // per_seq, 0,
                                                       jnp.maximum(i % per_seq - (per_seq - per_tail), 0)))
        tail_shape = jax.ShapeDtypeStruct((rows // seq, WB, keep), F32)
    wspec = pl.BlockSpec((None,) + w_al.shape[1:], lambda i: (layer, 0, 0), pipeline_mode=pl.Buffered(1))
    small = (lw['wgu'], lw['bg'], lw['qg'], lw['kg'])
    return pl.pallas_call(
        functools.partial(_in_proj_kernel, tail_steps=tail_steps),
        grid=(rows // tm,),
        in_specs=[row(D_MODEL), _resident(lw['ng'].shape), wspec] + [_resident(w.shape) for w in small],
        out_specs=[row(w) for w in widths] + [tail, tail],
        out_shape=[jax.ShapeDtypeStruct((rows, w), dt) for w, dt in zip(widths, dtypes)] + [tail_shape] * 2,
        compiler_params=_params(("arbitrary",)),
        name="in_proj",
    )(x2d, lw['ng'], w_al, *small)


def _gla_kernel(q_ref, k_ref, lg_ref, v_ref, s0_ref, o_ref, s_out_ref, st_ref, *, n_chunks):
    c = pl.program_id(1)

    @pl.when(c == 0)
    def _():
        st_ref[...] = s0_ref[...]

    tb = n_chunks * CHUNK
    n_sub = CHUNK // SUB
    shift = CHUNK.bit_length() - 1
    r = lax.broadcasted_iota(jnp.int32, (tb, tb), 0)
    cc = lax.broadcasted_iota(jnp.int32, (tb, tb), 1)
    same_chunk = lax.shift_right_logical(r, shift) == lax.shift_right_logical(cc, shift)
    ltri = jnp.where(same_chunk & (cc <= r), 1.0, 0.0).astype(BF16)
    lg = lg_ref[...]
    lg_hi = lg.astype(BF16)
    lg_lo = (lg - lg_hi.astype(F32)).astype(BF16)
    b_all = _dot(ltri, lg_hi) + _dot(ltri, lg_lo)

    ar = lax.broadcasted_iota(jnp.int32, (CHUNK, n_sub * CHUNK), 0)
    ac = lax.broadcasted_iota(jnp.int32, (CHUNK, n_sub * CHUNK), 1)
    sub_shift = SUB.bit_length() - 1
    a_keep = (lax.shift_right_logical(ar, sub_shift) == lax.shift_right_logical(ac, shift)) \
        & ((ac & (CHUNK - 1)) <= ar)
    krow = lax.broadcasted_iota(jnp.int32, (CHUNK, DK_A), 0)

    states = [st_ref[hh] for hh in range(HA)]
    for ci in range(n_chunks):
        rows = slice(ci * CHUNK, (ci + 1) * CHUNK)
        for hh in range(HA):
            ks = slice(hh * DK_A, (hh + 1) * DK_A)
            vs = slice(hh * DV_A, (hh + 1) * DV_A)
            b = b_all[rows, ks]
            q = q_ref[rows, ks]
            k = k_ref[rows, ks]
            v16 = v_ref[rows, vs]
            b_last = b[CHUNK - 1:CHUNK, :]
            o = _dot((q * jnp.exp(b)).astype(BF16), states[hh].astype(BF16))
            refs_b = [b[sb * SUB:sb * SUB + 1, :] for sb in range(n_sub)]
            ref_rows = jnp.concatenate([jnp.broadcast_to(rb, (SUB, DK_A)) for rb in refs_b], axis=0)
            qt = (q * jnp.exp(b - ref_rows)).astype(BF16)
            kt = jnp.concatenate(
                [(k * jnp.exp(jnp.where(krow < (sb + 1) * SUB, refs_b[sb] - b, 0.0))).astype(BF16)
                 for sb in range(n_sub)], axis=0)
            a = jnp.where(a_keep, _dot_nt(qt, kt), 0.0).astype(BF16)
            o_ref[rows, vs] = o + _dot(a, jnp.concatenate([v16] * n_sub, axis=0))
            k_dec_t = (k * jnp.exp(b_last - b)).T
            decay = jnp.exp(b.T[:, CHUNK - 1:CHUNK])
            states[hh] = states[hh] * decay + _dot(k_dec_t.astype(BF16), v16)
    for hh in range(HA):
        st_ref[hh] = states[hh]

    @pl.when(c == pl.num_programs(1) - 1)
    def _():
        s_out_ref[...] = st_ref[...]


def _gla(q, k, lg, v, s0, layer, tb):
    bsz, t, _ = q.shape
    kspec = pl.BlockSpec((None, tb, WA_K), lambda b, c: (b, c, 0))
    vspec = pl.BlockSpec((None, tb, WA_V), lambda b, c: (b, c, 0))
    sspec = pl.BlockSpec((None, HA, DK_A, DV_A), lambda b, c: (b, 0, 0, 0))
    s0spec = pl.BlockSpec((None, None, HA, DK_A, DV_A), lambda b, c: (layer, b, 0, 0, 0))
    return pl.pallas_call(
        functools.partial(_gla_kernel, n_chunks=tb // CHUNK),
        grid=(bsz, t // tb),
        in_specs=[kspec, kspec, kspec, vspec, s0spec],
        out_specs=[vspec, sspec],
        out_shape=[jax.ShapeDtypeStruct((bsz, t, WA_V), F32),
                   jax.ShapeDtypeStruct((bsz, HA, DK_A, DV_A), F32)],
        scratch_shapes=[pltpu.VMEM((HA, DK_A, DV_A), F32)],
        compiler_params=_params(("arbitrary", "arbitrary")),
        name="gla",
    )(q, k, lg, v, s0)


def _band_core(q_ref, k_refs, v_refs, bias_ref, o_ref):
    tq = q_ref.shape[0]
    widths = [kr.shape[0] for kr in k_refs]
    offs = [sum(widths[:p]) for p in range(len(widths) + 1)]
    n_pieces = len(widths)

    def half_masks(rows):
        lane = lax.broadcasted_iota(jnp.int32, (rows, LANES), 1)
        lo = jnp.where(lane < DH_B, 1.0, 0.0)
        return lo.astype(BF16), (1.0 - lo).astype(BF16)

    sel = {w: half_masks(w) for w in set(widths) | {tq}}

    for hp in range(HB // 2):
        ls = slice(hp * LANES, (hp + 1) * LANES)
        q2 = q_ref[:, ls]
        k2 = [kr[:, ls].astype(BF16) for kr in k_refs]
        v2 = [vr[:, ls].astype(BF16) for vr in v_refs]
        acc = None
        for e in range(2):
            hd = 2 * hp + e
            qm = q2 * sel[tq][e]
            s = [_dot_nt(qm, k2[p]) + bias_ref[hd, :, offs[p]:offs[p + 1]] for p in range(n_pieces)]
            if len(set(widths)) == 1:
                m = jnp.max(functools.reduce(jnp.maximum, s), axis=-1, keepdims=True)
            else:
                m = functools.reduce(jnp.maximum, [jnp.max(sp, axis=-1, keepdims=True) for sp in s])
            for p in range(n_pieces):
                ex = jnp.exp2(s[p] - m).astype(BF16)
                keep = sel[widths[p]][e]
                part = _dot(ex, jnp.concatenate([v2[p] * keep, keep], axis=1))
                acc = part if acc is None else acc + part
        o_ref[:, ls] = acc[:, :LANES] / acc[:, LANES:]


def _band_prompt_kernel(*refs, n_pieces):
    _band_core(refs[0], refs[1:1 + n_pieces], refs[1 + n_pieces:1 + 2 * n_pieces],
               refs[1 + 2 * n_pieces], refs[2 + 2 * n_pieces])


def _band_sample_kernel(q_ref, kc_ref, kn_ref, vc_ref, vn_ref, kt_ref, vt_ref, bias_ref, o_ref, ko_ref, vo_ref):
    s, lc = q_ref.shape[0], kc_ref.shape[1]
    lane = lax.broadcasted_iota(jnp.int32, (s, LANES), 1)
    lane_lo = jnp.where(lane < DH_B, 1.0, 0.0)
    lane_sel = (lane_lo.astype(BF16), (1.0 - lane_lo).astype(BF16))
    row = lax.broadcasted_iota(jnp.int32, (LANES, lc), 0)
    row_lo = jnp.where(row < DH_B, 1.0, 0.0)
    row_sel = (row_lo.astype(BF16), (1.0 - row_lo).astype(BF16))

    for hp in range(HB // 2):
        ls = slice(hp * LANES, (hp + 1) * LANES)
        q2 = q_ref[:, ls]
        kc2 = kc_ref[ls, :].astype(BF16)
        vc2 = vc_ref[ls, :].astype(BF16)
        kn2, vn2 = kn_ref[:, ls], vn_ref[:, ls]
        acc = None
        for e in range(2):
            hd = 2 * hp + e
            qm = q2 * lane_sel[e]
            s_c = _dot(qm, kc2) + bias_ref[hd, :, :lc]
            s_n = _dot_nt(qm, kn2) + bias_ref[hd, :, lc:]
            m = jnp.maximum(jnp.max(s_c, axis=-1, keepdims=True), jnp.max(s_n, axis=-1, keepdims=True))
            ex_c = jnp.exp2(s_c - m).astype(BF16)
            ex_n = jnp.exp2(s_n - m).astype(BF16)
            part = _dot_nt(ex_c, jnp.concatenate([vc2 * row_sel[e], row_sel[e]], axis=0)) \
                + _dot(ex_n, jnp.concatenate([vn2 * lane_sel[e], lane_sel[e]], axis=1))
            acc = part if acc is None else acc + part
        o_ref[:, ls] = acc[:, :LANES] / acc[:, LANES:]

    col = lax.broadcasted_iota(jnp.int32, (WB, lc), 1)
    pad_rows = jnp.zeros((LANES - s, WB), F32)
    pad_cols = jnp.zeros((WB, lc - LANES), F32)
    for c_ref, t_ref, out_ref in ((kc_ref, kt_ref, ko_ref), (vc_ref, vt_ref, vo_ref)):
        new_t = jnp.concatenate([t_ref[...], pad_rows], axis=0).T
        new_t = jnp.concatenate([pad_cols, pltpu.roll(new_t, LANES - s, 1)], axis=1)
        out_ref[...] = jnp.where(col >= lc - s, new_t, pltpu.roll(c_ref[...], lc - s, 1))


def _band_prompt(q, k, v, bias, layer, tq):
    bsz, t, _ = q.shape
    n_pieces = WINDOW // tq + 1
    qspec = pl.BlockSpec((None, tq, WB), lambda i, b: (b, i, 0))

    def kspec(p):
        back = n_pieces - 1 - p
        return pl.BlockSpec((None, tq, WB), lambda i, b: (b, jnp.maximum(i - back, 0), 0))

    kspecs = [kspec(p) for p in range(n_pieces)]
    bspec = pl.BlockSpec((None, None) + bias.shape[2:],
                         lambda i, b: (layer, jnp.minimum(i, n_pieces - 1), 0, 0, 0))
    return pl.pallas_call(
        functools.partial(_band_prompt_kernel, n_pieces=n_pieces),
        grid=(t // tq, bsz),
        in_specs=[qspec] + kspecs + kspecs + [bspec],
        out_specs=qspec,
        out_shape=jax.ShapeDtypeStruct((bsz, t, WB), F32),
        compiler_params=_params(("arbitrary", "arbitrary")),
        name="band_prompt",
    )(q, *([k] * n_pieces), *([v] * n_pieces), bias)


def _band_sample(q, kc_all, vc_all, layer, kn, vn, kt, vt, bias):
    bsz, s, _ = q.shape
    lc = kc_all.shape[3]
    assert s <= LANES <= lc and lc % LANES == 0
    nspec = pl.BlockSpec((None, s, WB), lambda b: (b, 0, 0))
    cspec = pl.BlockSpec((None, None, WB, lc), lambda b: (layer, b, 0, 0))
    ospec = pl.BlockSpec((None, WB, lc), lambda b: (b, 0, 0))
    buf = jax.ShapeDtypeStruct((bsz, WB, lc), F32)
    return pl.pallas_call(
        _band_sample_kernel,
        grid=(bsz,),
        in_specs=[nspec, cspec, nspec, cspec, nspec, nspec, nspec, _resident(bias.shape)],
        out_specs=[nspec, ospec, ospec],
        out_shape=[jax.ShapeDtypeStruct((bsz, s, WB), F32), buf, buf],
        compiler_params=_params(("arbitrary",)),
        name="band_sample",
    )(q, kc_all, kn, vc_all, vn, kt, vt, bias)


def _post_kernel(oa_ref, sga_ref, ob_ref, sgb_ref, sma_ref, smb_ref, x_ref, p_ref,
                 gg_ref, wa, wb, wo, pg_ref, wpg, wp, out_ref):
    oa = oa_ref[...]
    segs = []
    for hh in range(HA):
        seg = oa[:, hh * DV_A:(hh + 1) * DV_A]
        ms = jnp.mean(seg * seg, axis=-1, keepdims=True)
        segs.append(seg * lax.rsqrt(ms + EPS) * gg_ref[...])
    oan = jnp.concatenate(segs, axis=-1)
    ya = _dot((oan * sga_ref[...]).astype(BF16), wa[...])
    yb = _dot((ob_ref[...] * sgb_ref[...]).astype(BF16), wb[...])
    m = sma_ref[...] * ya + smb_ref[...] * yb
    x1 = x_ref[...] + _dot(m.astype(BF16), wo[...])
    ms = jnp.mean(x1 * x1, axis=-1, keepdims=True)
    hn = (x1 * lax.rsqrt(ms + EPS) * pg_ref[...]).astype(BF16)
    gate = jax.nn.sigmoid(_dot(hn, wpg[...]))
    out_ref[...] = x1 + gate * _dot(p_ref[...].astype(BF16), wp[...])


def _post(oa, sga, ob, sgb, sma, smb, x2d, p_all, layer, lw, tm):
    rows = x2d.shape[0]
    row = lambda w: pl.BlockSpec((tm, w), lambda i: (i, 0))
    pspec = pl.BlockSpec((None, tm, P_DIM), lambda i: (layer, i, 0))
    weights = (lw['gg'], lw['wa'], lw['wb'], lw['wo'], lw['pg'], lw['wpg'], lw['wp'])
    return pl.pallas_call(
        _post_kernel,
        grid=(rows // tm,),
        in_specs=[row(D_MODEL)] * 7 + [pspec] + [_resident(w.shape) for w in weights],
        out_specs=row(D_MODEL),
        out_shape=jax.ShapeDtypeStruct((rows, D_MODEL), F32),
        compiler_params=_params(("parallel",)),
        name="post",
    )(oa, sga, ob, sgb, sma, smb, x2d, p_all, *weights)


def _aligned_w_in(w_in):
    gate0 = 2 * WA_K + WA_V
    pad = jnp.zeros(w_in.shape[:2] + (GATE_PAD - GATE_RANK,), w_in.dtype)
    parts = [w_in[..., :gate0], w_in[..., gate0 + GATE_RANK:], w_in[..., gate0:gate0 + GATE_RANK], pad]
    return jnp.concatenate(parts, axis=-1).astype(BF16)


def _layer_weights(i, norm_g, w_gate_up, b_gate, gla_norm_g, q_norm_g, k_norm_g,
                   w_branch_a, w_branch_b, w_out, ple_norm_g, w_ple_gate, w_ple):
    lw = {}
    lw['wgu'] = jnp.pad(w_gate_up[i].astype(BF16), ((0, GATE_PAD - GATE_RANK), (0, 0)))
    lw['bg'] = b_gate[i].reshape(1, WA_K)
    lw['ng'] = norm_g[i].reshape(1, D_MODEL)
    lw['qg'] = jnp.tile(q_norm_g[i], HB).reshape(1, WB)
    lw['kg'] = jnp.tile(k_norm_g[i], HB).reshape(1, WB)
    lw['gg'] = gla_norm_g[i].reshape(1, DV_A)
    lw['wa'] = w_branch_a[i].astype(BF16)
    lw['wb'] = w_branch_b[i].astype(BF16)
    lw['wo'] = w_out[i].astype(BF16)
    lw['pg'] = ple_norm_g[i].reshape(1, D_MODEL)
    lw['wpg'] = w_ple_gate[i].astype(BF16)
    lw['wp'] = w_ple[i].astype(BF16)
    return lw


def _bias_kernel(vec_ref, o_ref, *, tq, nk):
    n_var = o_ref.shape[0]
    period = vec_ref.shape[-1]
    x = jnp.broadcast_to(vec_ref[...], (tq, period))
    toep = pltpu.roll(x, 0, 1, stride=1, stride_axis=0)[:, :nk]
    r = lax.broadcasted_iota(jnp.int32, (tq, nk), 0)
    c = lax.broadcasted_iota(jnp.int32, (tq, nk), 1)
    shift = CHUNK.bit_length() - 1
    dchunk = lax.shift_right_logical(r + WINDOW, shift) - lax.shift_right_logical(c, shift)
    visible = (dchunk >= 0) & (dchunk <= LEFT_CHUNKS)
    for j in range(n_var):
        o_ref[j] = jnp.where(visible & (c >= (n_var - 1 - j) * tq), toep, NEG)


def _band_bias(rel_bias, tq):
    depth = rel_bias.shape[0]
    nk = WINDOW + tq
    n_var = nk // tq
    period = nk + tq
    tab = rel_bias * LOG2E
    n_hi = WINDOW - MAX_REL
    far = jnp.broadcast_to(tab[..., -1:], (depth, HB, n_hi))
    near = jnp.broadcast_to(tab[..., :1], (depth, HB, max(nk - n_hi - (2 * MAX_REL + 1), 0)))
    wrap = jnp.broadcast_to(tab[..., -1:], (depth, HB, period - nk))
    body = jnp.concatenate([far, tab[..., ::-1], near], axis=-1)[..., :nk]
    vec = jnp.concatenate([body, wrap], axis=-1).reshape(depth, HB, 1, period)
    return pl.pallas_call(
        functools.partial(_bias_kernel, tq=tq, nk=nk),
        grid=(depth, HB),
        in_specs=[pl.BlockSpec((None, None, 1, period), lambda l, h: (l, h, 0, 0))],
        out_specs=pl.BlockSpec((None, n_var, None, tq, nk), lambda l, h: (l, 0, h, 0, 0)),
        out_shape=jax.ShapeDtypeStruct((depth, n_var, HB, tq, nk), F32),
        compiler_params=_params(("parallel", "parallel")),
        name="band_bias",
    )(vec)


def _layer(x, p_all, s0_all, kc, vc, bias, w_al, layer, lw, tm, tb, tq):
    bsz, t, _ = x.shape
    x2d = x.reshape(bsz * t, D_MODEL)
    keep = min(WINDOW, t)
    qa, ka, va, lg, sga, qb, kb, vb, sgb, sma, smb, ktail, vtail = _in_proj(x2d, w_al, layer, lw, tm, t, keep)
    r3 = lambda a: a.reshape(bsz, -1, a.shape[-1])
    if s0_all is None:
        s0_all, s_layer = jnp.zeros((1, bsz, HA, DK_A, DV_A), F32), 0
    else:
        s_layer = layer
    oa, st = _gla(r3(qa), r3(ka), r3(lg), r3(va), s0_all, s_layer, tb)
    if kc is None:
        ob = _band_prompt(r3(qb), r3(kb), r3(vb), bias, layer, tq)
        kbuf, vbuf = ktail, vtail
    else:
        lc = kc.shape[3]
        ob, kbuf, vbuf = _band_sample(r3(qb), kc, vc, layer, r3(kb), r3(vb), r3(ktail), r3(vtail),
                                      bias[layer, -1, :, :t, :lc + t])
    x_new = _post(oa.reshape(bsz * t, WA_V), sga, ob.reshape(bsz * t, WB), sgb, sma, smb, x2d,
                  p_all.reshape(p_all.shape[0], bsz * t, P_DIM), layer, lw, tm)
    return x_new.reshape(bsz, t, D_MODEL), st, kbuf, vbuf


def kernel(x_prompt, x_sample, state_gla, cache_band_k, cache_band_v, p_prompt, p_sample,
           norm_g, w_in, w_gate_up, b_gate, gla_norm_g, q_norm_g, k_norm_g, rel_bias,
           w_branch_a, w_branch_b, w_out, ple_norm_g, w_ple_gate, w_ple):
    depth = w_in.shape[0]
    tq = 256
    bias = _band_bias(rel_bias, tq)
    w_al = _aligned_w_in(w_in)
    xp, xs = x_prompt, x_sample

    def feature_major(a):
        return jnp.transpose(a, (0, 1, 3, 4, 2)).reshape(a.shape[:2] + (WB, a.shape[2]))

    def frame_major(a):
        return jnp.transpose(a.reshape(a.shape[:2] + (HB, DH_B, a.shape[3])), (0, 1, 4, 2, 3))

    kc, vc = feature_major(cache_band_k), feature_major(cache_band_v)
    outs = [[] for _ in range(6)]
    for i in range(depth):
        lw = _layer_weights(i, norm_g, w_gate_up, b_gate, gla_norm_g, q_norm_g, k_norm_g,
                            w_branch_a, w_branch_b, w_out, ple_norm_g, w_ple_gate, w_ple)
        xp, sp, kbp, vbp = _layer(xp, p_prompt, None, None, None, bias, w_al, i, lw, tm=512, tb=512, tq=tq)
        xs, ss, kbs, vbs = _layer(xs, p_sample, state_gla, kc, vc, bias, w_al, i, lw, tm=512, tb=CHUNK, tq=tq)
        for lst, a in zip(outs, (sp, kbp, vbp, ss, kbs, vbs)):
            lst.append(a)
    sp, kbp, vbp, ss, kbs, vbs = (jnp.stack(lst) for lst in outs)
    return (xp, xs, sp, frame_major(kbp), frame_major(vbp), ss, frame_major(kbs), frame_major(vbs))
```

```python
import functools

import numpy as np
import jax
import jax.numpy as jnp
from jax import lax
from jax.experimental import pallas as pl
from jax.experimental.pallas import tpu as pltpu

D_MODEL = 1024
CHUNK = 64
P_DIM = 256
EPS = 1e-6
NEG = -1e30
HA = 4
DK_A = 128
DV_A = 256
GATE_RANK = 16
GATE_TAU = 16.0
WA_K = HA * DK_A
WA_V = HA * DV_A
HB = 16
DH_B = 64
LEFT_CHUNKS = 8
WINDOW = LEFT_CHUNKS * CHUNK
MAX_REL = 128
WB = HB * DH_B

LANES = 128
GATE_PAD = LANES
SUB = 16
VMEM_LIMIT = 60 * 1024 * 1024
LOG2E = 1.4426950408889634

F32 = jnp.float32
BF16 = jnp.bfloat16


def _dot(a, b):
    return jnp.dot(a, b, preferred_element_type=F32)


def _dot_nt(a, b):
    return lax.dot_general(a, b, (((1,), (1,)), ((), ())), preferred_element_type=F32)


def _resident(shape):
    return pl.BlockSpec(shape, lambda *_: (0,) * len(shape), pipeline_mode=pl.Buffered(1))


def _params(sem):
    return pltpu.CompilerParams(dimension_semantics=sem, vmem_limit_bytes=VMEM_LIMIT)


_W_ROWS = {}
_off = 0
for _name, _width in (('qa', WA_K), ('ka', WA_K), ('va', WA_V), ('ra', GATE_RANK), ('ga', WA_V), ('qb', WB),
                      ('kb', WB), ('vb', WB), ('gb', WB), ('mga', D_MODEL), ('mgb', D_MODEL)):
    _W_ROWS[_name] = slice(_off, _off + (GATE_PAD if _name == 'ra' else _width))
    _off += _width
N_IN = _off


def _in_proj_kernel(x_ref, ng_ref, w_ref, wgu, bg_ref, qg_ref, kg_ref,
                    qa_o, ka_o, va_o, lg_o, sga_o, qb_o, kb_o, vb_o, sgb_o, sma_o, smb_o, kt_o, vt_o,
                    *, tail_steps):
    x = x_ref[...]
    ms = jnp.mean(x * x, axis=-1, keepdims=True)
    h = (x * lax.rsqrt(ms + EPS) * ng_ref[...]).astype(BF16)
    proj = lambda name: _dot_nt(h, w_ref[_W_ROWS[name], :])

    qa_o[...] = proj('qa') * (DK_A ** -0.5)
    ka_o[...] = proj('ka')
    va_o[...] = proj('va').astype(BF16)

    ra = proj('ra')
    gl = _dot(ra.astype(BF16), wgu[...]) + bg_ref[...]
    lg_o[...] = (jnp.minimum(gl, 0.0) - jnp.log(1.0 + jnp.exp(-jnp.abs(gl)))) * (1.0 / GATE_TAU)

    ga = proj('ga')
    sga_o[...] = (ga * jax.nn.sigmoid(ga)).astype(BF16)

    low = lax.broadcasted_iota(jnp.int32, (x.shape[0], LANES), 1) < DH_B

    def head_norm(z, g_ref):
        zz = z * z
        scales = []
        for c in range(WB // LANES):
            t = zz[:, c * LANES:(c + 1) * LANES]
            ss_lo = jnp.sum(jnp.where(low, t, 0.0), axis=-1, keepdims=True)
            ss_hi = jnp.sum(jnp.where(low, 0.0, t), axis=-1, keepdims=True)
            scales.append(jnp.where(low, lax.rsqrt(ss_lo * (1.0 / DH_B) + EPS),
                                    lax.rsqrt(ss_hi * (1.0 / DH_B) + EPS)))
        return z * jnp.concatenate(scales, axis=1) * g_ref[...]

    qb_o[...] = (head_norm(proj('qb'), qg_ref) * (DH_B ** -0.5 * LOG2E)).astype(BF16)
    kb = head_norm(proj('kb'), kg_ref)
    kb_o[...] = kb.astype(BF16)
    vb = proj('vb')
    vb_o[...] = vb.astype(BF16)
    gb = proj('gb')
    sgb_o[...] = (gb * jax.nn.sigmoid(gb)).astype(BF16)
    sma_o[...] = jax.nn.sigmoid(proj('mga')).astype(BF16)
    smb_o[...] = jax.nn.sigmoid(proj('mgb')).astype(BF16)

    if tail_steps is None:
        kt_o[...] = kb
        vt_o[...] = vb
    else:
        per_seq, per_tail = tail_steps

        @pl.when(pl.program_id(0) % per_seq >= per_seq - per_tail)
        def _():
            kt_o[...] = kb.T
            vt_o[...] = vb.T


def _in_proj(x2d, w_al, layer, lw, tm, seq, keep):
    rows = x2d.shape[0]
    row = lambda w: pl.BlockSpec((tm, w), lambda i: (i, 0))
    widths = (WA_K, WA_K, WA_V, WA_K, WA_V, WB, WB, WB, WB, D_MODEL, D_MODEL)
    dtypes = (F32, F32, BF16, F32, BF16, BF16, BF16, BF16, BF16, BF16, BF16)
    if keep == seq:
        tail_steps = None
        tail = row(WB)
        tail_shape = jax.ShapeDtypeStruct((rows, WB), F32)
    else:
        assert keep % tm == 0 and seq % tm == 0
        per_seq, per_tail = seq // tm, keep // tm
        tail_steps = (per_seq, per_tail)
        tail = pl.BlockSpec((None, WB, tm), lambda i: (i // per_seq, 0,
                                                       jnp.maximum(i % per_seq - (per_seq - per_tail), 0)))
        tail_shape = jax.ShapeDtypeStruct((rows // seq, WB, keep), F32)
    wspec = pl.BlockSpec((None,) + w_al.shape[1:], lambda i: (layer, 0, 0), pipeline_mode=pl.Buffered(1))
    small = (lw['wgu'], lw['bg'], lw['qg'], lw['kg'])
    return pl.pallas_call(
        functools.partial(_in_proj_kernel, tail_steps=tail_steps),
        grid=(rows // tm,),
        in_specs=[row(D_MODEL), _resident(lw['ng'].shape), wspec] + [_resident(w.shape) for w in small],
        out_specs=[row(w) for w in widths] + [tail, tail],
        out_shape=[jax.ShapeDtypeStruct((rows, w), dt) for w, dt in zip(widths, dtypes)] + [tail_shape] * 2,
        compiler_params=_params(("arbitrary",)),
        name="in_proj",
    )(x2d, lw['ng'], w_al, *small)


def _gla_steps(q_ref, k_ref, lg_ref, v_ref, o_ref, states):
    tb = q_ref.shape[0]
    n_chunks = tb // CHUNK
    n_sub = CHUNK // SUB
    shift = CHUNK.bit_length() - 1
    r = lax.broadcasted_iota(jnp.int32, (tb, tb), 0)
    cc = lax.broadcasted_iota(jnp.int32, (tb, tb), 1)
    same_chunk = lax.shift_right_logical(r, shift) == lax.shift_right_logical(cc, shift)
    ltri = jnp.where(same_chunk & (cc <= r), 1.0, 0.0).astype(BF16)
    lg = lg_ref[...]
    lg_hi = lg.astype(BF16)
    lg_lo = (lg - lg_hi.astype(F32)).astype(BF16)
    b_all = _dot(ltri, lg_hi) + _dot(ltri, lg_lo)

    ar = lax.broadcasted_iota(jnp.int32, (CHUNK, n_sub * CHUNK), 0)
    ac = lax.broadcasted_iota(jnp.int32, (CHUNK, n_sub * CHUNK), 1)
    sub_shift = SUB.bit_length() - 1
    a_keep = (lax.shift_right_logical(ar, sub_shift) == lax.shift_right_logical(ac, shift)) \
        & ((ac & (CHUNK - 1)) <= ar)
    krow = lax.broadcasted_iota(jnp.int32, (CHUNK, DK_A), 0)

    for ci in range(n_chunks):
        rows = slice(ci * CHUNK, (ci + 1) * CHUNK)
        for hh in range(HA):
            ks = slice(hh * DK_A, (hh + 1) * DK_A)
            vs = slice(hh * DV_A, (hh + 1) * DV_A)
            b = b_all[rows, ks]
            q = q_ref[rows, ks]
            k = k_ref[rows, ks]
            v16 = v_ref[rows, vs]
            b_last = b[CHUNK - 1:CHUNK, :]
            o = _dot((q * jnp.exp(b)).astype(BF16), states[hh].astype(BF16))
            refs_b = [b[sb * SUB:sb * SUB + 1, :] for sb in range(n_sub)]
            ref_rows = jnp.concatenate([jnp.broadcast_to(rb, (SUB, DK_A)) for rb in refs_b], axis=0)
            qt = (q * jnp.exp(b - ref_rows)).astype(BF16)
            kt = jnp.concatenate(
                [(k * jnp.exp(jnp.where(krow < (sb + 1) * SUB, refs_b[sb] - b, 0.0))).astype(BF16)
                 for sb in range(n_sub)], axis=0)
            a = jnp.where(a_keep, _dot_nt(qt, kt), 0.0).astype(BF16)
            o_ref[rows, vs] = o + _dot(a, jnp.concatenate([v16] * n_sub, axis=0))
            k_dec_t = (k * jnp.exp(b_last - b)).T
            decay = jnp.exp(b.T[:, CHUNK - 1:CHUNK])
            states[hh] = states[hh] * decay + _dot(k_dec_t.astype(BF16), v16)
            yield


def _gla_kernel(q_ref, k_ref, lg_ref, v_ref, s0_ref, o_ref, s_out_ref, st_ref):
    c = pl.program_id(1)

    @pl.when(c == 0)
    def _():
        st_ref[...] = s0_ref[...]

    states = [st_ref[hh] for hh in range(HA)]
    for _ in _gla_steps(q_ref, k_ref, lg_ref, v_ref, o_ref, states):
        pass
    for hh in range(HA):
        st_ref[hh] = states[hh]

    @pl.when(c == pl.num_programs(1) - 1)
    def _():
        s_out_ref[...] = st_ref[...]


def _gla(q, k, lg, v, s0, layer, tb):
    bsz, t, _ = q.shape
    kspec = pl.BlockSpec((None, tb, WA_K), lambda b, c: (b, c, 0))
    vspec = pl.BlockSpec((None, tb, WA_V), lambda b, c: (b, c, 0))
    sspec = pl.BlockSpec((None, HA, DK_A, DV_A), lambda b, c: (b, 0, 0, 0))
    s0spec = pl.BlockSpec((None, None, HA, DK_A, DV_A), lambda b, c: (layer, b, 0, 0, 0))
    return pl.pallas_call(
        _gla_kernel,
        grid=(bsz, t // tb),
        in_specs=[kspec, kspec, kspec, vspec, s0spec],
        out_specs=[vspec, sspec],
        out_shape=[jax.ShapeDtypeStruct((bsz, t, WA_V), F32),
                   jax.ShapeDtypeStruct((bsz, HA, DK_A, DV_A), F32)],
        scratch_shapes=[pltpu.VMEM((HA, DK_A, DV_A), F32)],
        compiler_params=_params(("arbitrary", "arbitrary")),
        name="gla",
    )(q, k, lg, v, s0)


def _band_steps(q_ref, k_refs, v_refs, bias_ref, o_ref):
    tq = q_ref.shape[0]
    widths = [kr.shape[0] for kr in k_refs]
    offs = [sum(widths[:p]) for p in range(len(widths) + 1)]
    n_pieces = len(widths)

    def half_masks(rows):
        lane = lax.broadcasted_iota(jnp.int32, (rows, LANES), 1)
        lo = jnp.where(lane < DH_B, 1.0, 0.0)
        return lo.astype(BF16), (1.0 - lo).astype(BF16)

    sel = {w: half_masks(w) for w in set(widths) | {tq}}

    for hp in range(HB // 2):
        ls = slice(hp * LANES, (hp + 1) * LANES)
        q2 = q_ref[:, ls]
        k2 = [kr[:, ls].astype(BF16) for kr in k_refs]
        v2 = [vr[:, ls].astype(BF16) for vr in v_refs]
        acc = None
        for e in range(2):
            hd = 2 * hp + e
            qm = q2 * sel[tq][e]
            s = [_dot_nt(qm, k2[p]) + bias_ref[hd, :, offs[p]:offs[p + 1]] for p in range(n_pieces)]
            if len(set(widths)) == 1:
                m = jnp.max(functools.reduce(jnp.maximum, s), axis=-1, keepdims=True)
            else:
                m = functools.reduce(jnp.maximum, [jnp.max(sp, axis=-1, keepdims=True) for sp in s])
            for p in range(n_pieces):
                ex = jnp.exp2(s[p] - m).astype(BF16)
                keep = sel[widths[p]][e]
                part = _dot(ex, jnp.concatenate([v2[p] * keep, keep], axis=1))
                acc = part if acc is None else acc + part
            if e == 1:
                o_ref[:, ls] = acc[:, :LANES] / acc[:, LANES:]
            yield


def _band_prompt_kernel(*refs, n_pieces):
    for _ in _band_steps(refs[0], refs[1:1 + n_pieces], refs[1 + n_pieces:1 + 2 * n_pieces],
                         refs[1 + 2 * n_pieces], refs[2 + 2 * n_pieces]):
        pass


def _band_sample_kernel(q_ref, kc_ref, kn_ref, vc_ref, vn_ref, kt_ref, vt_ref, bias_ref, *rest):
    o_ref, ko_ref, vo_ref = rest[-3:]
    if len(rest) == 5:
        ko_ref[:-1] = rest[0][...]
        vo_ref[:-1] = rest[1][...]
    s, lc = q_ref.shape[0], kc_ref.shape[1]
    lane = lax.broadcasted_iota(jnp.int32, (s, LANES), 1)
    lane_lo = jnp.where(lane < DH_B, 1.0, 0.0)
    lane_sel = (lane_lo.astype(BF16), (1.0 - lane_lo).astype(BF16))
    row = lax.broadcasted_iota(jnp.int32, (LANES, lc), 0)
    row_lo = jnp.where(row < DH_B, 1.0, 0.0)
    row_sel = (row_lo.astype(BF16), (1.0 - row_lo).astype(BF16))

    for hp in range(HB // 2):
        ls = slice(hp * LANES, (hp + 1) * LANES)
        q2 = q_ref[:, ls]
        kc2 = kc_ref[ls, :].astype(BF16)
        vc2 = vc_ref[ls, :].astype(BF16)
        kn2, vn2 = kn_ref[:, ls], vn_ref[:, ls]
        acc = None
        for e in range(2):
            hd = 2 * hp + e
            qm = q2 * lane_sel[e]
            s_c = _dot(qm, kc2) + bias_ref[hd, :, :lc]
            s_n = _dot_nt(qm, kn2) + bias_ref[hd, :, lc:]
            m = jnp.maximum(jnp.max(s_c, axis=-1, keepdims=True), jnp.max(s_n, axis=-1, keepdims=True))
            ex_c = jnp.exp2(s_c - m).astype(BF16)
            ex_n = jnp.exp2(s_n - m).astype(BF16)
            part = _dot_nt(ex_c, jnp.concatenate([vc2 * row_sel[e], row_sel[e]], axis=0)) \
                + _dot(ex_n, jnp.concatenate([vn2 * lane_sel[e], lane_sel[e]], axis=1))
            acc = part if acc is None else acc + part
        o_ref[:, ls] = acc[:, :LANES] / acc[:, LANES:]

    col = lax.broadcasted_iota(jnp.int32, (WB, lc), 1)
    pad_rows = jnp.zeros((LANES - s, WB), F32)
    pad_cols = jnp.zeros((WB, lc - LANES), F32)
    for c_ref, t_ref, out_ref in ((kc_ref, kt_ref, ko_ref), (vc_ref, vt_ref, vo_ref)):
        new_t = jnp.concatenate([t_ref[...], pad_rows], axis=0).T
        new_t = jnp.concatenate([pad_cols, pltpu.roll(new_t, LANES - s, 1)], axis=1)
        out_ref[out_ref.shape[0] - 1] = jnp.where(col >= lc - s, new_t, pltpu.roll(c_ref[...], lc - s, 1))


def _band_prompt(q, k, v, bias, layer, tq):
    bsz, t, _ = q.shape
    n_pieces = WINDOW // tq + 1
    qspec = pl.BlockSpec((None, tq, WB), lambda i, b: (b, i, 0))

    def kspec(p):
        back = n_pieces - 1 - p
        return pl.BlockSpec((None, tq, WB), lambda i, b: (b, jnp.maximum(i - back, 0), 0))

    kspecs = [kspec(p) for p in range(n_pieces)]
    bspec = pl.BlockSpec((None, None) + bias.shape[2:],
                         lambda i, b: (layer, jnp.minimum(i, n_pieces - 1), 0, 0, 0))
    return pl.pallas_call(
        functools.partial(_band_prompt_kernel, n_pieces=n_pieces),
        grid=(t // tq, bsz),
        in_specs=[qspec] + kspecs + kspecs + [bspec],
        out_specs=qspec,
        out_shape=jax.ShapeDtypeStruct((bsz, t, WB), F32),
        compiler_params=_params(("arbitrary", "arbitrary")),
        name="band_prompt",
    )(q, *([k] * n_pieces), *([v] * n_pieces), bias)


def _band_sample(q, kc_all, vc_all, layer, kn, vn, kt, vt, bias, prev):
    bsz, s, _ = q.shape
    lc = kc_all.shape[3]
    assert s <= LANES <= lc and lc % LANES == 0
    assert (prev is None) == (layer == 0)
    nspec = pl.BlockSpec((None, s, WB), lambda b: (b, 0, 0))
    cspec = pl.BlockSpec((None, None, WB, lc), lambda b: (layer, b, 0, 0))
    stacked = lambda n: pl.BlockSpec((n, None, WB, lc), lambda b: (0, b, 0, 0))
    buf = jax.ShapeDtypeStruct((layer + 1, bsz, WB, lc), F32)
    prev = () if prev is None else tuple(prev)
    return pl.pallas_call(
        _band_sample_kernel,
        grid=(bsz,),
        in_specs=[nspec, cspec, nspec, cspec, nspec, nspec, nspec, _resident(bias.shape)]
        + [stacked(layer)] * len(prev),
        out_specs=[nspec, stacked(layer + 1), stacked(layer + 1)],
        out_shape=[jax.ShapeDtypeStruct((bsz, s, WB), F32), buf, buf],
        compiler_params=_params(("arbitrary",)),
        name="band_sample",
    )(q, kc_all, kn, vc_all, vn, kt, vt, bias, *prev)


def _post_kernel(oa_ref, sga_ref, ob_ref, sgb_ref, sma_ref, smb_ref, x_ref, p_ref,
                 gg_ref, wa, wb, wo, pg_ref, wpg, wp, out_ref):
    tm = oa_ref.shape[0]
    halves = (slice(0, tm // 2), slice(tm // 2, tm))

    def gla_normed(rs):
        oa = oa_ref[rs, :]
        segs = []
        for hh in range(HA):
            seg = oa[:, hh * DV_A:(hh + 1) * DV_A]
            ms = jnp.mean(seg * seg, axis=-1, keepdims=True)
            segs.append(seg * lax.rsqrt(ms + EPS) * gg_ref[...])
        return jnp.concatenate(segs, axis=-1)

    ya = [_dot((gla_normed(rs) * sga_ref[rs, :]).astype(BF16), wa[...]) for rs in halves]
    yb = [_dot((ob_ref[rs, :] * sgb_ref[rs, :]).astype(BF16), wb[...]) for rs in halves]
    m = [(sma_ref[rs, :] * a + smb_ref[rs, :] * b).astype(BF16) for rs, a, b in zip(halves, ya, yb)]
    x1 = [x_ref[rs, :] + _dot(mm, wo[...]) for rs, mm in zip(halves, m)]
    hn = [(x * lax.rsqrt(jnp.mean(x * x, axis=-1, keepdims=True) + EPS) * pg_ref[...]).astype(BF16) for x in x1]
    gate = [jax.nn.sigmoid(_dot(h, wpg[...])) for h in hn]
    pe = [_dot(p_ref[rs, :].astype(BF16), wp[...]) for rs in halves]
    for rs, x, g, e in zip(halves, x1, gate, pe):
        out_ref[rs, :] = x + g * e


def _post(oa, sga, ob, sgb, sma, smb, x2d, p_all, layer, lw, tm):
    rows = x2d.shape[0]
    row = lambda w: pl.BlockSpec((tm, w), lambda i: (i, 0))
    pspec = pl.BlockSpec((None, tm, P_DIM), lambda i: (layer, i, 0))
    weights = (lw['gg'], lw['wa'], lw['wb'], lw['wo'], lw['pg'], lw['wpg'], lw['wp'])
    return pl.pallas_call(
        _post_kernel,
        grid=(rows // tm,),
        in_specs=[row(D_MODEL)] * 7 + [pspec] + [_resident(w.shape) for w in weights],
        out_specs=row(D_MODEL),
        out_shape=jax.ShapeDtypeStruct((rows, D_MODEL), F32),
        compiler_params=_params(("parallel",)),
        name="post",
    )(oa, sga, ob, sgb, sma, smb, x2d, p_all, *weights)


def _feature_major_w_in(w_in):
    assert w_in.shape[1:] == (D_MODEL, N_IN)
    return jnp.transpose(w_in, (0, 2, 1)).astype(BF16)


def _layer_weights(i, norm_g, w_gate_up, b_gate, gla_norm_g, q_norm_g, k_norm_g,
                   w_branch_a, w_branch_b, w_out, ple_norm_g, w_ple_gate, w_ple):
    lw = {}
    lw['wgu'] = jnp.pad(w_gate_up[i].astype(BF16), ((0, GATE_PAD - GATE_RANK), (0, 0)))
    lw['bg'] = b_gate[i].reshape(1, WA_K)
    lw['ng'] = norm_g[i].reshape(1, D_MODEL)
    lw['qg'] = jnp.tile(q_norm_g[i], HB).reshape(1, WB)
    lw['kg'] = jnp.tile(k_norm_g[i], HB).reshape(1, WB)
    lw['gg'] = gla_norm_g[i].reshape(1, DV_A)
    lw['wa'] = w_branch_a[i].astype(BF16)
    lw['wb'] = w_branch_b[i].astype(BF16)
    lw['wo'] = w_out[i].astype(BF16)
    lw['pg'] = ple_norm_g[i].reshape(1, D_MODEL)
    lw['wpg'] = w_ple_gate[i].astype(BF16)
    lw['wp'] = w_ple[i].astype(BF16)
    return lw


def _bias_kernel(vec_ref, o_ref, *, tq, nk):
    n_var = o_ref.shape[0]
    period = vec_ref.shape[-1]
    x = jnp.broadcast_to(vec_ref[...], (tq, period))
    toep = pltpu.roll(x, 0, 1, stride=1, stride_axis=0)[:, :nk]
    r = lax.broadcasted_iota(jnp.int32, (tq, nk), 0)
    c = lax.broadcasted_iota(jnp.int32, (tq, nk), 1)
    shift = CHUNK.bit_length() - 1
    dchunk = lax.shift_right_logical(r + WINDOW, shift) - lax.shift_right_logical(c, shift)
    visible = (dchunk >= 0) & (dchunk <= LEFT_CHUNKS)
    for j in range(n_var):
        o_ref[j] = jnp.where(visible & (c >= (n_var - 1 - j) * tq), toep, NEG)


def _band_bias(rel_bias, tq):
    depth = rel_bias.shape[0]
    nk = WINDOW + tq
    n_var = nk // tq
    period = nk + tq
    tab = rel_bias * LOG2E
    n_hi = WINDOW - MAX_REL
    far = jnp.broadcast_to(tab[..., -1:], (depth, HB, n_hi))
    near = jnp.broadcast_to(tab[..., :1], (depth, HB, max(nk - n_hi - (2 * MAX_REL + 1), 0)))
    wrap = jnp.broadcast_to(tab[..., -1:], (depth, HB, period - nk))
    body = jnp.concatenate([far, tab[..., ::-1], near], axis=-1)[..., :nk]
    vec = jnp.concatenate([body, wrap], axis=-1).reshape(depth, HB, 1, period)
    return pl.pallas_call(
        functools.partial(_bias_kernel, tq=tq, nk=nk),
        grid=(depth, HB),
        in_specs=[pl.BlockSpec((None, None, 1, period), lambda l, h: (l, h, 0, 0))],
        out_specs=pl.BlockSpec((None, n_var, None, tq, nk), lambda l, h: (l, 0, h, 0, 0)),
        out_shape=jax.ShapeDtypeStruct((depth, n_var, HB, tq, nk), F32),
        compiler_params=_params(("parallel", "parallel")),
        name="band_bias",
    )(vec)


def _layer(x, p_all, s0_all, kc, vc, prev_bufs, bias, w_al, layer, lw, tm, tb, tq):
    bsz, t, _ = x.shape
    x2d = x.reshape(bsz * t, D_MODEL)
    keep = min(WINDOW, t)
    qa, ka, va, lg, sga, qb, kb, vb, sgb, sma, smb, ktail, vtail = _in_proj(x2d, w_al, layer, lw, tm, t, keep)
    r3 = lambda a: a.reshape(bsz, -1, a.shape[-1])
    if s0_all is None:
        s0_all, s_layer = jnp.zeros((1, bsz, HA, DK_A, DV_A), F32), 0
    else:
        s_layer = layer
    oa, st = _gla(r3(qa), r3(ka), r3(lg), r3(va), s0_all, s_layer, tb)
    if kc is None:
        ob = _band_prompt(r3(qb), r3(kb), r3(vb), bias, layer, tq)
        kbuf, vbuf = ktail, vtail
    else:
        lc = kc.shape[3]
        ob, kbuf, vbuf = _band_sample(r3(qb), kc, vc, layer, r3(kb), r3(vb), r3(ktail), r3(vtail),
                                      bias[layer, -1, :, :t, :lc + t], prev_bufs)
    x_new = _post(oa.reshape(bsz * t, WA_V), sga, ob.reshape(bsz * t, WB), sgb, sma, smb, x2d,
                  p_all.reshape(p_all.shape[0], bsz * t, P_DIM), layer, lw, tm)
    return x_new.reshape(bsz, t, D_MODEL), st, kbuf, vbuf


def kernel(x_prompt, x_sample, state_gla, cache_band_k, cache_band_v, p_prompt, p_sample,
           norm_g, w_in, w_gate_up, b_gate, gla_norm_g, q_norm_g, k_norm_g, rel_bias,
           w_branch_a, w_branch_b, w_out, ple_norm_g, w_ple_gate, w_ple):
    depth = w_in.shape[0]
    tq = 256
    bias = _band_bias(rel_bias, tq)
    w_al = _feature_major_w_in(w_in)
    xp, xs = x_prompt, x_sample

    def feature_major(a):
        return jnp.transpose(a, (0, 1, 3, 4, 2)).reshape(a.shape[:2] + (WB, a.shape[2]))

    def frame_major(a):
        return jnp.transpose(a.reshape(a.shape[:2] + (HB, DH_B, a.shape[3])), (0, 1, 4, 2, 3))

    kc, vc = feature_major(cache_band_k), feature_major(cache_band_v)
    outs = [[] for _ in range(4)]
    sample_bufs = None
    for i in range(depth):
        lw = _layer_weights(i, norm_g, w_gate_up, b_gate, gla_norm_g, q_norm_g, k_norm_g,
                            w_branch_a, w_branch_b, w_out, ple_norm_g, w_ple_gate, w_ple)
        xp, sp, kbp, vbp = _layer(xp, p_prompt, None, None, None, None, bias, w_al, i, lw,
                                  tm=512, tb=512, tq=tq)
        xs, ss, kbs, vbs = _layer(xs, p_sample, state_gla, kc, vc, sample_bufs, bias, w_al, i, lw,
                                  tm=512, tb=CHUNK, tq=tq)
        sample_bufs = (kbs, vbs)
        for lst, a in zip(outs, (sp, kbp, vbp, ss)):
            lst.append(a)
    sp, kbp, vbp, ss = (jnp.stack(lst) for lst in outs)
    return (xp, xs, sp, frame_major(kbp), frame_major(vbp), ss, frame_major(kbs), frame_major(vbs))
```

```python
import functools

import numpy as np
import jax
import jax.numpy as jnp
from jax import lax
from jax.experimental import pallas as pl
from jax.experimental.pallas import tpu as pltpu

D_MODEL = 1024
CHUNK = 64
P_DIM = 256
EPS = 1e-6
NEG = -1e30
HA = 4
DK_A = 128
DV_A = 256
GATE_RANK = 16
GATE_TAU = 16.0
WA_K = HA * DK_A
WA_V = HA * DV_A
HB = 16
DH_B = 64
LEFT_CHUNKS = 8
WINDOW = LEFT_CHUNKS * CHUNK
MAX_REL = 128
WB = HB * DH_B

LANES = 128
GATE_PAD = LANES
SUB = 16
VMEM_LIMIT = 60 * 1024 * 1024
LOG2E = 1.4426950408889634

F32 = jnp.float32
BF16 = jnp.bfloat16


def _dot(a, b):
    return jnp.dot(a, b, preferred_element_type=F32)


def _dot_nt(a, b):
    return lax.dot_general(a, b, (((1,), (1,)), ((), ())), preferred_element_type=F32)


def _resident(shape):
    return pl.BlockSpec(shape, lambda *_: (0,) * len(shape), pipeline_mode=pl.Buffered(1))


def _params(sem):
    return pltpu.CompilerParams(dimension_semantics=sem, vmem_limit_bytes=VMEM_LIMIT)


_W_ROWS = {}
_off = 0
for _name, _width in (('qa', WA_K), ('ka', WA_K), ('va', WA_V), ('ra', GATE_RANK), ('ga', WA_V), ('qb', WB),
                      ('kb', WB), ('vb', WB), ('gb', WB), ('mga', D_MODEL), ('mgb', D_MODEL)):
    _W_ROWS[_name] = slice(_off, _off + (GATE_PAD if _name == 'ra' else _width))
    _off += _width
N_IN = _off


def _in_proj_kernel(x_ref, ng_ref, w_ref, wgu, bg_ref, qg_ref, kg_ref,
                    qa_o, ka_o, va_o, lg_o, sga_o, qb_o, kb_o, vb_o, sgb_o, sma_o, smb_o, kt_o, vt_o,
                    *, tail_steps):
    x = x_ref[...]
    ms = jnp.mean(x * x, axis=-1, keepdims=True)
    h = (x * lax.rsqrt(ms + EPS) * ng_ref[...]).astype(BF16)
    proj = lambda name: _dot_nt(h, w_ref[_W_ROWS[name], :])

    qa_o[...] = proj('qa') * (DK_A ** -0.5)
    ka_o[...] = proj('ka')
    va_o[...] = proj('va').astype(BF16)

    ra = proj('ra')
    gl = _dot(ra.astype(BF16), wgu[...]) + bg_ref[...]
    lg_o[...] = (jnp.minimum(gl, 0.0) - jnp.log(1.0 + jnp.exp(-jnp.abs(gl)))) * (1.0 / GATE_TAU)

    ga = proj('ga')
    sga_o[...] = (ga * jax.nn.sigmoid(ga)).astype(BF16)

    low = lax.broadcasted_iota(jnp.int32, (x.shape[0], LANES), 1) < DH_B

    def head_norm(z, g_ref):
        zz = z * z
        scales = []
        for c in range(WB // LANES):
            t = zz[:, c * LANES:(c + 1) * LANES]
            ss_lo = jnp.sum(jnp.where(low, t, 0.0), axis=-1, keepdims=True)
            ss_hi = jnp.sum(jnp.where(low, 0.0, t), axis=-1, keepdims=True)
            scales.append(jnp.where(low, lax.rsqrt(ss_lo * (1.0 / DH_B) + EPS),
                                    lax.rsqrt(ss_hi * (1.0 / DH_B) + EPS)))
        return z * jnp.concatenate(scales, axis=1) * g_ref[...]

    qb_o[...] = (head_norm(proj('qb'), qg_ref) * (DH_B ** -0.5 * LOG2E)).astype(BF16)
    kb = head_norm(proj('kb'), kg_ref)
    kb_o[...] = kb.astype(BF16)
    vb = proj('vb')
    vb_o[...] = vb.astype(BF16)
    gb = proj('gb')
    sgb_o[...] = (gb * jax.nn.sigmoid(gb)).astype(BF16)
    sma_o[...] = jax.nn.sigmoid(proj('mga')).astype(BF16)
    smb_o[...] = jax.nn.sigmoid(proj('mgb')).astype(BF16)

    if tail_steps is None:
        kt_o[...] = kb
        vt_o[...] = vb
    else:
        per_seq, per_tail = tail_steps

        @pl.when(pl.program_id(0) % per_seq >= per_seq - per_tail)
        def _():
            kt_o[...] = kb.T
            vt_o[...] = vb.T


def _in_proj(x2d, w_al, layer, lw, tm, seq, keep):
    rows = x2d.shape[0]
    row = lambda w: pl.BlockSpec((tm, w), lambda i: (i, 0))
    widths = (WA_K, WA_K, WA_V, WA_K, WA_V, WB, WB, WB, WB, D_MODEL, D_MODEL)
    dtypes = (F32, F32, BF16, F32, BF16, BF16, BF16, BF16, BF16, BF16, BF16)
    if keep == seq:
        tail_steps = None
        tail = row(WB)
        tail_shape = jax.ShapeDtypeStruct((rows, WB), F32)
    else:
        assert keep % tm == 0 and seq % tm == 0
        per_seq, per_tail = seq // tm, keep // tm
        tail_steps = (per_seq, per_tail)
        tail = pl.BlockSpec((None, WB, tm), lambda i: (i // per_seq, 0,
                                                       jnp.maximum(i % per_seq - (per_seq - per_tail), 0)))
        tail_shape = jax.ShapeDtypeStruct((rows // seq, WB, keep), F32)
    wspec = pl.BlockSpec((None,) + w_al.shape[1:], lambda i: (layer, 0, 0), pipeline_mode=pl.Buffered(1))
    small = (lw['wgu'], lw['bg'], lw['qg'], lw['kg'])
    return pl.pallas_call(
        functools.partial(_in_proj_kernel, tail_steps=tail_steps),
        grid=(rows // tm,),
        in_specs=[row(D_MODEL), _resident(lw['ng'].shape), wspec] + [_resident(w.shape) for w in small],
        out_specs=[row(w) for w in widths] + [tail, tail],
        out_shape=[jax.ShapeDtypeStruct((rows, w), dt) for w, dt in zip(widths, dtypes)] + [tail_shape] * 2,
        compiler_params=_params(("arbitrary",)),
        name="in_proj",
    )(x2d, lw['ng'], w_al, *small)


def _gla_steps(q_ref, k_ref, lg_ref, v_ref, o_ref, states):
    tb = q_ref.shape[0]
    n_chunks = tb // CHUNK
    n_sub = CHUNK // SUB
    shift = CHUNK.bit_length() - 1
    gw = min(tb, 2 * CHUNK)
    r = lax.broadcasted_iota(jnp.int32, (gw, gw), 0)
    cc = lax.broadcasted_iota(jnp.int32, (gw, gw), 1)
    same_chunk = lax.shift_right_logical(r, shift) == lax.shift_right_logical(cc, shift)
    ltri = jnp.where(same_chunk & (cc <= r), 1.0, 0.0).astype(BF16)
    ltri2 = jnp.concatenate([ltri, ltri], axis=1)
    b_groups = []
    for g0 in range(0, tb, gw):
        lg = lg_ref[g0:g0 + gw, :]
        lg_hi = lg.astype(BF16)
        lg_lo = (lg - lg_hi.astype(F32)).astype(BF16)
        b_groups.append(_dot(ltri2, jnp.concatenate([lg_hi, lg_lo], axis=0)))

    ar = lax.broadcasted_iota(jnp.int32, (CHUNK, n_sub * CHUNK), 0)
    ac = lax.broadcasted_iota(jnp.int32, (CHUNK, n_sub * CHUNK), 1)
    sub_shift = SUB.bit_length() - 1
    a_keep = (lax.shift_right_logical(ar, sub_shift) == lax.shift_right_logical(ac, shift)) \
        & ((ac & (CHUNK - 1)) <= ar)
    krow = lax.broadcasted_iota(jnp.int32, (CHUNK, DK_A), 0)

    for ci in range(n_chunks):
        rows = slice(ci * CHUNK, (ci + 1) * CHUNK)
        for hh in range(HA):
            ks = slice(hh * DK_A, (hh + 1) * DK_A)
            vs = slice(hh * DV_A, (hh + 1) * DV_A)
            g_row = ci * CHUNK % gw
            b = b_groups[ci * CHUNK // gw][g_row:g_row + CHUNK, ks]
            q = q_ref[rows, ks]
            k = k_ref[rows, ks]
            v16 = v_ref[rows, vs]
            b_last = b[CHUNK - 1:CHUNK, :]
            o = _dot((q * jnp.exp(b)).astype(BF16), states[hh].astype(BF16))
            refs_b = [b[sb * SUB:sb * SUB + 1, :] for sb in range(n_sub)]
            ref_rows = jnp.concatenate([jnp.broadcast_to(rb, (SUB, DK_A)) for rb in refs_b], axis=0)
            qt = (q * jnp.exp(b - ref_rows)).astype(BF16)
            kt = jnp.concatenate(
                [(k * jnp.exp(jnp.where(krow < (sb + 1) * SUB, refs_b[sb] - b, 0.0))).astype(BF16)
                 for sb in range(n_sub)], axis=0)
            a = jnp.where(a_keep, _dot_nt(qt, kt), 0.0).astype(BF16)
            o_ref[rows, vs] = o + _dot(a, jnp.concatenate([v16] * n_sub, axis=0))
            k_dec_t = (k * jnp.exp(b_last - b)).T
            decay = jnp.exp(b.T[:, CHUNK - 1:CHUNK])
            states[hh] = states[hh] * decay + _dot(k_dec_t.astype(BF16), v16)
            yield


def _gla_kernel(q_ref, k_ref, lg_ref, v_ref, s0_ref, *rest, n_extra):
    new_refs = rest[:n_extra]
    has_prev = len(rest) == 3 * n_extra + 4
    prev_refs = rest[n_extra:2 * n_extra + 1] if has_prev else ()
    o_ref = rest[-(n_extra + 3)]
    out_refs = rest[-(n_extra + 2):-1]
    st_ref = rest[-1]
    c = pl.program_id(1)

    @pl.when(c == 0)
    def _():
        st_ref[...] = s0_ref[...]

    states = [st_ref[hh] for hh in range(HA)]
    for _ in _gla_steps(q_ref, k_ref, lg_ref, v_ref, o_ref, states):
        pass
    for hh in range(HA):
        st_ref[hh] = states[hh]

    @pl.when(c == pl.num_programs(1) - 1)
    def _():
        last = out_refs[0].shape[0] - 1
        for new, out in zip((st_ref,) + tuple(new_refs), out_refs):
            out[last] = new[...]
        for prev, out in zip(prev_refs, out_refs):
            out[:last] = prev[...]


def _gla(q, k, lg, v, s0, s0_layer, tb, extras, prev):
    bsz, t, _ = q.shape
    n_prev = 0 if prev is None else prev[0].shape[0]
    kspec = pl.BlockSpec((None, tb, WA_K), lambda b, c: (b, c, 0))
    vspec = pl.BlockSpec((None, tb, WA_V), lambda b, c: (b, c, 0))
    s0spec = pl.BlockSpec((None, None, HA, DK_A, DV_A), lambda b, c: (s0_layer, b, 0, 0, 0))

    def per_seq(shape, lead):
        zeros = (0,) * len(shape)
        if lead is None:
            return pl.BlockSpec((None,) + shape, lambda b, c: (b,) + zeros)
        return pl.BlockSpec((lead, None) + shape, lambda b, c: (0, b) + zeros)

    stacked_shapes = [(HA, DK_A, DV_A)] + [e.shape[1:] for e in extras]
    prev = () if prev is None else tuple(prev)
    outs = pl.pallas_call(
        functools.partial(_gla_kernel, n_extra=len(extras)),
        grid=(bsz, t // tb),
        in_specs=[kspec, kspec, kspec, vspec, s0spec] + [per_seq(e.shape[1:], None) for e in extras]
        + [per_seq(s, n_prev) for s in stacked_shapes[:len(prev)]],
        out_specs=[vspec] + [per_seq(s, n_prev + 1) for s in stacked_shapes],
        out_shape=[jax.ShapeDtypeStruct((bsz, t, WA_V), F32)]
        + [jax.ShapeDtypeStruct((n_prev + 1, bsz) + s, F32) for s in stacked_shapes],
        scratch_shapes=[pltpu.VMEM((HA, DK_A, DV_A), F32)],
        compiler_params=_params(("arbitrary", "arbitrary")),
        name="gla",
    )(q, k, lg, v, s0, *extras, *prev)
    return outs[0], outs[1:]


def _band_steps(q_ref, k_refs, v_refs, bias_ref, o_ref):
    tq = q_ref.shape[0]
    widths = [kr.shape[0] for kr in k_refs]
    offs = [sum(widths[:p]) for p in range(len(widths) + 1)]
    n_pieces = len(widths)

    def half_masks(rows):
        lane = lax.broadcasted_iota(jnp.int32, (rows, LANES), 1)
        lo = jnp.where(lane < DH_B, 1.0, 0.0)
        return lo.astype(BF16), (1.0 - lo).astype(BF16)

    sel = {w: half_masks(w) for w in set(widths) | {tq}}

    for hp in range(HB // 2):
        ls = slice(hp * LANES, (hp + 1) * LANES)
        q2 = q_ref[:, ls]
        k2 = [kr[:, ls].astype(BF16) for kr in k_refs]
        v2 = [vr[:, ls].astype(BF16) for vr in v_refs]
        acc = None
        for e in range(2):
            hd = 2 * hp + e
            qm = q2 * sel[tq][e]
            s = [_dot_nt(qm, k2[p]) + bias_ref[hd, :, offs[p]:offs[p + 1]] for p in range(n_pieces)]
            if len(set(widths)) == 1:
                m = jnp.max(functools.reduce(jnp.maximum, s), axis=-1, keepdims=True)
            else:
                m = functools.reduce(jnp.maximum, [jnp.max(sp, axis=-1, keepdims=True) for sp in s])
            for p in range(n_pieces):
                ex = jnp.exp2(s[p] - m).astype(BF16)
                keep = sel[widths[p]][e]
                part = _dot(ex, jnp.concatenate([v2[p] * keep, keep], axis=1))
                acc = part if acc is None else acc + part
            if e == 1:
                o_ref[:, ls] = acc[:, :LANES] / acc[:, LANES:]
            yield


def _band_prompt_kernel(*refs, n_pieces):
    for _ in _band_steps(refs[0], refs[1:1 + n_pieces], refs[1 + n_pieces:1 + 2 * n_pieces],
                         refs[1 + 2 * n_pieces], refs[2 + 2 * n_pieces]):
        pass


def _band_sample_kernel(q_ref, kc_ref, kn_ref, vc_ref, vn_ref, kt_ref, vt_ref, bias_ref, *rest):
    o_ref, ko_ref, vo_ref = rest[-3:]
    if len(rest) == 5:
        ko_ref[:-1] = rest[0][...]
        vo_ref[:-1] = rest[1][...]
    s, lc = q_ref.shape[0], kc_ref.shape[1]
    lane = lax.broadcasted_iota(jnp.int32, (s, LANES), 1)
    lane_lo = jnp.where(lane < DH_B, 1.0, 0.0)
    lane_sel = (lane_lo.astype(BF16), (1.0 - lane_lo).astype(BF16))
    row = lax.broadcasted_iota(jnp.int32, (LANES, lc), 0)
    row_lo = jnp.where(row < DH_B, 1.0, 0.0)
    row_sel = (row_lo.astype(BF16), (1.0 - row_lo).astype(BF16))

    for hp in range(HB // 2):
        ls = slice(hp * LANES, (hp + 1) * LANES)
        q2 = q_ref[:, ls]
        kc2 = kc_ref[ls, :].astype(BF16)
        vc2 = vc_ref[ls, :].astype(BF16)
        kn2, vn2 = kn_ref[:, ls], vn_ref[:, ls]
        acc = None
        for e in range(2):
            hd = 2 * hp + e
            qm = q2 * lane_sel[e]
            s_c = _dot(qm, kc2) + bias_ref[hd, :, :lc]
            s_n = _dot_nt(qm, kn2) + bias_ref[hd, :, lc:]
            m = jnp.maximum(jnp.max(s_c, axis=-1, keepdims=True), jnp.max(s_n, axis=-1, keepdims=True))
            ex_c = jnp.exp2(s_c - m).astype(BF16)
            ex_n = jnp.exp2(s_n - m).astype(BF16)
            part = _dot_nt(ex_c, jnp.concatenate([vc2 * row_sel[e], row_sel[e]], axis=0)) \
                + _dot(ex_n, jnp.concatenate([vn2 * lane_sel[e], lane_sel[e]], axis=1))
            acc = part if acc is None else acc + part
        o_ref[:, ls] = acc[:, :LANES] / acc[:, LANES:]

    col = lax.broadcasted_iota(jnp.int32, (WB, lc), 1)
    pad_rows = jnp.zeros((LANES - s, WB), F32)
    pad_cols = jnp.zeros((WB, lc - LANES), F32)
    for c_ref, t_ref, out_ref in ((kc_ref, kt_ref, ko_ref), (vc_ref, vt_ref, vo_ref)):
        new_t = jnp.concatenate([t_ref[...], pad_rows], axis=0).T
        new_t = jnp.concatenate([pad_cols, pltpu.roll(new_t, LANES - s, 1)], axis=1)
        out_ref[out_ref.shape[0] - 1] = jnp.where(col >= lc - s, new_t, pltpu.roll(c_ref[...], lc - s, 1))


def _band_prompt(q, k, v, bias, layer, tq):
    bsz, t, _ = q.shape
    n_pieces = WINDOW // tq + 1
    qspec = pl.BlockSpec((None, tq, WB), lambda i, b: (b, i, 0))

    def kspec(p):
        back = n_pieces - 1 - p
        return pl.BlockSpec((None, tq, WB), lambda i, b: (b, jnp.maximum(i - back, 0), 0))

    kspecs = [kspec(p) for p in range(n_pieces)]
    bspec = pl.BlockSpec((None, None) + bias.shape[2:],
                         lambda i, b: (layer, jnp.minimum(i, n_pieces - 1), 0, 0, 0))
    return pl.pallas_call(
        functools.partial(_band_prompt_kernel, n_pieces=n_pieces),
        grid=(t // tq, bsz),
        in_specs=[qspec] + kspecs + kspecs + [bspec],
        out_specs=qspec,
        out_shape=jax.ShapeDtypeStruct((bsz, t, WB), F32),
        compiler_params=_params(("arbitrary", "arbitrary")),
        name="band_prompt",
    )(q, *([k] * n_pieces), *([v] * n_pieces), bias)


def _band_sample(q, kc_all, vc_all, layer, kn, vn, kt, vt, bias, prev):
    bsz, s, _ = q.shape
    lc = kc_all.shape[3]
    assert s <= LANES <= lc and lc % LANES == 0
    assert (prev is None) == (layer == 0)
    nspec = pl.BlockSpec((None, s, WB), lambda b: (b, 0, 0))
    cspec = pl.BlockSpec((None, None, WB, lc), lambda b: (layer, b, 0, 0))
    stacked = lambda n: pl.BlockSpec((n, None, WB, lc), lambda b: (0, b, 0, 0))
    buf = jax.ShapeDtypeStruct((layer + 1, bsz, WB, lc), F32)
    prev = () if prev is None else tuple(prev)
    return pl.pallas_call(
        _band_sample_kernel,
        grid=(bsz,),
        in_specs=[nspec, cspec, nspec, cspec, nspec, nspec, nspec, _resident(bias.shape)]
        + [stacked(layer)] * len(prev),
        out_specs=[nspec, stacked(layer + 1), stacked(layer + 1)],
        out_shape=[jax.ShapeDtypeStruct((bsz, s, WB), F32), buf, buf],
        compiler_params=_params(("arbitrary",)),
        name="band_sample",
    )(q, kc_all, kn, vc_all, vn, kt, vt, bias, *prev)


def _post_kernel(oa_ref, sga_ref, ob_ref, sgb_ref, sma_ref, smb_ref, x_ref, p_ref,
                 gg_ref, wa, wb, wo, pg_ref, wpg, wp, out_ref):
    tm = oa_ref.shape[0]
    halves = (slice(0, tm // 2), slice(tm // 2, tm))

    def gla_normed(rs):
        oa = oa_ref[rs, :]
        segs = []
        for hh in range(HA):
            seg = oa[:, hh * DV_A:(hh + 1) * DV_A]
            ms = jnp.mean(seg * seg, axis=-1, keepdims=True)
            segs.append(seg * lax.rsqrt(ms + EPS) * gg_ref[...])
        return jnp.concatenate(segs, axis=-1)

    ya = [_dot((gla_normed(rs) * sga_ref[rs, :]).astype(BF16), wa[...]) for rs in halves]
    yb = [_dot((ob_ref[rs, :] * sgb_ref[rs, :]).astype(BF16), wb[...]) for rs in halves]
    m = [(sma_ref[rs, :] * a + smb_ref[rs, :] * b).astype(BF16) for rs, a, b in zip(halves, ya, yb)]
    x1 = [x_ref[rs, :] + _dot(mm, wo[...]) for rs, mm in zip(halves, m)]
    hn = [(x * lax.rsqrt(jnp.mean(x * x, axis=-1, keepdims=True) + EPS) * pg_ref[...]).astype(BF16) for x in x1]
    gate = [jax.nn.sigmoid(_dot(h, wpg[...])) for h in hn]
    pe = [_dot(p_ref[rs, :].astype(BF16), wp[...]) for rs in halves]
    for rs, x, g, e in zip(halves, x1, gate, pe):
        out_ref[rs, :] = x + g * e


def _post(oa, sga, ob, sgb, sma, smb, x2d, p_all, layer, lw, tm):
    rows = x2d.shape[0]
    row = lambda w: pl.BlockSpec((tm, w), lambda i: (i, 0))
    pspec = pl.BlockSpec((None, tm, P_DIM), lambda i: (layer, i, 0))
    weights = (lw['gg'], lw['wa'], lw['wb'], lw['wo'], lw['pg'], lw['wpg'], lw['wp'])
    return pl.pallas_call(
        _post_kernel,
        grid=(rows // tm,),
        in_specs=[row(D_MODEL)] * 7 + [pspec] + [_resident(w.shape) for w in weights],
        out_specs=row(D_MODEL),
        out_shape=jax.ShapeDtypeStruct((rows, D_MODEL), F32),
        compiler_params=_params(("parallel",)),
        name="post",
    )(oa, sga, ob, sgb, sma, smb, x2d, p_all, *weights)


def _feature_major_w_in(w_in):
    assert w_in.shape[1:] == (D_MODEL, N_IN)
    return jnp.transpose(w_in, (0, 2, 1)).astype(BF16)


def _layer_weights(i, norm_g, w_gate_up, b_gate, gla_norm_g, q_norm_g, k_norm_g,
                   w_branch_a, w_branch_b, w_out, ple_norm_g, w_ple_gate, w_ple):
    lw = {}
    lw['wgu'] = jnp.pad(w_gate_up[i].astype(BF16), ((0, GATE_PAD - GATE_RANK), (0, 0)))
    lw['bg'] = b_gate[i].reshape(1, WA_K)
    lw['ng'] = norm_g[i].reshape(1, D_MODEL)
    lw['qg'] = jnp.tile(q_norm_g[i], HB).reshape(1, WB)
    lw['kg'] = jnp.tile(k_norm_g[i], HB).reshape(1, WB)
    lw['gg'] = gla_norm_g[i].reshape(1, DV_A)
    lw['wa'] = w_branch_a[i].astype(BF16)
    lw['wb'] = w_branch_b[i].astype(BF16)
    lw['wo'] = w_out[i].astype(BF16)
    lw['pg'] = ple_norm_g[i].reshape(1, D_MODEL)
    lw['wpg'] = w_ple_gate[i].astype(BF16)
    lw['wp'] = w_ple[i].astype(BF16)
    return lw


def _bias_kernel(vec_ref, o_ref, *, tq, nk):
    n_var = o_ref.shape[0]
    period = vec_ref.shape[-1]
    x = jnp.broadcast_to(vec_ref[...], (tq, period))
    toep = pltpu.roll(x, 0, 1, stride=1, stride_axis=0)[:, :nk]
    r = lax.broadcasted_iota(jnp.int32, (tq, nk), 0)
    c = lax.broadcasted_iota(jnp.int32, (tq, nk), 1)
    shift = CHUNK.bit_length() - 1
    dchunk = lax.shift_right_logical(r + WINDOW, shift) - lax.shift_right_logical(c, shift)
    visible = (dchunk >= 0) & (dchunk <= LEFT_CHUNKS)
    for j in range(n_var):
        o_ref[j] = jnp.where(visible & (c >= (n_var - 1 - j) * tq), toep, NEG)


def _band_bias(rel_bias, tq):
    depth = rel_bias.shape[0]
    nk = WINDOW + tq
    n_var = nk // tq
    period = nk + tq
    tab = rel_bias * LOG2E
    n_hi = WINDOW - MAX_REL
    far = jnp.broadcast_to(tab[..., -1:], (depth, HB, n_hi))
    near = jnp.broadcast_to(tab[..., :1], (depth, HB, max(nk - n_hi - (2 * MAX_REL + 1), 0)))
    wrap = jnp.broadcast_to(tab[..., -1:], (depth, HB, period - nk))
    body = jnp.concatenate([far, tab[..., ::-1], near], axis=-1)[..., :nk]
    vec = jnp.concatenate([body, wrap], axis=-1).reshape(depth, HB, 1, period)
    return pl.pallas_call(
        functools.partial(_bias_kernel, tq=tq, nk=nk),
        grid=(depth, HB),
        in_specs=[pl.BlockSpec((None, None, 1, period), lambda l, h: (l, h, 0, 0))],
        out_specs=pl.BlockSpec((None, n_var, None, tq, nk), lambda l, h: (l, 0, h, 0, 0)),
        out_shape=jax.ShapeDtypeStruct((depth, n_var, HB, tq, nk), F32),
        compiler_params=_params(("parallel", "parallel")),
        name="band_bias",
    )(vec)


def _layer(x, p_all, s0_all, kc, vc, prev, bias, w_al, layer, lw, tm, tb, tq):
    bsz, t, _ = x.shape
    x2d = x.reshape(bsz * t, D_MODEL)
    keep = min(WINDOW, t)
    qa, ka, va, lg, sga, qb, kb, vb, sgb, sma, smb, ktail, vtail = _in_proj(x2d, w_al, layer, lw, tm, t, keep)
    r3 = lambda a: a.reshape(bsz, -1, a.shape[-1])
    if s0_all is None:
        s0_all, s_layer = jnp.zeros((1, bsz, HA, DK_A, DV_A), F32), 0
    else:
        s_layer = layer
    if kc is None:
        oa, (st, kbuf, vbuf) = _gla(r3(qa), r3(ka), r3(lg), r3(va), s0_all, s_layer, tb, (ktail, vtail), prev)
        ob = _band_prompt(r3(qb), r3(kb), r3(vb), bias, layer, tq)
    else:
        oa, (st,) = _gla(r3(qa), r3(ka), r3(lg), r3(va), s0_all, s_layer, tb, (), None if prev is None else prev[:1])
        lc = kc.shape[3]
        ob, kbuf, vbuf = _band_sample(r3(qb), kc, vc, layer, r3(kb), r3(vb), r3(ktail), r3(vtail),
                                      bias[layer, -1, :, :t, :lc + t], None if prev is None else prev[1:])
    x_new = _post(oa.reshape(bsz * t, WA_V), sga, ob.reshape(bsz * t, WB), sgb, sma, smb, x2d,
                  p_all.reshape(p_all.shape[0], bsz * t, P_DIM), layer, lw, tm)
    return x_new.reshape(bsz, t, D_MODEL), (st, kbuf, vbuf)


def kernel(x_prompt, x_sample, state_gla, cache_band_k, cache_band_v, p_prompt, p_sample,
           norm_g, w_in, w_gate_up, b_gate, gla_norm_g, q_norm_g, k_norm_g, rel_bias,
           w_branch_a, w_branch_b, w_out, ple_norm_g, w_ple_gate, w_ple):
    depth = w_in.shape[0]
    tq = 256
    bias = _band_bias(rel_bias, tq)
    w_al = _feature_major_w_in(w_in)
    xp, xs = x_prompt, x_sample

    def feature_major(a):
        return jnp.transpose(a, (0, 1, 3, 4, 2)).reshape(a.shape[:2] + (WB, a.shape[2]))

    def frame_major(a):
        return jnp.transpose(a.reshape(a.shape[:2] + (HB, DH_B, a.shape[3])), (0, 1, 4, 2, 3))

    kc, vc = feature_major(cache_band_k), feature_major(cache_band_v)
    prompt_out = sample_out = None
    for i in range(depth):
        lw = _layer_weights(i, norm_g, w_gate_up, b_gate, gla_norm_g, q_norm_g, k_norm_g,
                            w_branch_a, w_branch_b, w_out, ple_norm_g, w_ple_gate, w_ple)
        xp, prompt_out = _layer(xp, p_prompt, None, None, None, prompt_out, bias, w_al, i, lw,
                                tm=512, tb=512, tq=tq)
        xs, sample_out = _layer(xs, p_sample, state_gla, kc, vc, sample_out, bias, w_al, i, lw,
                                tm=512, tb=CHUNK, tq=tq)
    (sp, kbp, vbp), (ss, kbs, vbs) = prompt_out, sample_out
    return (xp, xs, sp, frame_major(kbp), frame_major(vbp), ss, frame_major(kbs), frame_major(vbs))
```

```python
import functools

import numpy as np
import jax
import jax.numpy as jnp
from jax import lax
from jax.experimental import pallas as pl
from jax.experimental.pallas import tpu as pltpu

D_MODEL = 1024
CHUNK = 64
P_DIM = 256
EPS = 1e-6
NEG = -1e30
HA = 4
DK_A = 128
DV_A = 256
GATE_RANK = 16
GATE_TAU = 16.0
WA_K = HA * DK_A
WA_V = HA * DV_A
HB = 16
DH_B = 64
LEFT_CHUNKS = 8
WINDOW = LEFT_CHUNKS * CHUNK
MAX_REL = 128
WB = HB * DH_B

LANES = 128
GATE_PAD = LANES
SUB = 16
VMEM_LIMIT = 60 * 1024 * 1024
LOG2E = 1.4426950408889634

F32 = jnp.float32
BF16 = jnp.bfloat16


def _dot(a, b):
    return jnp.dot(a, b, preferred_element_type=F32)


def _dot_nt(a, b):
    return lax.dot_general(a, b, (((1,), (1,)), ((), ())), preferred_element_type=F32)


def _resident(shape):
    return pl.BlockSpec(shape, lambda *_: (0,) * len(shape), pipeline_mode=pl.Buffered(1))


def _params(sem):
    return pltpu.CompilerParams(dimension_semantics=sem, vmem_limit_bytes=VMEM_LIMIT)


_W_ROWS = {}
_off = 0
for _name, _width in (('qa', WA_K), ('ka', WA_K), ('va', WA_V), ('ra', GATE_RANK), ('ga', WA_V), ('qb', WB),
                      ('kb', WB), ('vb', WB), ('gb', WB), ('mga', D_MODEL), ('mgb', D_MODEL)):
    _W_ROWS[_name] = slice(_off, _off + (GATE_PAD if _name == 'ra' else _width))
    _off += _width
N_IN = _off


def _in_proj_kernel(x_ref, ng_ref, w_ref, wgu, bg_ref, qg_ref, kg_ref,
                    qa_o, ka_o, va_o, lg_o, sga_o, qb_o, kb_o, vb_o, sgb_o, sma_o, smb_o, kt_o, vt_o,
                    *, tail_steps):
    x = x_ref[...]
    ms = jnp.mean(x * x, axis=-1, keepdims=True)
    h = (x * lax.rsqrt(ms + EPS) * ng_ref[...]).astype(BF16)
    proj = lambda name: _dot_nt(h, w_ref[_W_ROWS[name], :])

    qa_o[...] = (proj('qa') * (DK_A ** -0.5)).astype(BF16)
    ka_o[...] = proj('ka').astype(BF16)
    va_o[...] = proj('va').astype(BF16)

    ra = proj('ra')
    gl = _dot(ra.astype(BF16), wgu[...]) + bg_ref[...]
    lg_o[...] = (jnp.minimum(gl, 0.0) - jnp.log(1.0 + jnp.exp(-jnp.abs(gl)))) * (1.0 / GATE_TAU)

    ga = proj('ga')
    sga_o[...] = (ga * jax.nn.sigmoid(ga)).astype(BF16)

    low = lax.broadcasted_iota(jnp.int32, (x.shape[0], LANES), 1) < DH_B

    def head_norm(z, g_ref):
        zz = z * z
        scales = []
        for c in range(WB // LANES):
            t = zz[:, c * LANES:(c + 1) * LANES]
            ss_lo = jnp.sum(jnp.where(low, t, 0.0), axis=-1, keepdims=True)
            ss_hi = jnp.sum(jnp.where(low, 0.0, t), axis=-1, keepdims=True)
            scales.append(jnp.where(low, lax.rsqrt(ss_lo * (1.0 / DH_B) + EPS),
                                    lax.rsqrt(ss_hi * (1.0 / DH_B) + EPS)))
        return z * jnp.concatenate(scales, axis=1) * g_ref[...]

    qb_o[...] = (head_norm(proj('qb'), qg_ref) * (DH_B ** -0.5 * LOG2E)).astype(BF16)
    kb = head_norm(proj('kb'), kg_ref)
    kb_o[...] = kb.astype(BF16)
    vb = proj('vb')
    vb_o[...] = vb.astype(BF16)
    gb = proj('gb')
    sgb_o[...] = (gb * jax.nn.sigmoid(gb)).astype(BF16)
    sma_o[...] = jax.nn.sigmoid(proj('mga')).astype(BF16)
    smb_o[...] = jax.nn.sigmoid(proj('mgb')).astype(BF16)

    if tail_steps is None:
        kt_o[...] = kb
        vt_o[...] = vb
    else:
        per_seq, per_tail = tail_steps

        @pl.when(pl.program_id(0) % per_seq >= per_seq - per_tail)
        def _():
            kt_o[...] = kb.T
            vt_o[...] = vb.T


def _in_proj(x2d, w_al, layer, lw, tm, seq, keep):
    rows = x2d.shape[0]
    row = lambda w: pl.BlockSpec((tm, w), lambda i: (i, 0))
    widths = (WA_K, WA_K, WA_V, WA_K, WA_V, WB, WB, WB, WB, D_MODEL, D_MODEL)
    dtypes = (BF16, BF16, BF16, F32, BF16, BF16, BF16, BF16, BF16, BF16, BF16)
    if keep == seq:
        tail_steps = None
        tail = row(WB)
        tail_shape = jax.ShapeDtypeStruct((rows, WB), F32)
    else:
        assert keep % tm == 0 and seq % tm == 0
        per_seq, per_tail = seq // tm, keep // tm
        tail_steps = (per_seq, per_tail)
        tail = pl.BlockSpec((None, WB, tm), lambda i: (i // per_seq, 0,
                                                       jnp.maximum(i % per_seq - (per_seq - per_tail), 0)))
        tail_shape = jax.ShapeDtypeStruct((rows // seq, WB, keep), F32)
    wspec = pl.BlockSpec((None,) + w_al.shape[1:], lambda i: (layer, 0, 0), pipeline_mode=pl.Buffered(1))
    small = (lw['wgu'], lw['bg'], lw['qg'], lw['kg'])
    return pl.pallas_call(
        functools.partial(_in_proj_kernel, tail_steps=tail_steps),
        grid=(rows // tm,),
        in_specs=[row(D_MODEL), _resident(lw['ng'].shape), wspec] + [_resident(w.shape) for w in small],
        out_specs=[row(w) for w in widths] + [tail, tail],
        out_shape=[jax.ShapeDtypeStruct((rows, w), dt) for w, dt in zip(widths, dtypes)] + [tail_shape] * 2,
        compiler_params=_params(("arbitrary",)),
        name="in_proj",
    )(x2d, lw['ng'], w_al, *small)


def _gla_steps(q_ref, k_ref, lg_ref, v_ref, o_ref, states):
    tb = q_ref.shape[0]
    n_chunks = tb // CHUNK
    n_sub = CHUNK // SUB
    shift = CHUNK.bit_length() - 1
    gw = min(tb, 2 * CHUNK)
    r = lax.broadcasted_iota(jnp.int32, (gw, gw), 0)
    cc = lax.broadcasted_iota(jnp.int32, (gw, gw), 1)
    same_chunk = lax.shift_right_logical(r, shift) == lax.shift_right_logical(cc, shift)
    ltri = jnp.where(same_chunk & (cc <= r), 1.0, 0.0).astype(BF16)
    ltri2 = jnp.concatenate([ltri, ltri], axis=1)
    b_groups = []
    for g0 in range(0, tb, gw):
        lg = lg_ref[g0:g0 + gw, :]
        lg_hi = lg.astype(BF16)
        lg_lo = (lg - lg_hi.astype(F32)).astype(BF16)
        b_groups.append(_dot(ltri2, jnp.concatenate([lg_hi, lg_lo], axis=0)))

    ar = lax.broadcasted_iota(jnp.int32, (CHUNK, n_sub * CHUNK), 0)
    ac = lax.broadcasted_iota(jnp.int32, (CHUNK, n_sub * CHUNK), 1)
    sub_shift = SUB.bit_length() - 1
    a_keep = (lax.shift_right_logical(ar, sub_shift) == lax.shift_right_logical(ac, shift)) \
        & ((ac & (CHUNK - 1)) <= ar)
    krow = lax.broadcasted_iota(jnp.int32, (CHUNK, DK_A), 0)

    for ci in range(n_chunks):
        rows = slice(ci * CHUNK, (ci + 1) * CHUNK)
        for hh in range(HA):
            ks = slice(hh * DK_A, (hh + 1) * DK_A)
            vs = slice(hh * DV_A, (hh + 1) * DV_A)
            g_row = ci * CHUNK % gw
            b = b_groups[ci * CHUNK // gw][g_row:g_row + CHUNK, ks]
            q = q_ref[rows, ks].astype(F32)
            k = k_ref[rows, ks].astype(F32)
            v16 = v_ref[rows, vs]
            b_last = b[CHUNK - 1:CHUNK, :]
            o = _dot((q * jnp.exp(b)).astype(BF16), states[hh].astype(BF16))
            refs_b = [b[sb * SUB:sb * SUB + 1, :] for sb in range(n_sub)]
            ref_rows = jnp.concatenate([jnp.broadcast_to(rb, (SUB, DK_A)) for rb in refs_b], axis=0)
            qt = (q * jnp.exp(b - ref_rows)).astype(BF16)
            kt = jnp.concatenate(
                [(k * jnp.exp(jnp.where(krow < (sb + 1) * SUB, refs_b[sb] - b, 0.0))).astype(BF16)
                 for sb in range(n_sub)], axis=0)
            a = jnp.where(a_keep, _dot_nt(qt, kt), 0.0).astype(BF16)
            o_ref[rows, vs] = (o + _dot(a, jnp.concatenate([v16] * n_sub, axis=0))).astype(o_ref.dtype)
            k_dec_t = (k * jnp.exp(b_last - b)).T
            decay = jnp.exp(b.T[:, CHUNK - 1:CHUNK])
            states[hh] = states[hh] * decay + _dot(k_dec_t.astype(BF16), v16)
            yield


def _gla_kernel(q_ref, k_ref, lg_ref, v_ref, s0_ref, *rest, n_extra):
    new_refs = rest[:n_extra]
    has_prev = len(rest) == 3 * n_extra + 4
    prev_refs = rest[n_extra:2 * n_extra + 1] if has_prev else ()
    o_ref = rest[-(n_extra + 3)]
    out_refs = rest[-(n_extra + 2):-1]
    st_ref = rest[-1]
    c = pl.program_id(1)

    @pl.when(c == 0)
    def _():
        st_ref[...] = s0_ref[...]

    states = [st_ref[hh] for hh in range(HA)]
    for _ in _gla_steps(q_ref, k_ref, lg_ref, v_ref, o_ref, states):
        pass
    for hh in range(HA):
        st_ref[hh] = states[hh]

    @pl.when(c == pl.num_programs(1) - 1)
    def _():
        last = out_refs[0].shape[0] - 1
        for new, out in zip((st_ref,) + tuple(new_refs), out_refs):
            out[last] = new[...]
        for prev, out in zip(prev_refs, out_refs):
            out[:last] = prev[...]


def _gla(q, k, lg, v, s0, s0_layer, tb, extras, prev):
    bsz, t, _ = q.shape
    n_prev = 0 if prev is None else prev[0].shape[0]
    kspec = pl.BlockSpec((None, tb, WA_K), lambda b, c: (b, c, 0))
    vspec = pl.BlockSpec((None, tb, WA_V), lambda b, c: (b, c, 0))
    s0spec = pl.BlockSpec((None, None, HA, DK_A, DV_A), lambda b, c: (s0_layer, b, 0, 0, 0))

    def per_seq(shape, lead):
        zeros = (0,) * len(shape)
        if lead is None:
            return pl.BlockSpec((None,) + shape, lambda b, c: (b,) + zeros)
        return pl.BlockSpec((lead, None) + shape, lambda b, c: (0, b) + zeros)

    stacked_shapes = [(HA, DK_A, DV_A)] + [e.shape[1:] for e in extras]
    prev = () if prev is None else tuple(prev)
    outs = pl.pallas_call(
        functools.partial(_gla_kernel, n_extra=len(extras)),
        grid=(bsz, t // tb),
        in_specs=[kspec, kspec, kspec, vspec, s0spec] + [per_seq(e.shape[1:], None) for e in extras]
        + [per_seq(s, n_prev) for s in stacked_shapes[:len(prev)]],
        out_specs=[vspec] + [per_seq(s, n_prev + 1) for s in stacked_shapes],
        out_shape=[jax.ShapeDtypeStruct((bsz, t, WA_V), BF16)]
        + [jax.ShapeDtypeStruct((n_prev + 1, bsz) + s, F32) for s in stacked_shapes],
        scratch_shapes=[pltpu.VMEM((HA, DK_A, DV_A), F32)],
        compiler_params=_params(("arbitrary", "arbitrary")),
        name="gla",
    )(q, k, lg, v, s0, *extras, *prev)
    return outs[0], outs[1:]


def _band_steps(q_ref, k_refs, v_refs, bias_ref, o_ref):
    tq = q_ref.shape[0]
    widths = [kr.shape[0] for kr in k_refs]
    offs = [sum(widths[:p]) for p in range(len(widths) + 1)]
    n_pieces = len(widths)

    def half_masks(rows):
        lane = lax.broadcasted_iota(jnp.int32, (rows, LANES), 1)
        lo = jnp.where(lane < DH_B, 1.0, 0.0)
        return lo.astype(BF16), (1.0 - lo).astype(BF16)

    sel = {w: half_masks(w) for w in set(widths) | {tq}}

    for hp in range(HB // 2):
        ls = slice(hp * LANES, (hp + 1) * LANES)
        q2 = q_ref[:, ls]
        k2 = [kr[:, ls].astype(BF16) for kr in k_refs]
        v2 = [vr[:, ls].astype(BF16) for vr in v_refs]
        acc = None
        for e in range(2):
            hd = 2 * hp + e
            qm = q2 * sel[tq][e]
            s = [_dot_nt(qm, k2[p]) + bias_ref[hd, :, offs[p]:offs[p + 1]] for p in range(n_pieces)]
            if len(set(widths)) == 1:
                m = jnp.max(functools.reduce(jnp.maximum, s), axis=-1, keepdims=True)
            else:
                m = functools.reduce(jnp.maximum, [jnp.max(sp, axis=-1, keepdims=True) for sp in s])
            for p in range(n_pieces):
                ex = jnp.exp2(s[p] - m).astype(BF16)
                keep = sel[widths[p]][e]
                part = _dot(ex, jnp.concatenate([v2[p] * keep, keep], axis=1))
                acc = part if acc is None else acc + part
            if e == 1:
                o_ref[:, ls] = (acc[:, :LANES] / acc[:, LANES:]).astype(o_ref.dtype)
            yield


def _band_prompt_kernel(*refs, n_pieces):
    for _ in _band_steps(refs[0], refs[1:1 + n_pieces], refs[1 + n_pieces:1 + 2 * n_pieces],
                         refs[1 + 2 * n_pieces], refs[2 + 2 * n_pieces]):
        pass


def _band_sample_kernel(q_ref, kc_ref, kn_ref, vc_ref, vn_ref, kt_ref, vt_ref, bias_ref, *rest):
    o_ref, ko_ref, vo_ref = rest[-3:]
    if len(rest) == 5:
        ko_ref[:-1] = rest[0][...]
        vo_ref[:-1] = rest[1][...]
    s, lc = q_ref.shape[0], kc_ref.shape[1]
    lane = lax.broadcasted_iota(jnp.int32, (s, LANES), 1)
    lane_lo = jnp.where(lane < DH_B, 1.0, 0.0)
    lane_sel = (lane_lo.astype(BF16), (1.0 - lane_lo).astype(BF16))
    row = lax.broadcasted_iota(jnp.int32, (LANES, lc), 0)
    row_lo = jnp.where(row < DH_B, 1.0, 0.0)
    row_sel = (row_lo.astype(BF16), (1.0 - row_lo).astype(BF16))

    for hp in range(HB // 2):
        ls = slice(hp * LANES, (hp + 1) * LANES)
        q2 = q_ref[:, ls]
        kc2 = kc_ref[ls, :].astype(BF16)
        vc2 = vc_ref[ls, :].astype(BF16)
        kn2, vn2 = kn_ref[:, ls], vn_ref[:, ls]
        acc = None
        for e in range(2):
            hd = 2 * hp + e
            qm = q2 * lane_sel[e]
            s_c = _dot(qm, kc2) + bias_ref[hd, :, :lc]
            s_n = _dot_nt(qm, kn2) + bias_ref[hd, :, lc:]
            m = jnp.maximum(jnp.max(s_c, axis=-1, keepdims=True), jnp.max(s_n, axis=-1, keepdims=True))
            ex_c = jnp.exp2(s_c - m).astype(BF16)
            ex_n = jnp.exp2(s_n - m).astype(BF16)
            part = _dot_nt(ex_c, jnp.concatenate([vc2 * row_sel[e], row_sel[e]], axis=0)) \
                + _dot(ex_n, jnp.concatenate([vn2 * lane_sel[e], lane_sel[e]], axis=1))
            acc = part if acc is None else acc + part
        o_ref[:, ls] = (acc[:, :LANES] / acc[:, LANES:]).astype(o_ref.dtype)

    col = lax.broadcasted_iota(jnp.int32, (WB, lc), 1)
    pad_rows = jnp.zeros((LANES - s, WB), F32)
    pad_cols = jnp.zeros((WB, lc - LANES), F32)
    for c_ref, t_ref, out_ref in ((kc_ref, kt_ref, ko_ref), (vc_ref, vt_ref, vo_ref)):
        new_t = jnp.concatenate([t_ref[...], pad_rows], axis=0).T
        new_t = jnp.concatenate([pad_cols, pltpu.roll(new_t, LANES - s, 1)], axis=1)
        out_ref[out_ref.shape[0] - 1] = jnp.where(col >= lc - s, new_t, pltpu.roll(c_ref[...], lc - s, 1))


def _band_prompt(q, k, v, bias, layer, tq):
    bsz, t, _ = q.shape
    n_pieces = WINDOW // tq + 1
    qspec = pl.BlockSpec((None, tq, WB), lambda i, b: (b, i, 0))

    def kspec(p):
        back = n_pieces - 1 - p
        return pl.BlockSpec((None, tq, WB), lambda i, b: (b, jnp.maximum(i - back, 0), 0))

    kspecs = [kspec(p) for p in range(n_pieces)]
    bspec = pl.BlockSpec((None, None) + bias.shape[2:],
                         lambda i, b: (layer, jnp.minimum(i, n_pieces - 1), 0, 0, 0))
    return pl.pallas_call(
        functools.partial(_band_prompt_kernel, n_pieces=n_pieces),
        grid=(t // tq, bsz),
        in_specs=[qspec] + kspecs + kspecs + [bspec],
        out_specs=qspec,
        out_shape=jax.ShapeDtypeStruct((bsz, t, WB), BF16),
        compiler_params=_params(("arbitrary", "arbitrary")),
        name="band_prompt",
    )(q, *([k] * n_pieces), *([v] * n_pieces), bias)


def _band_sample(q, kc_all, vc_all, layer, kn, vn, kt, vt, bias, prev):
    bsz, s, _ = q.shape
    lc = kc_all.shape[3]
    assert s <= LANES <= lc and lc % LANES == 0
    assert (prev is None) == (layer == 0)
    nspec = pl.BlockSpec((None, s, WB), lambda b: (b, 0, 0))
    cspec = pl.BlockSpec((None, None, WB, lc), lambda b: (layer, b, 0, 0))
    stacked = lambda n: pl.BlockSpec((n, None, WB, lc), lambda b: (0, b, 0, 0))
    buf = jax.ShapeDtypeStruct((layer + 1, bsz, WB, lc), F32)
    prev = () if prev is None else tuple(prev)
    return pl.pallas_call(
        _band_sample_kernel,
        grid=(bsz,),
        in_specs=[nspec, cspec, nspec, cspec, nspec, nspec, nspec, _resident(bias.shape)]
        + [stacked(layer)] * len(prev),
        out_specs=[nspec, stacked(layer + 1), stacked(layer + 1)],
        out_shape=[jax.ShapeDtypeStruct((bsz, s, WB), BF16), buf, buf],
        compiler_params=_params(("arbitrary",)),
        name="band_sample",
    )(q, kc_all, kn, vc_all, vn, kt, vt, bias, *prev)


def _post_kernel(oa_ref, sga_ref, ob_ref, sgb_ref, sma_ref, smb_ref, x_ref, p_ref,
                 gg_ref, wa, wb, wo, pg_ref, wpg, wp, out_ref):
    tm = oa_ref.shape[0]
    halves = (slice(0, tm // 2), slice(tm // 2, tm))

    def gla_normed(rs):
        oa = oa_ref[rs, :].astype(F32)
        segs = []
        for hh in range(HA):
            seg = oa[:, hh * DV_A:(hh + 1) * DV_A]
            ms = jnp.mean(seg * seg, axis=-1, keepdims=True)
            segs.append(seg * lax.rsqrt(ms + EPS) * gg_ref[...])
        return jnp.concatenate(segs, axis=-1)

    ya = [_dot((gla_normed(rs) * sga_ref[rs, :]).astype(BF16), wa[...]) for rs in halves]
    yb = [_dot((ob_ref[rs, :] * sgb_ref[rs, :]).astype(BF16), wb[...]) for rs in halves]
    m = [(sma_ref[rs, :] * a + smb_ref[rs, :] * b).astype(BF16) for rs, a, b in zip(halves, ya, yb)]
    x1 = [x_ref[rs, :] + _dot(mm, wo[...]) for rs, mm in zip(halves, m)]
    hn = [(x * lax.rsqrt(jnp.mean(x * x, axis=-1, keepdims=True) + EPS) * pg_ref[...]).astype(BF16) for x in x1]
    gate = [jax.nn.sigmoid(_dot(h, wpg[...])) for h in hn]
    pe = [_dot(p_ref[rs, :].astype(BF16), wp[...]) for rs in halves]
    for rs, x, g, e in zip(halves, x1, gate, pe):
        out_ref[rs, :] = x + g * e


def _post(oa, sga, ob, sgb, sma, smb, x2d, p_all, layer, lw, tm):
    rows = x2d.shape[0]
    row = lambda w: pl.BlockSpec((tm, w), lambda i: (i, 0))
    pspec = pl.BlockSpec((None, tm, P_DIM), lambda i: (layer, i, 0))
    weights = (lw['gg'], lw['wa'], lw['wb'], lw['wo'], lw['pg'], lw['wpg'], lw['wp'])
    return pl.pallas_call(
        _post_kernel,
        grid=(rows // tm,),
        in_specs=[row(D_MODEL)] * 7 + [pspec] + [_resident(w.shape) for w in weights],
        out_specs=row(D_MODEL),
        out_shape=jax.ShapeDtypeStruct((rows, D_MODEL), F32),
        compiler_params=_params(("parallel",)),
        name="post",
    )(oa, sga, ob, sgb, sma, smb, x2d, p_all, *weights)


def _feature_major_w_in(w_in):
    assert w_in.shape[1:] == (D_MODEL, N_IN)
    return jnp.transpose(w_in, (0, 2, 1)).astype(BF16)


def _layer_weights(i, norm_g, w_gate_up, b_gate, gla_norm_g, q_norm_g, k_norm_g,
                   w_branch_a, w_branch_b, w_out, ple_norm_g, w_ple_gate, w_ple):
    lw = {}
    lw['wgu'] = jnp.pad(w_gate_up[i].astype(BF16), ((0, GATE_PAD - GATE_RANK), (0, 0)))
    lw['bg'] = b_gate[i].reshape(1, WA_K)
    lw['ng'] = norm_g[i].reshape(1, D_MODEL)
    lw['qg'] = jnp.tile(q_norm_g[i], HB).reshape(1, WB)
    lw['kg'] = jnp.tile(k_norm_g[i], HB).reshape(1, WB)
    lw['gg'] = gla_norm_g[i].reshape(1, DV_A)
    lw['wa'] = w_branch_a[i].astype(BF16)
    lw['wb'] = w_branch_b[i].astype(BF16)
    lw['wo'] = w_out[i].astype(BF16)
    lw['pg'] = ple_norm_g[i].reshape(1, D_MODEL)
    lw['wpg'] = w_ple_gate[i].astype(BF16)
    lw['wp'] = w_ple[i].astype(BF16)
    return lw


def _bias_kernel(vec_ref, o_ref, *, tq, nk):
    n_var = o_ref.shape[0]
    period = vec_ref.shape[-1]
    x = jnp.broadcast_to(vec_ref[...], (tq, period))
    toep = pltpu.roll(x, 0, 1, stride=1, stride_axis=0)[:, :nk]
    r = lax.broadcasted_iota(jnp.int32, (tq, nk), 0)
    c = lax.broadcasted_iota(jnp.int32, (tq, nk), 1)
    shift = CHUNK.bit_length() - 1
    dchunk = lax.shift_right_logical(r + WINDOW, shift) - lax.shift_right_logical(c, shift)
    visible = (dchunk >= 0) & (dchunk <= LEFT_CHUNKS)
    for j in range(n_var):
        o_ref[j] = jnp.where(visible & (c >= (n_var - 1 - j) * tq), toep, NEG)


def _band_bias(rel_bias, tq):
    depth = rel_bias.shape[0]
    nk = WINDOW + tq
    n_var = nk // tq
    period = nk + tq
    tab = rel_bias * LOG2E
    n_hi = WINDOW - MAX_REL
    far = jnp.broadcast_to(tab[..., -1:], (depth, HB, n_hi))
    near = jnp.broadcast_to(tab[..., :1], (depth, HB, max(nk - n_hi - (2 * MAX_REL + 1), 0)))
    wrap = jnp.broadcast_to(tab[..., -1:], (depth, HB, period - nk))
    body = jnp.concatenate([far, tab[..., ::-1], near], axis=-1)[..., :nk]
    vec = jnp.concatenate([body, wrap], axis=-1).reshape(depth, HB, 1, period)
    return pl.pallas_call(
        functools.partial(_bias_kernel, tq=tq, nk=nk),
        grid=(depth, HB),
        in_specs=[pl.BlockSpec((None, None, 1, period), lambda l, h: (l, h, 0, 0))],
        out_specs=pl.BlockSpec((None, n_var, None, tq, nk), lambda l, h: (l, 0, h, 0, 0)),
        out_shape=jax.ShapeDtypeStruct((depth, n_var, HB, tq, nk), F32),
        compiler_params=_params(("parallel", "parallel")),
        name="band_bias",
    )(vec)


def _layer(x, p_all, s0_all, kc, vc, prev, bias, w_al, layer, lw, tm, tb, tq):
    bsz, t, _ = x.shape
    x2d = x.reshape(bsz * t, D_MODEL)
    keep = min(WINDOW, t)
    qa, ka, va, lg, sga, qb, kb, vb, sgb, sma, smb, ktail, vtail = _in_proj(x2d, w_al, layer, lw, tm, t, keep)
    r3 = lambda a: a.reshape(bsz, -1, a.shape[-1])
    if s0_all is None:
        s0_all, s_layer = jnp.zeros((1, bsz, HA, DK_A, DV_A), F32), 0
    else:
        s_layer = layer
    if kc is None:
        oa, (st, kbuf, vbuf) = _gla(r3(qa), r3(ka), r3(lg), r3(va), s0_all, s_layer, tb, (ktail, vtail), prev)
        ob = _band_prompt(r3(qb), r3(kb), r3(vb), bias, layer, tq)
    else:
        oa, (st,) = _gla(r3(qa), r3(ka), r3(lg), r3(va), s0_all, s_layer, tb, (), None if prev is None else prev[:1])
        lc = kc.shape[3]
        ob, kbuf, vbuf = _band_sample(r3(qb), kc, vc, layer, r3(kb), r3(vb), r3(ktail), r3(vtail),
                                      bias[layer, -1, :, :t, :lc + t], None if prev is None else prev[1:])
    x_new = _post(oa.reshape(bsz * t, WA_V), sga, ob.reshape(bsz * t, WB), sgb, sma, smb, x2d,
                  p_all.reshape(p_all.shape[0], bsz * t, P_DIM), layer, lw, tm)
    return x_new.reshape(bsz, t, D_MODEL), (st, kbuf, vbuf)


def kernel(x_prompt, x_sample, state_gla, cache_band_k, cache_band_v, p_prompt, p_sample,
           norm_g, w_in, w_gate_up, b_gate, gla_norm_g, q_norm_g, k_norm_g, rel_bias,
           w_branch_a, w_branch_b, w_out, ple_norm_g, w_ple_gate, w_ple):
    depth = w_in.shape[0]
    tq = 256
    bias = _band_bias(rel_bias, tq)
    w_al = _feature_major_w_in(w_in)
    xp, xs = x_prompt, x_sample

    def feature_major(a):
        return jnp.transpose(a, (0, 1, 3, 4, 2)).reshape(a.shape[:2] + (WB, a.shape[2]))

    def frame_major(a):
        return jnp.transpose(a.reshape(a.shape[:2] + (HB, DH_B, a.shape[3])), (0, 1, 4, 2, 3))

    kc, vc = feature_major(cache_band_k), feature_major(cache_band_v)
    prompt_out = sample_out = None
    for i in range(depth):
        lw = _layer_weights(i, norm_g, w_gate_up, b_gate, gla_norm_g, q_norm_g, k_norm_g,
                            w_branch_a, w_branch_b, w_out, ple_norm_g, w_ple_gate, w_ple)
        xp, prompt_out = _layer(xp, p_prompt, None, None, None, prompt_out, bias, w_al, i, lw,
                                tm=512, tb=512, tq=tq)
        xs, sample_out = _layer(xs, p_sample, state_gla, kc, vc, sample_out, bias, w_al, i, lw,
                                tm=512, tb=CHUNK, tq=tq)
    (sp, kbp, vbp), (ss, kbs, vbs) = prompt_out, sample_out
    return (xp, xs, sp, frame_major(kbp), frame_major(vbp), ss, frame_major(kbs), frame_major(vbs))
```

```python
import functools

import numpy as np
import jax
import jax.numpy as jnp
from jax import lax
from jax.experimental import pallas as pl
from jax.experimental.pallas import tpu as pltpu

D_MODEL = 1024
CHUNK = 64
P_DIM = 256
EPS = 1e-6
NEG = -1e30
HA = 4
DK_A = 128
DV_A = 256
GATE_RANK = 16
GATE_TAU = 16.0
WA_K = HA * DK_A
WA_V = HA * DV_A
HB = 16
DH_B = 64
LEFT_CHUNKS = 8
WINDOW = LEFT_CHUNKS * CHUNK
MAX_REL = 128
WB = HB * DH_B

LANES = 128
GATE_PAD = LANES
SUB = 16
VMEM_LIMIT = 60 * 1024 * 1024
LOG2E = 1.4426950408889634

F32 = jnp.float32
BF16 = jnp.bfloat16


def _dot(a, b):
    return jnp.dot(a, b, preferred_element_type=F32)


def _dot_nt(a, b):
    return lax.dot_general(a, b, (((1,), (1,)), ((), ())), preferred_element_type=F32)


def _resident(shape):
    return pl.BlockSpec(shape, lambda *_: (0,) * len(shape), pipeline_mode=pl.Buffered(1))


def _params(sem):
    return pltpu.CompilerParams(dimension_semantics=sem, vmem_limit_bytes=VMEM_LIMIT)


_W_ROWS = {}
_off = 0
for _name, _width in (('qa', WA_K), ('ka', WA_K), ('va', WA_V), ('ra', GATE_RANK), ('ga', WA_V), ('qb', WB),
                      ('kb', WB), ('vb', WB), ('gb', WB), ('mga', D_MODEL), ('mgb', D_MODEL)):
    _W_ROWS[_name] = slice(_off, _off + (GATE_PAD if _name == 'ra' else _width))
    _off += _width
N_IN = _off


def _in_proj_kernel(x_ref, ng_ref, w_ref, wgu, bg_ref, qg_ref, kg_ref,
                    qa_o, ka_o, va_o, lg_o, sga_o, qb_o, kb_o, vb_o, sgb_o, sma_o, smb_o, kt_o, vt_o,
                    *, tail_steps):
    x = x_ref[...]
    ms = jnp.mean(x * x, axis=-1, keepdims=True)
    h = (x * lax.rsqrt(ms + EPS) * ng_ref[...]).astype(BF16)
    proj = lambda name: _dot_nt(h, w_ref[_W_ROWS[name], :])

    qa_o[...] = (proj('qa') * (DK_A ** -0.5)).astype(BF16)
    ka_o[...] = proj('ka').astype(BF16)
    va_o[...] = proj('va').astype(BF16)

    ra = proj('ra')
    gl = _dot(ra.astype(BF16), wgu[...]) + bg_ref[...]
    lg_o[...] = (jnp.minimum(gl, 0.0) - jnp.log(1.0 + jnp.exp(-jnp.abs(gl)))) * (1.0 / GATE_TAU)

    ga = proj('ga')
    sga_o[...] = (ga * jax.nn.sigmoid(ga)).astype(BF16)

    low = lax.broadcasted_iota(jnp.int32, (x.shape[0], LANES), 1) < DH_B

    def head_norm(z, g_ref):
        zz = z * z
        scales = []
        for c in range(WB // LANES):
            t = zz[:, c * LANES:(c + 1) * LANES]
            ss_lo = jnp.sum(jnp.where(low, t, 0.0), axis=-1, keepdims=True)
            ss_hi = jnp.sum(jnp.where(low, 0.0, t), axis=-1, keepdims=True)
            scales.append(jnp.where(low, lax.rsqrt(ss_lo * (1.0 / DH_B) + EPS),
                                    lax.rsqrt(ss_hi * (1.0 / DH_B) + EPS)))
        return z * jnp.concatenate(scales, axis=1) * g_ref[...]

    qb_o[...] = (head_norm(proj('qb'), qg_ref) * (DH_B ** -0.5 * LOG2E)).astype(BF16)
    kb = head_norm(proj('kb'), kg_ref)
    kb_o[...] = kb.astype(BF16)
    vb = proj('vb')
    vb_o[...] = vb.astype(BF16)
    gb = proj('gb')
    sgb_o[...] = (gb * jax.nn.sigmoid(gb)).astype(BF16)
    sma_o[...] = jax.nn.sigmoid(proj('mga')).astype(BF16)
    smb_o[...] = jax.nn.sigmoid(proj('mgb')).astype(BF16)

    if tail_steps is None:
        kt_o[...] = kb
        vt_o[...] = vb
    else:
        per_seq, per_tail = tail_steps

        @pl.when(pl.program_id(0) % per_seq >= per_seq - per_tail)
        def _():
            kt_o[...] = kb.T
            vt_o[...] = vb.T


def _in_proj(x2d, w_al, layer, lw, tm, seq, keep):
    rows = x2d.shape[0]
    row = lambda w: pl.BlockSpec((tm, w), lambda i: (i, 0))
    widths = (WA_K, WA_K, WA_V, WA_K, WA_V, WB, WB, WB, WB, D_MODEL, D_MODEL)
    dtypes = (BF16, BF16, BF16, F32, BF16, BF16, BF16, BF16, BF16, BF16, BF16)
    if keep == seq:
        tail_steps = None
        tail = row(WB)
        tail_shape = jax.ShapeDtypeStruct((rows, WB), F32)
    else:
        assert keep % tm == 0 and seq % tm == 0
        per_seq, per_tail = seq // tm, keep // tm
        tail_steps = (per_seq, per_tail)
        tail = pl.BlockSpec((None, WB, tm), lambda i: (i // per_seq, 0,
                                                       jnp.maximum(i % per_seq - (per_seq - per_tail), 0)))
        tail_shape = jax.ShapeDtypeStruct((rows // seq, WB, keep), F32)
    wspec = pl.BlockSpec((None,) + w_al.shape[1:], lambda i: (layer, 0, 0), pipeline_mode=pl.Buffered(1))
    small = (lw['wgu'], lw['bg'], lw['qg'], lw['kg'])
    return pl.pallas_call(
        functools.partial(_in_proj_kernel, tail_steps=tail_steps),
        grid=(rows // tm,),
        in_specs=[row(D_MODEL), _resident(lw['ng'].shape), wspec] + [_resident(w.shape) for w in small],
        out_specs=[row(w) for w in widths] + [tail, tail],
        out_shape=[jax.ShapeDtypeStruct((rows, w), dt) for w, dt in zip(widths, dtypes)] + [tail_shape] * 2,
        compiler_params=_params(("arbitrary",)),
        name="in_proj",
    )(x2d, lw['ng'], w_al, *small)


def _gla_block(q_ref, k_ref, lg_ref, v_ref, o_ref, states):
    tb = q_ref.shape[0]
    n_chunks = tb // CHUNK
    n_sub = CHUNK // SUB
    shift = CHUNK.bit_length() - 1
    gw = min(tb, 2 * CHUNK)
    r = lax.broadcasted_iota(jnp.int32, (gw, gw), 0)
    cc = lax.broadcasted_iota(jnp.int32, (gw, gw), 1)
    same_chunk = lax.shift_right_logical(r, shift) == lax.shift_right_logical(cc, shift)
    ltri = jnp.where(same_chunk & (cc <= r), 1.0, 0.0).astype(BF16)
    ltri2 = jnp.concatenate([ltri, ltri], axis=1)
    b_groups = []
    for g0 in range(0, tb, gw):
        lg = lg_ref[g0:g0 + gw, :]
        lg_hi = lg.astype(BF16)
        lg_lo = (lg - lg_hi.astype(F32)).astype(BF16)
        b_groups.append(_dot(ltri2, jnp.concatenate([lg_hi, lg_lo], axis=0)))

    ar = lax.broadcasted_iota(jnp.int32, (CHUNK, n_sub * CHUNK), 0)
    ac = lax.broadcasted_iota(jnp.int32, (CHUNK, n_sub * CHUNK), 1)
    sub_shift = SUB.bit_length() - 1
    a_keep = (lax.shift_right_logical(ar, sub_shift) == lax.shift_right_logical(ac, shift)) \
        & ((ac & (CHUNK - 1)) <= ar)
    krow = lax.broadcasted_iota(jnp.int32, (CHUNK, DK_A), 0)

    for ci in range(n_chunks):
        rows = slice(ci * CHUNK, (ci + 1) * CHUNK)
        for hh in range(HA):
            ks = slice(hh * DK_A, (hh + 1) * DK_A)
            vs = slice(hh * DV_A, (hh + 1) * DV_A)
            g_row = ci * CHUNK % gw
            b = b_groups[ci * CHUNK // gw][g_row:g_row + CHUNK, ks]
            q = q_ref[rows, ks].astype(F32)
            k = k_ref[rows, ks].astype(F32)
            v16 = v_ref[rows, vs]
            b_last = b[CHUNK - 1:CHUNK, :]
            o = _dot((q * jnp.exp(b)).astype(BF16), states[hh].astype(BF16))
            refs_b = [b[sb * SUB:sb * SUB + 1, :] for sb in range(n_sub)]
            ref_rows = jnp.concatenate([jnp.broadcast_to(rb, (SUB, DK_A)) for rb in refs_b], axis=0)
            qt = (q * jnp.exp(b - ref_rows)).astype(BF16)
            kt = jnp.concatenate(
                [(k * jnp.exp(jnp.where(krow < (sb + 1) * SUB, refs_b[sb] - b, 0.0))).astype(BF16)
                 for sb in range(n_sub)], axis=0)
            a = jnp.where(a_keep, _dot_nt(qt, kt), 0.0).astype(BF16)
            o_ref[rows, vs] = (o + _dot(a, jnp.concatenate([v16] * n_sub, axis=0))).astype(o_ref.dtype)
            k_dec_t = (k * jnp.exp(b_last - b)).T
            decay = jnp.exp(b.T[:, CHUNK - 1:CHUNK])
            states[hh] = states[hh] * decay + _dot(k_dec_t.astype(BF16), v16)


def _gla_kernel(q_ref, k_ref, lg_ref, v_ref, s0_ref, *rest, n_extra):
    new_refs = rest[:n_extra]
    has_prev = len(rest) == 3 * n_extra + 4
    prev_refs = rest[n_extra:2 * n_extra + 1] if has_prev else ()
    o_ref = rest[-(n_extra + 3)]
    out_refs = rest[-(n_extra + 2):-1]
    st_ref = rest[-1]
    c = pl.program_id(1)

    @pl.when(c == 0)
    def _():
        st_ref[...] = s0_ref[...]

    states = [st_ref[hh] for hh in range(HA)]
    _gla_block(q_ref, k_ref, lg_ref, v_ref, o_ref, states)
    for hh in range(HA):
        st_ref[hh] = states[hh]

    @pl.when(c == pl.num_programs(1) - 1)
    def _():
        last = out_refs[0].shape[0] - 1
        for new, out in zip((st_ref,) + tuple(new_refs), out_refs):
            out[last] = new[...]
        for prev, out in zip(prev_refs, out_refs):
            out[:last] = prev[...]


def _gla(q, k, lg, v, s0, s0_layer, tb, extras, prev):
    bsz, t, _ = q.shape
    n_prev = 0 if prev is None else prev[0].shape[0]
    kspec = pl.BlockSpec((None, tb, WA_K), lambda b, c: (b, c, 0))
    vspec = pl.BlockSpec((None, tb, WA_V), lambda b, c: (b, c, 0))
    s0spec = pl.BlockSpec((None, None, HA, DK_A, DV_A), lambda b, c: (s0_layer, b, 0, 0, 0))

    def per_seq(shape, lead):
        zeros = (0,) * len(shape)
        if lead is None:
            return pl.BlockSpec((None,) + shape, lambda b, c: (b,) + zeros)
        return pl.BlockSpec((lead, None) + shape, lambda b, c: (0, b) + zeros)

    stacked_shapes = [(HA, DK_A, DV_A)] + [e.shape[1:] for e in extras]
    prev = () if prev is None else tuple(prev)
    outs = pl.pallas_call(
        functools.partial(_gla_kernel, n_extra=len(extras)),
        grid=(bsz, t // tb),
        in_specs=[kspec, kspec, kspec, vspec, s0spec] + [per_seq(e.shape[1:], None) for e in extras]
        + [per_seq(s, n_prev) for s in stacked_shapes[:len(prev)]],
        out_specs=[vspec] + [per_seq(s, n_prev + 1) for s in stacked_shapes],
        out_shape=[jax.ShapeDtypeStruct((bsz, t, WA_V), BF16)]
        + [jax.ShapeDtypeStruct((n_prev + 1, bsz) + s, F32) for s in stacked_shapes],
        scratch_shapes=[pltpu.VMEM((HA, DK_A, DV_A), F32)],
        compiler_params=_params(("arbitrary", "arbitrary")),
        name="gla",
    )(q, k, lg, v, s0, *extras, *prev)
    return outs[0], outs[1:]


def _band_prompt_kernel(*refs, n_pieces):
    q_ref = refs[0]
    k_refs = refs[1:1 + n_pieces]
    v_refs = refs[1 + n_pieces:1 + 2 * n_pieces]
    bias_ref, o_ref = refs[1 + 2 * n_pieces], refs[2 + 2 * n_pieces]
    w = k_refs[0].shape[0]
    lane = lax.broadcasted_iota(jnp.int32, (w, LANES), 1)
    lane_lo = jnp.where(lane < DH_B, 1.0, 0.0)
    sel = (lane_lo.astype(BF16), (1.0 - lane_lo).astype(BF16))

    for hp in range(HB // 2):
        ls = slice(hp * LANES, (hp + 1) * LANES)
        q2 = q_ref[:, ls]
        k2 = [kr[:, ls] for kr in k_refs]
        v2 = [vr[:, ls] for vr in v_refs]
        acc = None
        for e in range(2):
            hd = 2 * hp + e
            qm = q2 * sel[e]
            s = [_dot_nt(qm, k2[p]) + bias_ref[hd, :, p * w:(p + 1) * w] for p in range(n_pieces)]
            m = jnp.max(functools.reduce(jnp.maximum, s), axis=-1, keepdims=True)
            for p in range(n_pieces):
                ex = jnp.exp2(s[p] - m).astype(BF16)
                part = _dot(ex, jnp.concatenate([v2[p] * sel[e], sel[e]], axis=1))
                acc = part if acc is None else acc + part
        o_ref[:, ls] = (acc[:, :LANES] / acc[:, LANES:]).astype(o_ref.dtype)


def _band_sample_kernel(q_ref, kc_ref, kn_ref, vc_ref, vn_ref, kt_ref, vt_ref, bias_ref, *rest):
    o_ref, ko_ref, vo_ref = rest[-3:]
    if len(rest) == 5:
        ko_ref[:-1] = rest[0][...]
        vo_ref[:-1] = rest[1][...]
    s, lc = q_ref.shape[0], kc_ref.shape[1]
    lane = lax.broadcasted_iota(jnp.int32, (s, LANES), 1)
    lane_lo = jnp.where(lane < DH_B, 1.0, 0.0)
    lane_sel = (lane_lo.astype(BF16), (1.0 - lane_lo).astype(BF16))
    row = lax.broadcasted_iota(jnp.int32, (LANES, lc), 0)
    row_lo = jnp.where(row < DH_B, 1.0, 0.0)
    row_sel = (row_lo.astype(BF16), (1.0 - row_lo).astype(BF16))

    for hp in range(HB // 2):
        ls = slice(hp * LANES, (hp + 1) * LANES)
        q2 = q_ref[:, ls]
        kc2 = kc_ref[ls, :].astype(BF16)
        vc2 = vc_ref[ls, :].astype(BF16)
        kn2, vn2 = kn_ref[:, ls], vn_ref[:, ls]
        acc = None
        for e in range(2):
            hd = 2 * hp + e
            qm = q2 * lane_sel[e]
            s_c = _dot(qm, kc2) + bias_ref[hd, :, :lc]
            s_n = _dot_nt(qm, kn2) + bias_ref[hd, :, lc:]
            m = jnp.maximum(jnp.max(s_c, axis=-1, keepdims=True), jnp.max(s_n, axis=-1, keepdims=True))
            ex_c = jnp.exp2(s_c - m).astype(BF16)
            ex_n = jnp.exp2(s_n - m).astype(BF16)
            part = _dot_nt(ex_c, jnp.concatenate([vc2 * row_sel[e], row_sel[e]], axis=0)) \
                + _dot(ex_n, jnp.concatenate([vn2 * lane_sel[e], lane_sel[e]], axis=1))
            acc = part if acc is None else acc + part
        o_ref[:, ls] = (acc[:, :LANES] / acc[:, LANES:]).astype(o_ref.dtype)

    col = lax.broadcasted_iota(jnp.int32, (WB, lc), 1)
    pad_rows = jnp.zeros((LANES - s, WB), F32)
    pad_cols = jnp.zeros((WB, lc - LANES), F32)
    for c_ref, t_ref, out_ref in ((kc_ref, kt_ref, ko_ref), (vc_ref, vt_ref, vo_ref)):
        new_t = jnp.concatenate([t_ref[...], pad_rows], axis=0).T
        new_t = jnp.concatenate([pad_cols, pltpu.roll(new_t, LANES - s, 1)], axis=1)
        out_ref[out_ref.shape[0] - 1] = jnp.where(col >= lc - s, new_t, pltpu.roll(c_ref[...], lc - s, 1))


def _band_prompt(q, k, v, bias, layer, tq):
    bsz, t, _ = q.shape
    n_pieces = WINDOW // tq + 1
    qspec = pl.BlockSpec((None, tq, WB), lambda i, b: (b, i, 0))

    def kspec(p):
        back = n_pieces - 1 - p
        return pl.BlockSpec((None, tq, WB), lambda i, b: (b, jnp.maximum(i - back, 0), 0))

    kspecs = [kspec(p) for p in range(n_pieces)]
    bspec = pl.BlockSpec((None, None) + bias.shape[2:],
                         lambda i, b: (layer, jnp.minimum(i, n_pieces - 1), 0, 0, 0))
    return pl.pallas_call(
        functools.partial(_band_prompt_kernel, n_pieces=n_pieces),
        grid=(t // tq, bsz),
        in_specs=[qspec] + kspecs + kspecs + [bspec],
        out_specs=qspec,
        out_shape=jax.ShapeDtypeStruct((bsz, t, WB), BF16),
        compiler_params=_params(("arbitrary", "arbitrary")),
        name="band_prompt",
    )(q, *([k] * n_pieces), *([v] * n_pieces), bias)


def _band_sample(q, kc_all, vc_all, layer, kn, vn, kt, vt, bias, prev):
    bsz, s, _ = q.shape
    lc = kc_all.shape[3]
    assert s <= LANES <= lc and lc % LANES == 0
    assert (prev is None) == (layer == 0)
    nspec = pl.BlockSpec((None, s, WB), lambda b: (b, 0, 0))
    cspec = pl.BlockSpec((None, None, WB, lc), lambda b: (layer, b, 0, 0))
    stacked = lambda n: pl.BlockSpec((n, None, WB, lc), lambda b: (0, b, 0, 0))
    buf = jax.ShapeDtypeStruct((layer + 1, bsz, WB, lc), F32)
    prev = () if prev is None else tuple(prev)
    return pl.pallas_call(
        _band_sample_kernel,
        grid=(bsz,),
        in_specs=[nspec, cspec, nspec, cspec, nspec, nspec, nspec, _resident(bias.shape)]
        + [stacked(layer)] * len(prev),
        out_specs=[nspec, stacked(layer + 1), stacked(layer + 1)],
        out_shape=[jax.ShapeDtypeStruct((bsz, s, WB), BF16), buf, buf],
        compiler_params=_params(("arbitrary",)),
        name="band_sample",
    )(q, kc_all, kn, vc_all, vn, kt, vt, bias, *prev)


def _post_kernel(oa_ref, sga_ref, ob_ref, sgb_ref, sma_ref, smb_ref, x_ref, p_ref,
                 gg_ref, wa, wb, wo, pg_ref, wpg, wp, out_ref):
    tm = oa_ref.shape[0]
    halves = (slice(0, tm // 2), slice(tm // 2, tm))

    def gla_normed(rs):
        oa = oa_ref[rs, :].astype(F32)
        segs = []
        for hh in range(HA):
            seg = oa[:, hh * DV_A:(hh + 1) * DV_A]
            ms = jnp.mean(seg * seg, axis=-1, keepdims=True)
            segs.append(seg * lax.rsqrt(ms + EPS) * gg_ref[...])
        return jnp.concatenate(segs, axis=-1)

    ya = [_dot((gla_normed(rs) * sga_ref[rs, :]).astype(BF16), wa[...]) for rs in halves]
    yb = [_dot((ob_ref[rs, :] * sgb_ref[rs, :]).astype(BF16), wb[...]) for rs in halves]
    m = [(sma_ref[rs, :] * a + smb_ref[rs, :] * b).astype(BF16) for rs, a, b in zip(halves, ya, yb)]
    x1 = [x_ref[rs, :] + _dot(mm, wo[...]) for rs, mm in zip(halves, m)]
    hn = [(x * lax.rsqrt(jnp.mean(x * x, axis=-1, keepdims=True) + EPS) * pg_ref[...]).astype(BF16) for x in x1]
    gate = [jax.nn.sigmoid(_dot(h, wpg[...])) for h in hn]
    pe = [_dot(p_ref[rs, :].astype(BF16), wp[...]) for rs in halves]
    for rs, x, g, e in zip(halves, x1, gate, pe):
        out_ref[rs, :] = x + g * e


def _post(oa, sga, ob, sgb, sma, smb, x2d, p_all, layer, lw, tm):
    rows = x2d.shape[0]
    row = lambda w: pl.BlockSpec((tm, w), lambda i: (i, 0))
    pspec = pl.BlockSpec((None, tm, P_DIM), lambda i: (layer, i, 0))
    weights = (lw['gg'], lw['wa'], lw['wb'], lw['wo'], lw['pg'], lw['wpg'], lw['wp'])
    return pl.pallas_call(
        _post_kernel,
        grid=(rows // tm,),
        in_specs=[row(D_MODEL)] * 7 + [pspec] + [_resident(w.shape) for w in weights],
        out_specs=row(D_MODEL),
        out_shape=jax.ShapeDtypeStruct((rows, D_MODEL), F32),
        compiler_params=_params(("parallel",)),
        name="post",
    )(oa, sga, ob, sgb, sma, smb, x2d, p_all, *weights)


def _feature_major_w_in(w_in):
    assert w_in.shape[1:] == (D_MODEL, N_IN)
    return jnp.transpose(w_in, (0, 2, 1)).astype(BF16)


def _layer_weights(i, norm_g, w_gate_up, b_gate, gla_norm_g, q_norm_g, k_norm_g,
                   w_branch_a, w_branch_b, w_out, ple_norm_g, w_ple_gate, w_ple):
    lw = {}
    lw['wgu'] = jnp.pad(w_gate_up[i].astype(BF16), ((0, GATE_PAD - GATE_RANK), (0, 0)))
    lw['bg'] = b_gate[i].reshape(1, WA_K)
    lw['ng'] = norm_g[i].reshape(1, D_MODEL)
    lw['qg'] = jnp.tile(q_norm_g[i], HB).reshape(1, WB)
    lw['kg'] = jnp.tile(k_norm_g[i], HB).reshape(1, WB)
    lw['gg'] = gla_norm_g[i].reshape(1, DV_A)
    lw['wa'] = w_branch_a[i].astype(BF16)
    lw['wb'] = w_branch_b[i].astype(BF16)
    lw['wo'] = w_out[i].astype(BF16)
    lw['pg'] = ple_norm_g[i].reshape(1, D_MODEL)
    lw['wpg'] = w_ple_gate[i].astype(BF16)
    lw['wp'] = w_ple[i].astype(BF16)
    return lw


def _bias_kernel(vec_ref, o_ref, *, tq, nk):
    n_var = o_ref.shape[0]
    period = vec_ref.shape[-1]
    x = jnp.broadcast_to(vec_ref[...], (tq, period))
    toep = pltpu.roll(x, 0, 1, stride=1, stride_axis=0)[:, :nk]
    r = lax.broadcasted_iota(jnp.int32, (tq, nk), 0)
    c = lax.broadcasted_iota(jnp.int32, (tq, nk), 1)
    shift = CHUNK.bit_length() - 1
    dchunk = lax.shift_right_logical(r + WINDOW, shift) - lax.shift_right_logical(c, shift)
    visible = (dchunk >= 0) & (dchunk <= LEFT_CHUNKS)
    for j in range(n_var):
        o_ref[j] = jnp.where(visible & (c >= (n_var - 1 - j) * tq), toep, NEG)


def _band_bias(rel_bias, tq):
    depth = rel_bias.shape[0]
    nk = WINDOW + tq
    n_var = nk // tq
    period = nk + tq
    tab = rel_bias * LOG2E
    n_hi = WINDOW - MAX_REL
    far = jnp.broadcast_to(tab[..., -1:], (depth, HB, n_hi))
    near = jnp.broadcast_to(tab[..., :1], (depth, HB, max(nk - n_hi - (2 * MAX_REL + 1), 0)))
    wrap = jnp.broadcast_to(tab[..., -1:], (depth, HB, period - nk))
    body = jnp.concatenate([far, tab[..., ::-1], near], axis=-1)[..., :nk]
    vec = jnp.concatenate([body, wrap], axis=-1).reshape(depth, HB, 1, period)
    return pl.pallas_call(
        functools.partial(_bias_kernel, tq=tq, nk=nk),
        grid=(depth, HB),
        in_specs=[pl.BlockSpec((None, None, 1, period), lambda l, h: (l, h, 0, 0))],
        out_specs=pl.BlockSpec((None, n_var, None, tq, nk), lambda l, h: (l, 0, h, 0, 0)),
        out_shape=jax.ShapeDtypeStruct((depth, n_var, HB, tq, nk), F32),
        compiler_params=_params(("parallel", "parallel")),
        name="band_bias",
    )(vec)


BAND_TQ = 256


def _tiles(rows, seq):
    return dict(tm=min(rows, 512), tm_post=min(rows, 1024), tb=min(seq, 1024))


def _layer(x, p_all, s0_all, kc, vc, prev, bias, w_al, layer, lw):
    bsz, t, _ = x.shape
    x2d = x.reshape(bsz * t, D_MODEL)
    keep = min(WINDOW, t)
    tiles = _tiles(bsz * t, t)
    tm, tm_post, tb, tq = tiles['tm'], tiles['tm_post'], tiles['tb'], BAND_TQ
    qa, ka, va, lg, sga, qb, kb, vb, sgb, sma, smb, ktail, vtail = _in_proj(x2d, w_al, layer, lw, tm, t, keep)
    r3 = lambda a: a.reshape(bsz, -1, a.shape[-1])
    if s0_all is None:
        s0_all, s_layer = jnp.zeros((1, bsz, HA, DK_A, DV_A), F32), 0
    else:
        s_layer = layer
    if kc is None:
        oa, (st, kbuf, vbuf) = _gla(r3(qa), r3(ka), r3(lg), r3(va), s0_all, s_layer, tb, (ktail, vtail), prev)
        ob = _band_prompt(r3(qb), r3(kb), r3(vb), bias, layer, tq)
    else:
        oa, (st,) = _gla(r3(qa), r3(ka), r3(lg), r3(va), s0_all, s_layer, tb, (), None if prev is None else prev[:1])
        lc = kc.shape[3]
        ob, kbuf, vbuf = _band_sample(r3(qb), kc, vc, layer, r3(kb), r3(vb), r3(ktail), r3(vtail),
                                      bias[layer, -1, :, :t, :lc + t], None if prev is None else prev[1:])
    x_new = _post(oa.reshape(bsz * t, WA_V), sga, ob.reshape(bsz * t, WB), sgb, sma, smb, x2d,
                  p_all.reshape(p_all.shape[0], bsz * t, P_DIM), layer, lw, tm_post)
    return x_new.reshape(bsz, t, D_MODEL), (st, kbuf, vbuf)


def kernel(x_prompt, x_sample, state_gla, cache_band_k, cache_band_v, p_prompt, p_sample,
           norm_g, w_in, w_gate_up, b_gate, gla_norm_g, q_norm_g, k_norm_g, rel_bias,
           w_branch_a, w_branch_b, w_out, ple_norm_g, w_ple_gate, w_ple):
    depth = w_in.shape[0]
    bias = _band_bias(rel_bias, BAND_TQ)
    w_al = _feature_major_w_in(w_in)
    xp, xs = x_prompt, x_sample

    def feature_major(a):
        return jnp.transpose(a, (0, 1, 3, 4, 2)).reshape(a.shape[:2] + (WB, a.shape[2]))

    def frame_major(a):
        return jnp.transpose(a.reshape(a.shape[:2] + (HB, DH_B, a.shape[3])), (0, 1, 4, 2, 3))

    kc, vc = feature_major(cache_band_k), feature_major(cache_band_v)
    prompt_out = sample_out = None
    for i in range(depth):
        lw = _layer_weights(i, norm_g, w_gate_up, b_gate, gla_norm_g, q_norm_g, k_norm_g,
                            w_branch_a, w_branch_b, w_out, ple_norm_g, w_ple_gate, w_ple)
        xp, prompt_out = _layer(xp, p_prompt, None, None, None, prompt_out, bias, w_al, i, lw)
        xs, sample_out = _layer(xs, p_sample, state_gla, kc, vc, sample_out, bias, w_al, i, lw)
    (sp, kbp, vbp), (ss, kbs, vbs) = prompt_out, sample_out
    return (xp, xs, sp, frame_major(kbp), frame_major(vbp), ss, frame_major(kbs), frame_major(vbs))
```

```python
import functools

import numpy as np
import jax
import jax.numpy as jnp
from jax import lax
from jax.experimental import pallas as pl
from jax.experimental.pallas import tpu as pltpu

D_MODEL = 1024
CHUNK = 64
P_DIM = 256
EPS = 1e-6
NEG = -1e30
HA = 4
DK_A = 128
DV_A = 256
GATE_RANK = 16
GATE_TAU = 16.0
WA_K = HA * DK_A
WA_V = HA * DV_A
HB = 16
DH_B = 64
LEFT_CHUNKS = 8
WINDOW = LEFT_CHUNKS * CHUNK
MAX_REL = 128
WB = HB * DH_B

LANES = 128
GATE_PAD = LANES
SUB = 16
VMEM_LIMIT = 60 * 1024 * 1024
LOG2E = 1.4426950408889634

F32 = jnp.float32
BF16 = jnp.bfloat16


def _dot(a, b):
    return jnp.dot(a, b, preferred_element_type=F32)


def _dot_nt(a, b):
    return lax.dot_general(a, b, (((1,), (1,)), ((), ())), preferred_element_type=F32)


def _resident(shape):
    return pl.BlockSpec(shape, lambda *_: (0,) * len(shape), pipeline_mode=pl.Buffered(1))


def _params(sem):
    return pltpu.CompilerParams(dimension_semantics=sem, vmem_limit_bytes=VMEM_LIMIT)


_W_ROWS = {}
_off = 0
for _name, _width in (('qa', WA_K), ('ka', WA_K), ('va', WA_V), ('ra', GATE_RANK), ('ga', WA_V), ('qb', WB),
                      ('kb', WB), ('vb', WB), ('gb', WB), ('mga', D_MODEL), ('mgb', D_MODEL)):
    _W_ROWS[_name] = slice(_off, _off + (GATE_PAD if _name == 'ra' else _width))
    _off += _width
N_IN = _off


def _in_proj_kernel(x_ref, ng_ref, w_ref, wgu, bg_ref, qg_ref, kg_ref,
                    qa_o, ka_o, va_o, lg_o, sga_o, qb_o, kb_o, vb_o, sgb_o, sma_o, smb_o, kt_o, vt_o,
                    *, tail_steps):
    x = x_ref[...]
    ms = jnp.mean(x * x, axis=-1, keepdims=True)
    h = (x * lax.rsqrt(ms + EPS) * ng_ref[...]).astype(BF16)
    proj = lambda name: _dot_nt(h, w_ref[_W_ROWS[name], :])

    qa_o[...] = (proj('qa') * (DK_A ** -0.5)).astype(BF16)
    ka_o[...] = proj('ka').astype(BF16)
    va_o[...] = proj('va').astype(BF16)

    ra = proj('ra')
    gl = _dot(ra.astype(BF16), wgu[...]) + bg_ref[...]
    lg_o[...] = (jnp.minimum(gl, 0.0) - jnp.log(1.0 + jnp.exp(-jnp.abs(gl)))) * (1.0 / GATE_TAU)

    ga = proj('ga')
    sga_o[...] = (ga * jax.nn.sigmoid(ga)).astype(BF16)

    low = lax.broadcasted_iota(jnp.int32, (x.shape[0], LANES), 1) < DH_B

    def head_norm(z, g_ref):
        zz = z * z
        scales = []
        for c in range(WB // LANES):
            t = zz[:, c * LANES:(c + 1) * LANES]
            ss_lo = jnp.sum(jnp.where(low, t, 0.0), axis=-1, keepdims=True)
            ss_hi = jnp.sum(jnp.where(low, 0.0, t), axis=-1, keepdims=True)
            scales.append(jnp.where(low, lax.rsqrt(ss_lo * (1.0 / DH_B) + EPS),
                                    lax.rsqrt(ss_hi * (1.0 / DH_B) + EPS)))
        return z * jnp.concatenate(scales, axis=1) * g_ref[...]

    qb_o[...] = (head_norm(proj('qb'), qg_ref) * (DH_B ** -0.5 * LOG2E)).astype(BF16)
    kb = head_norm(proj('kb'), kg_ref)
    kb_o[...] = kb.astype(BF16)
    vb = proj('vb')
    vb_o[...] = vb.astype(BF16)
    gb = proj('gb')
    sgb_o[...] = (gb * jax.nn.sigmoid(gb)).astype(BF16)
    sma_o[...] = jax.nn.sigmoid(proj('mga')).astype(BF16)
    smb_o[...] = jax.nn.sigmoid(proj('mgb')).astype(BF16)

    if tail_steps is None:
        kt_o[...] = kb
        vt_o[...] = vb
    else:
        per_seq, per_tail = tail_steps

        @pl.when(pl.program_id(0) % per_seq >= per_seq - per_tail)
        def _():
            kt_o[...] = kb.T
            vt_o[...] = vb.T


def _in_proj(x2d, w_al, layer, lw, tm, seq, keep):
    rows = x2d.shape[0]
    row = lambda w: pl.BlockSpec((tm, w), lambda i: (i, 0))
    widths = (WA_K, WA_K, WA_V, WA_K, WA_V, WB, WB, WB, WB, D_MODEL, D_MODEL)
    dtypes = (BF16, BF16, BF16, F32, BF16, BF16, BF16, BF16, BF16, BF16, BF16)
    if keep == seq:
        tail_steps = None
        tail = row(WB)
        tail_shape = jax.ShapeDtypeStruct((rows, WB), F32)
    else:
        assert keep % tm == 0 and seq % tm == 0
        per_seq, per_tail = seq // tm, keep // tm
        tail_steps = (per_seq, per_tail)
        tail = pl.BlockSpec((None, WB, tm), lambda i: (i // per_seq, 0,
                                                       jnp.maximum(i % per_seq - (per_seq - per_tail), 0)))
        tail_shape = jax.ShapeDtypeStruct((rows // seq, WB, keep), F32)
    wspec = pl.BlockSpec((None,) + w_al.shape[1:], lambda i: (layer, 0, 0), pipeline_mode=pl.Buffered(1))
    small = (lw['wgu'], lw['bg'], lw['qg'], lw['kg'])
    return pl.pallas_call(
        functools.partial(_in_proj_kernel, tail_steps=tail_steps),
        grid=(rows // tm,),
        in_specs=[row(D_MODEL), _resident(lw['ng'].shape), wspec] + [_resident(w.shape) for w in small],
        out_specs=[row(w) for w in widths] + [tail, tail],
        out_shape=[jax.ShapeDtypeStruct((rows, w), dt) for w, dt in zip(widths, dtypes)] + [tail_shape] * 2,
        compiler_params=_params(("arbitrary",)),
        name="in_proj",
    )(x2d, lw['ng'], w_al, *small)


def _gla_block(q_ref, k_ref, lg_ref, v_ref, o_ref, states):
    tb = q_ref.shape[0]
    n_chunks = tb // CHUNK
    n_sub = CHUNK // SUB
    shift = CHUNK.bit_length() - 1
    gw = min(tb, 2 * CHUNK)
    r = lax.broadcasted_iota(jnp.int32, (gw, gw), 0)
    cc = lax.broadcasted_iota(jnp.int32, (gw, gw), 1)
    same_chunk = lax.shift_right_logical(r, shift) == lax.shift_right_logical(cc, shift)
    ltri = jnp.where(same_chunk & (cc <= r), 1.0, 0.0).astype(BF16)
    ltri2 = jnp.concatenate([ltri, ltri], axis=1)
    b_groups = []
    for g0 in range(0, tb, gw):
        lg = lg_ref[g0:g0 + gw, :]
        lg_hi = lg.astype(BF16)
        lg_lo = (lg - lg_hi.astype(F32)).astype(BF16)
        b_groups.append(_dot(ltri2, jnp.concatenate([lg_hi, lg_lo], axis=0)))

    ar = lax.broadcasted_iota(jnp.int32, (CHUNK, n_sub * CHUNK), 0)
    ac = lax.broadcasted_iota(jnp.int32, (CHUNK, n_sub * CHUNK), 1)
    sub_shift = SUB.bit_length() - 1
    a_keep = (lax.shift_right_logical(ar, sub_shift) == lax.shift_right_logical(ac, shift)) \
        & ((ac & (CHUNK - 1)) <= ar)
    krow = lax.broadcasted_iota(jnp.int32, (CHUNK, DK_A), 0)

    for ci in range(n_chunks):
        rows = slice(ci * CHUNK, (ci + 1) * CHUNK)
        for hh in range(HA):
            ks = slice(hh * DK_A, (hh + 1) * DK_A)
            vs = slice(hh * DV_A, (hh + 1) * DV_A)
            g_row = ci * CHUNK % gw
            b = b_groups[ci * CHUNK // gw][g_row:g_row + CHUNK, ks]
            q = q_ref[rows, ks].astype(F32)
            k = k_ref[rows, ks].astype(F32)
            v16 = v_ref[rows, vs]
            b_last = b[CHUNK - 1:CHUNK, :]
            o = _dot((q * jnp.exp(b)).astype(BF16), states[hh].astype(BF16))
            refs_b = [b[sb * SUB:sb * SUB + 1, :] for sb in range(n_sub)]
            ref_rows = jnp.concatenate([jnp.broadcast_to(rb, (SUB, DK_A)) for rb in refs_b], axis=0)
            qt = (q * jnp.exp(b - ref_rows)).astype(BF16)
            kt = jnp.concatenate(
                [(k * jnp.exp(jnp.where(krow < (sb + 1) * SUB, refs_b[sb] - b, 0.0))).astype(BF16)
                 for sb in range(n_sub)], axis=0)
            a = jnp.where(a_keep, _dot_nt(qt, kt), 0.0).astype(BF16)
            o_ref[rows, vs] = (o + _dot(a, jnp.concatenate([v16] * n_sub, axis=0))).astype(o_ref.dtype)
            k_dec_t = (k * jnp.exp(b_last - b)).T
            decay = jnp.exp(b.T[:, CHUNK - 1:CHUNK])
            states[hh] = states[hh] * decay + _dot(k_dec_t.astype(BF16), v16)


def _gla_kernel(q_ref, k_ref, lg_ref, v_ref, s0_ref, *rest, n_extra):
    new_refs = rest[:n_extra]
    has_prev = len(rest) == 3 * n_extra + 4
    prev_refs = rest[n_extra:2 * n_extra + 1] if has_prev else ()
    o_ref = rest[-(n_extra + 3)]
    out_refs = rest[-(n_extra + 2):-1]
    st_ref = rest[-1]
    c = pl.program_id(1)

    @pl.when(c == 0)
    def _():
        st_ref[...] = s0_ref[...]

    states = [st_ref[hh] for hh in range(HA)]
    _gla_block(q_ref, k_ref, lg_ref, v_ref, o_ref, states)
    for hh in range(HA):
        st_ref[hh] = states[hh]

    @pl.when(c == pl.num_programs(1) - 1)
    def _():
        last = out_refs[0].shape[0] - 1
        for new, out in zip((st_ref,) + tuple(new_refs), out_refs):
            out[last] = new[...]
        for prev, out in zip(prev_refs, out_refs):
            out[:last] = prev[...]


def _gla(q, k, lg, v, s0, s0_layer, tb, extras, prev):
    bsz, t, _ = q.shape
    n_prev = 0 if prev is None else prev[0].shape[0]
    kspec = pl.BlockSpec((None, tb, WA_K), lambda b, c: (b, c, 0))
    vspec = pl.BlockSpec((None, tb, WA_V), lambda b, c: (b, c, 0))
    s0spec = pl.BlockSpec((None, None, HA, DK_A, DV_A), lambda b, c: (s0_layer, b, 0, 0, 0))

    def per_seq(shape, lead):
        zeros = (0,) * len(shape)
        if lead is None:
            return pl.BlockSpec((None,) + shape, lambda b, c: (b,) + zeros)
        return pl.BlockSpec((lead, None) + shape, lambda b, c: (0, b) + zeros)

    stacked_shapes = [(HA, DK_A, DV_A)] + [e.shape[1:] for e in extras]
    prev = () if prev is None else tuple(prev)
    outs = pl.pallas_call(
        functools.partial(_gla_kernel, n_extra=len(extras)),
        grid=(bsz, t // tb),
        in_specs=[kspec, kspec, kspec, vspec, s0spec] + [per_seq(e.shape[1:], None) for e in extras]
        + [per_seq(s, n_prev) for s in stacked_shapes[:len(prev)]],
        out_specs=[vspec] + [per_seq(s, n_prev + 1) for s in stacked_shapes],
        out_shape=[jax.ShapeDtypeStruct((bsz, t, WA_V), BF16)]
        + [jax.ShapeDtypeStruct((n_prev + 1, bsz) + s, F32) for s in stacked_shapes],
        scratch_shapes=[pltpu.VMEM((HA, DK_A, DV_A), F32)],
        compiler_params=_params(("arbitrary", "arbitrary")),
        name="gla",
    )(q, k, lg, v, s0, *extras, *prev)
    return outs[0], outs[1:]


def _band_prompt_kernel(*refs, n_pieces):
    q_ref = refs[0]
    k_refs = refs[1:1 + n_pieces]
    v_refs = refs[1 + n_pieces:1 + 2 * n_pieces]
    bias_ref, o_ref = refs[1 + 2 * n_pieces], refs[2 + 2 * n_pieces]
    w = k_refs[0].shape[0]
    lane = lax.broadcasted_iota(jnp.int32, (w, LANES), 1)
    lane_lo = jnp.where(lane < DH_B, 1.0, 0.0)
    sel = (lane_lo.astype(BF16), (1.0 - lane_lo).astype(BF16))

    hq = w // 2
    cols = []
    for half in range(2):
        first = half * hq // CHUNK * CHUNK
        last = ((half + 1) * hq - 1) // CHUNK * CHUNK + WINDOW + CHUNK
        cols.append([slice((max(first, p * w) - p * w) // LANES * LANES,
                           -(-(min(last, (p + 1) * w) - p * w) // LANES) * LANES) for p in range(n_pieces)])

    for hp in range(HB // 2):
        ls = slice(hp * LANES, (hp + 1) * LANES)
        q2 = q_ref[:, ls]
        k2 = [kr[:, ls] for kr in k_refs]
        v2 = [vr[:, ls] for vr in v_refs]
        acc = None
        for e in range(2):
            hd = 2 * hp + e
            qm = q2 * sel[e]
            raw = [_dot_nt(qm, k2[p]) for p in range(n_pieces)]
            ex = [[None, None] for _ in range(n_pieces)]
            for half in range(2):
                rs = slice(half * hq, (half + 1) * hq)
                live = [cols[half][p] for p in range(n_pieces)]
                s = [raw[p][rs, live[p]] + bias_ref[hd, rs, p * w + live[p].start:p * w + live[p].stop]
                     for p in range(n_pieces)]
                m = functools.reduce(jnp.maximum, [jnp.max(sp, axis=-1, keepdims=True) for sp in s])
                for p in range(n_pieces):
                    pr = jnp.exp2(s[p] - m).astype(BF16)
                    if live[p].start > 0:
                        pr = jnp.concatenate([jnp.zeros((hq, live[p].start), BF16), pr], axis=1)
                    if live[p].stop < w:
                        pr = jnp.concatenate([pr, jnp.zeros((hq, w - live[p].stop), BF16)], axis=1)
                    ex[p][half] = pr
            for p in range(n_pieces):
                part = _dot(jnp.concatenate(ex[p], axis=0), jnp.concatenate([v2[p] * sel[e], sel[e]], axis=1))
                acc = part if acc is None else acc + part
        o_ref[:, ls] = (acc[:, :LANES] / acc[:, LANES:]).astype(o_ref.dtype)


def _band_sample_kernel(q_ref, kc_ref, kn_ref, vc_ref, vn_ref, kt_ref, vt_ref, bias_ref, *rest):
    o_ref, ko_ref, vo_ref = rest[-3:]
    if len(rest) == 5:
        ko_ref[:-1] = rest[0][...]
        vo_ref[:-1] = rest[1][...]
    s, lc = q_ref.shape[0], kc_ref.shape[1]
    lane = lax.broadcasted_iota(jnp.int32, (s, LANES), 1)
    lane_lo = jnp.where(lane < DH_B, 1.0, 0.0)
    lane_sel = (lane_lo.astype(BF16), (1.0 - lane_lo).astype(BF16))
    row = lax.broadcasted_iota(jnp.int32, (LANES, lc), 0)
    row_lo = jnp.where(row < DH_B, 1.0, 0.0)
    row_sel = (row_lo.astype(BF16), (1.0 - row_lo).astype(BF16))

    for hp in range(HB // 2):
        ls = slice(hp * LANES, (hp + 1) * LANES)
        q2 = q_ref[:, ls]
        kc2 = kc_ref[ls, :].astype(BF16)
        vc2 = vc_ref[ls, :].astype(BF16)
        kn2, vn2 = kn_ref[:, ls], vn_ref[:, ls]
        acc = None
        for e in range(2):
            hd = 2 * hp + e
            qm = q2 * lane_sel[e]
            s_c = _dot(qm, kc2) + bias_ref[hd, :, :lc]
            s_n = _dot_nt(qm, kn2) + bias_ref[hd, :, lc:]
            m = jnp.maximum(jnp.max(s_c, axis=-1, keepdims=True), jnp.max(s_n, axis=-1, keepdims=True))
            ex_c = jnp.exp2(s_c - m).astype(BF16)
            ex_n = jnp.exp2(s_n - m).astype(BF16)
            part = _dot_nt(ex_c, jnp.concatenate([vc2 * row_sel[e], row_sel[e]], axis=0)) \
                + _dot(ex_n, jnp.concatenate([vn2 * lane_sel[e], lane_sel[e]], axis=1))
            acc = part if acc is None else acc + part
        o_ref[:, ls] = (acc[:, :LANES] / acc[:, LANES:]).astype(o_ref.dtype)

    col = lax.broadcasted_iota(jnp.int32, (WB, lc), 1)
    pad_rows = jnp.zeros((LANES - s, WB), F32)
    pad_cols = jnp.zeros((WB, lc - LANES), F32)
    for c_ref, t_ref, out_ref in ((kc_ref, kt_ref, ko_ref), (vc_ref, vt_ref, vo_ref)):
        new_t = jnp.concatenate([t_ref[...], pad_rows], axis=0).T
        new_t = jnp.concatenate([pad_cols, pltpu.roll(new_t, LANES - s, 1)], axis=1)
        out_ref[out_ref.shape[0] - 1] = jnp.where(col >= lc - s, new_t, pltpu.roll(c_ref[...], lc - s, 1))


def _band_prompt(q, k, v, bias, layer, tq):
    bsz, t, _ = q.shape
    n_pieces = WINDOW // tq + 1
    qspec = pl.BlockSpec((None, tq, WB), lambda i, b: (b, i, 0))

    def kspec(p):
        back = n_pieces - 1 - p
        return pl.BlockSpec((None, tq, WB), lambda i, b: (b, jnp.maximum(i - back, 0), 0))

    kspecs = [kspec(p) for p in range(n_pieces)]
    bspec = pl.BlockSpec((None, None) + bias.shape[2:],
                         lambda i, b: (layer, jnp.minimum(i, n_pieces - 1), 0, 0, 0))
    return pl.pallas_call(
        functools.partial(_band_prompt_kernel, n_pieces=n_pieces),
        grid=(t // tq, bsz),
        in_specs=[qspec] + kspecs + kspecs + [bspec],
        out_specs=qspec,
        out_shape=jax.ShapeDtypeStruct((bsz, t, WB), BF16),
        compiler_params=_params(("arbitrary", "arbitrary")),
        name="band_prompt",
    )(q, *([k] * n_pieces), *([v] * n_pieces), bias)


def _band_sample(q, kc_all, vc_all, layer, kn, vn, kt, vt, bias, prev):
    bsz, s, _ = q.shape
    lc = kc_all.shape[3]
    assert s <= LANES <= lc and lc % LANES == 0
    assert (prev is None) == (layer == 0)
    nspec = pl.BlockSpec((None, s, WB), lambda b: (b, 0, 0))
    cspec = pl.BlockSpec((None, None, WB, lc), lambda b: (layer, b, 0, 0))
    stacked = lambda n: pl.BlockSpec((n, None, WB, lc), lambda b: (0, b, 0, 0))
    buf = jax.ShapeDtypeStruct((layer + 1, bsz, WB, lc), F32)
    prev = () if prev is None else tuple(prev)
    return pl.pallas_call(
        _band_sample_kernel,
        grid=(bsz,),
        in_specs=[nspec, cspec, nspec, cspec, nspec, nspec, nspec, _resident(bias.shape)]
        + [stacked(layer)] * len(prev),
        out_specs=[nspec, stacked(layer + 1), stacked(layer + 1)],
        out_shape=[jax.ShapeDtypeStruct((bsz, s, WB), BF16), buf, buf],
        compiler_params=_params(("arbitrary",)),
        name="band_sample",
    )(q, kc_all, kn, vc_all, vn, kt, vt, bias, *prev)


def _post_kernel(oa_ref, sga_ref, ob_ref, sgb_ref, sma_ref, smb_ref, x_ref, p_ref,
                 gg_ref, wa, wb, wo, pg_ref, wpg, wp, out_ref):
    tm = oa_ref.shape[0]
    halves = (slice(0, tm // 2), slice(tm // 2, tm))

    def gla_normed(rs):
        oa = oa_ref[rs, :].astype(F32)
        segs = []
        for hh in range(HA):
            seg = oa[:, hh * DV_A:(hh + 1) * DV_A]
            ms = jnp.mean(seg * seg, axis=-1, keepdims=True)
            segs.append(seg * lax.rsqrt(ms + EPS) * gg_ref[...])
        return jnp.concatenate(segs, axis=-1)

    ya = [_dot((gla_normed(rs) * sga_ref[rs, :]).astype(BF16), wa[...]) for rs in halves]
    yb = [_dot((ob_ref[rs, :] * sgb_ref[rs, :]).astype(BF16), wb[...]) for rs in halves]
    m = [(sma_ref[rs, :] * a + smb_ref[rs, :] * b).astype(BF16) for rs, a, b in zip(halves, ya, yb)]
    x1 = [x_ref[rs, :] + _dot(mm, wo[...]) for rs, mm in zip(halves, m)]
    hn = [(x * lax.rsqrt(jnp.mean(x * x, axis=-1, keepdims=True) + EPS) * pg_ref[...]).astype(BF16) for x in x1]
    gate = [jax.nn.sigmoid(_dot(h, wpg[...])) for h in hn]
    pe = [_dot(p_ref[rs, :].astype(BF16), wp[...]) for rs in halves]
    for rs, x, g, e in zip(halves, x1, gate, pe):
        out_ref[rs, :] = x + g * e


def _post(oa, sga, ob, sgb, sma, smb, x2d, p_all, layer, lw, tm):
    rows = x2d.shape[0]
    row = lambda w: pl.BlockSpec((tm, w), lambda i: (i, 0))
    pspec = pl.BlockSpec((None, tm, P_DIM), lambda i: (layer, i, 0))
    weights = (lw['gg'], lw['wa'], lw['wb'], lw['wo'], lw['pg'], lw['wpg'], lw['wp'])
    return pl.pallas_call(
        _post_kernel,
        grid=(rows // tm,),
        in_specs=[row(D_MODEL)] * 7 + [pspec] + [_resident(w.shape) for w in weights],
        out_specs=row(D_MODEL),
        out_shape=jax.ShapeDtypeStruct((rows, D_MODEL), F32),
        compiler_params=_params(("parallel",)),
        name="post",
    )(oa, sga, ob, sgb, sma, smb, x2d, p_all, *weights)


def _feature_major_w_in(w_in):
    assert w_in.shape[1:] == (D_MODEL, N_IN)
    return jnp.transpose(w_in, (0, 2, 1)).astype(BF16)


def _layer_weights(i, norm_g, w_gate_up, b_gate, gla_norm_g, q_norm_g, k_norm_g,
                   w_branch_a, w_branch_b, w_out, ple_norm_g, w_ple_gate, w_ple):
    lw = {}
    lw['wgu'] = jnp.pad(w_gate_up[i].astype(BF16), ((0, GATE_PAD - GATE_RANK), (0, 0)))
    lw['bg'] = b_gate[i].reshape(1, WA_K)
    lw['ng'] = norm_g[i].reshape(1, D_MODEL)
    lw['qg'] = jnp.tile(q_norm_g[i], HB).reshape(1, WB)
    lw['kg'] = jnp.tile(k_norm_g[i], HB).reshape(1, WB)
    lw['gg'] = gla_norm_g[i].reshape(1, DV_A)
    lw['wa'] = w_branch_a[i].astype(BF16)
    lw['wb'] = w_branch_b[i].astype(BF16)
    lw['wo'] = w_out[i].astype(BF16)
    lw['pg'] = ple_norm_g[i].reshape(1, D_MODEL)
    lw['wpg'] = w_ple_gate[i].astype(BF16)
    lw['wp'] = w_ple[i].astype(BF16)
    return lw


def _bias_kernel(vec_ref, o_ref, *, tq, nk):
    n_var = o_ref.shape[0]
    period = vec_ref.shape[-1]
    x = jnp.broadcast_to(vec_ref[...], (tq, period))
    toep = pltpu.roll(x, 0, 1, stride=1, stride_axis=0)[:, :nk]
    r = lax.broadcasted_iota(jnp.int32, (tq, nk), 0)
    c = lax.broadcasted_iota(jnp.int32, (tq, nk), 1)
    shift = CHUNK.bit_length() - 1
    dchunk = lax.shift_right_logical(r + WINDOW, shift) - lax.shift_right_logical(c, shift)
    visible = (dchunk >= 0) & (dchunk <= LEFT_CHUNKS)
    for j in range(n_var):
        o_ref[j] = jnp.where(visible & (c >= (n_var - 1 - j) * tq), toep, NEG)


def _band_bias(rel_bias, tq):
    depth = rel_bias.shape[0]
    nk = WINDOW + tq
    n_var = nk // tq
    period = nk + tq
    tab = rel_bias * LOG2E
    n_hi = WINDOW - MAX_REL
    far = jnp.broadcast_to(tab[..., -1:], (depth, HB, n_hi))
    near = jnp.broadcast_to(tab[..., :1], (depth, HB, max(nk - n_hi - (2 * MAX_REL + 1), 0)))
    wrap = jnp.broadcast_to(tab[..., -1:], (depth, HB, period - nk))
    body = jnp.concatenate([far, tab[..., ::-1], near], axis=-1)[..., :nk]
    vec = jnp.concatenate([body, wrap], axis=-1).reshape(depth, HB, 1, period)
    return pl.pallas_call(
        functools.partial(_bias_kernel, tq=tq, nk=nk),
        grid=(depth, HB),
        in_specs=[pl.BlockSpec((None, None, 1, period), lambda l, h: (l, h, 0, 0))],
        out_specs=pl.BlockSpec((None, n_var, None, tq, nk), lambda l, h: (l, 0, h, 0, 0)),
        out_shape=jax.ShapeDtypeStruct((depth, n_var, HB, tq, nk), F32),
        compiler_params=_params(("parallel", "parallel")),
        name="band_bias",
    )(vec)


BAND_TQ = 256


def _tiles(rows, seq):
    return dict(tm=min(rows, 512), tm_post=min(rows, 1024), tb=min(seq, 1024))


def _layer(x, p_all, s0_all, kc, vc, prev, bias, w_al, layer, lw):
    bsz, t, _ = x.shape
    x2d = x.reshape(bsz * t, D_MODEL)
    keep = min(WINDOW, t)
    tiles = _tiles(bsz * t, t)
    tm, tm_post, tb, tq = tiles['tm'], tiles['tm_post'], tiles['tb'], BAND_TQ
    qa, ka, va, lg, sga, qb, kb, vb, sgb, sma, smb, ktail, vtail = _in_proj(x2d, w_al, layer, lw, tm, t, keep)
    r3 = lambda a: a.reshape(bsz, -1, a.shape[-1])
    if s0_all is None:
        s0_all, s_layer = jnp.zeros((1, bsz, HA, DK_A, DV_A), F32), 0
    else:
        s_layer = layer
    if kc is None:
        oa, (st, kbuf, vbuf) = _gla(r3(qa), r3(ka), r3(lg), r3(va), s0_all, s_layer, tb, (ktail, vtail), prev)
        ob = _band_prompt(r3(qb), r3(kb), r3(vb), bias, layer, tq)
    else:
        oa, (st,) = _gla(r3(qa), r3(ka), r3(lg), r3(va), s0_all, s_layer, tb, (), None if prev is None else prev[:1])
        lc = kc.shape[3]
        ob, kbuf, vbuf = _band_sample(r3(qb), kc, vc, layer, r3(kb), r3(vb), r3(ktail), r3(vtail),
                                      bias[layer, -1, :, :t, :lc + t], None if prev is None else prev[1:])
    x_new = _post(oa.reshape(bsz * t, WA_V), sga, ob.reshape(bsz * t, WB), sgb, sma, smb, x2d,
                  p_all.reshape(p_all.shape[0], bsz * t, P_DIM), layer, lw, tm_post)
    return x_new.reshape(bsz, t, D_MODEL), (st, kbuf, vbuf)


def kernel(x_prompt, x_sample, state_gla, cache_band_k, cache_band_v, p_prompt, p_sample,
           norm_g, w_in, w_gate_up, b_gate, gla_norm_g, q_norm_g, k_norm_g, rel_bias,
           w_branch_a, w_branch_b, w_out, ple_norm_g, w_ple_gate, w_ple):
    depth = w_in.shape[0]
    bias = _band_bias(rel_bias, BAND_TQ)
    w_al = _feature_major_w_in(w_in)
    xp, xs = x_prompt, x_sample

    def feature_major(a):
        return jnp.transpose(a, (0, 1, 3, 4, 2)).reshape(a.shape[:2] + (WB, a.shape[2]))

    def frame_major(a):
        return jnp.transpose(a.reshape(a.shape[:2] + (HB, DH_B, a.shape[3])), (0, 1, 4, 2, 3))

    kc, vc = feature_major(cache_band_k), feature_major(cache_band_v)
    prompt_out = sample_out = None
    for i in range(depth):
        lw = _layer_weights(i, norm_g, w_gate_up, b_gate, gla_norm_g, q_norm_g, k_norm_g,
                            w_branch_a, w_branch_b, w_out, ple_norm_g, w_ple_gate, w_ple)
        xp, prompt_out = _layer(xp, p_prompt, None, None, None, prompt_out, bias, w_al, i, lw)
        xs, sample_out = _layer(xs, p_sample, state_gla, kc, vc, sample_out, bias, w_al, i, lw)
    (sp, kbp, vbp), (ss, kbs, vbs) = prompt_out, sample_out
    return (xp, xs, sp, frame_major(kbp), frame_major(vbp), ss, frame_major(kbs), frame_major(vbs))
```

```python
import functools

import numpy as np
import jax
import jax.numpy as jnp
from jax import lax
from jax.experimental import pallas as pl
from jax.experimental.pallas import tpu as pltpu

D_MODEL = 1024
CHUNK = 64
P_DIM = 256
EPS = 1e-6
NEG = -1e30
HA = 4
DK_A = 128
DV_A = 256
GATE_RANK = 16
GATE_TAU = 16.0
WA_K = HA * DK_A
WA_V = HA * DV_A
HB = 16
DH_B = 64
LEFT_CHUNKS = 8
WINDOW = LEFT_CHUNKS * CHUNK
MAX_REL = 128
WB = HB * DH_B

LANES = 128
GATE_PAD = LANES
SUB = 16
VMEM_LIMIT = 60 * 1024 * 1024
LOG2E = 1.4426950408889634

F32 = jnp.float32
BF16 = jnp.bfloat16


def _dot(a, b):
    return jnp.dot(a, b, preferred_element_type=F32)


def _dot_nt(a, b):
    return lax.dot_general(a, b, (((1,), (1,)), ((), ())), preferred_element_type=F32)


def _resident(shape):
    return pl.BlockSpec(shape, lambda *_: (0,) * len(shape), pipeline_mode=pl.Buffered(1))


def _params(sem):
    return pltpu.CompilerParams(dimension_semantics=sem, vmem_limit_bytes=VMEM_LIMIT)


_W_ROWS = {}
_off = 0
for _name, _width in (('qa', WA_K), ('ka', WA_K), ('va', WA_V), ('ra', GATE_RANK), ('ga', WA_V), ('qb', WB),
                      ('kb', WB), ('vb', WB), ('gb', WB), ('mga', D_MODEL), ('mgb', D_MODEL)):
    _W_ROWS[_name] = slice(_off, _off + (GATE_PAD if _name == 'ra' else _width))
    _off += _width
N_IN = _off


_ACT_COLS = {}
_off = 0
for _name, _width in (('qa', WA_K), ('ka', WA_K), ('va', WA_V), ('sga', WA_V), ('qb', WB), ('kb', WB), ('vb', WB),
                      ('sgb', WB), ('sma', D_MODEL), ('smb', D_MODEL)):
    assert _off % _width == 0
    _ACT_COLS[_name] = slice(_off, _off + _width)
    _off += _width
ACT_W = _off


def _act_block(name):
    cols = _ACT_COLS[name]
    return cols.start // (cols.stop - cols.start)


def _in_proj_kernel(x_ref, ng_ref, w_ref, wgu, bg_ref, qg_ref, kg_ref, act_o, lg_o, kt_o, vt_o, *, tail_steps):
    x = x_ref[...]
    ms = jnp.mean(x * x, axis=-1, keepdims=True)
    h = (x * lax.rsqrt(ms + EPS) * ng_ref[...]).astype(BF16)
    proj = lambda name: _dot_nt(h, w_ref[_W_ROWS[name], :])

    def put(name, value):
        act_o[:, _ACT_COLS[name]] = value.astype(BF16)

    put('qa', proj('qa') * (DK_A ** -0.5))
    put('ka', proj('ka'))
    put('va', proj('va'))

    ra = proj('ra')
    gl = _dot(ra.astype(BF16), wgu[...]) + bg_ref[...]
    lg_o[...] = (jnp.minimum(gl, 0.0) - jnp.log(1.0 + jnp.exp(-jnp.abs(gl)))) * (1.0 / GATE_TAU)

    ga = proj('ga')
    put('sga', ga * jax.nn.sigmoid(ga))

    low = lax.broadcasted_iota(jnp.int32, (x.shape[0], LANES), 1) < DH_B

    def head_norm(z, g_ref):
        zz = z * z
        scales = []
        for c in range(WB // LANES):
            t = zz[:, c * LANES:(c + 1) * LANES]
            ss_lo = jnp.sum(jnp.where(low, t, 0.0), axis=-1, keepdims=True)
            ss_hi = jnp.sum(jnp.where(low, 0.0, t), axis=-1, keepdims=True)
            scales.append(jnp.where(low, lax.rsqrt(ss_lo * (1.0 / DH_B) + EPS),
                                    lax.rsqrt(ss_hi * (1.0 / DH_B) + EPS)))
        return z * jnp.concatenate(scales, axis=1) * g_ref[...]

    put('qb', head_norm(proj('qb'), qg_ref) * (DH_B ** -0.5 * LOG2E))
    kb = head_norm(proj('kb'), kg_ref)
    put('kb', kb)
    vb = proj('vb')
    put('vb', vb)
    gb = proj('gb')
    put('sgb', gb * jax.nn.sigmoid(gb))
    put('sma', jax.nn.sigmoid(proj('mga')))
    put('smb', jax.nn.sigmoid(proj('mgb')))

    if tail_steps is None:
        kt_o[...] = kb
        vt_o[...] = vb
    else:
        per_seq, per_tail = tail_steps

        @pl.when(pl.program_id(0) % per_seq >= per_seq - per_tail)
        def _():
            kt_o[...] = kb.T
            vt_o[...] = vb.T


def _in_proj(x2d, w_al, layer, lw, tm, seq, keep):
    rows = x2d.shape[0]
    row = lambda w: pl.BlockSpec((tm, w), lambda i: (i, 0))
    widths = (ACT_W, WA_K)
    dtypes = (BF16, F32)
    if keep == seq:
        tail_steps = None
        tail = row(WB)
        tail_shape = jax.ShapeDtypeStruct((rows, WB), F32)
    else:
        assert keep % tm == 0 and seq % tm == 0
        per_seq, per_tail = seq // tm, keep // tm
        tail_steps = (per_seq, per_tail)
        tail = pl.BlockSpec((None, WB, tm), lambda i: (i // per_seq, 0,
                                                       jnp.maximum(i % per_seq - (per_seq - per_tail), 0)))
        tail_shape = jax.ShapeDtypeStruct((rows // seq, WB, keep), F32)
    wspec = pl.BlockSpec((None,) + w_al.shape[1:], lambda i: (layer, 0, 0), pipeline_mode=pl.Buffered(1))
    small = (lw['wgu'], lw['bg'], lw['qg'], lw['kg'])
    return pl.pallas_call(
        functools.partial(_in_proj_kernel, tail_steps=tail_steps),
        grid=(rows // tm,),
        in_specs=[row(D_MODEL), _resident(lw['ng'].shape), wspec] + [_resident(w.shape) for w in small],
        out_specs=[row(w) for w in widths] + [tail, tail],
        out_shape=[jax.ShapeDtypeStruct((rows, w), dt) for w, dt in zip(widths, dtypes)] + [tail_shape] * 2,
        compiler_params=_params(("arbitrary",)),
        name="in_proj",
    )(x2d, lw['ng'], w_al, *small)


def _gla_block(q_ref, k_ref, lg_ref, v_ref, o_ref, states):
    tb = q_ref.shape[0]
    n_chunks = tb // CHUNK
    n_sub = CHUNK // SUB
    shift = CHUNK.bit_length() - 1
    gw = min(tb, 2 * CHUNK)
    r = lax.broadcasted_iota(jnp.int32, (gw, gw), 0)
    cc = lax.broadcasted_iota(jnp.int32, (gw, gw), 1)
    same_chunk = lax.shift_right_logical(r, shift) == lax.shift_right_logical(cc, shift)
    ltri = jnp.where(same_chunk & (cc <= r), 1.0, 0.0).astype(BF16)
    ltri2 = jnp.concatenate([ltri, ltri], axis=1)
    b_groups = []
    for g0 in range(0, tb, gw):
        lg = lg_ref[g0:g0 + gw, :]
        lg_hi = lg.astype(BF16)
        lg_lo = (lg - lg_hi.astype(F32)).astype(BF16)
        b_groups.append(_dot(ltri2, jnp.concatenate([lg_hi, lg_lo], axis=0)))

    ar = lax.broadcasted_iota(jnp.int32, (CHUNK, n_sub * CHUNK), 0)
    ac = lax.broadcasted_iota(jnp.int32, (CHUNK, n_sub * CHUNK), 1)
    sub_shift = SUB.bit_length() - 1
    a_keep = (lax.shift_right_logical(ar, sub_shift) == lax.shift_right_logical(ac, shift)) \
        & ((ac & (CHUNK - 1)) <= ar)
    krow = lax.broadcasted_iota(jnp.int32, (CHUNK, DK_A), 0)

    for ci in range(n_chunks):
        rows = slice(ci * CHUNK, (ci + 1) * CHUNK)
        for hh in range(HA):
            ks = slice(hh * DK_A, (hh + 1) * DK_A)
            vs = slice(hh * DV_A, (hh + 1) * DV_A)
            g_row = ci * CHUNK % gw
            b = b_groups[ci * CHUNK // gw][g_row:g_row + CHUNK, ks]
            q = q_ref[rows, ks].astype(F32)
            k = k_ref[rows, ks].astype(F32)
            v16 = v_ref[rows, vs]
            b_last = b[CHUNK - 1:CHUNK, :]
            o = _dot((q * jnp.exp(b)).astype(BF16), states[hh].astype(BF16))
            refs_b = [b[sb * SUB:sb * SUB + 1, :] for sb in range(n_sub)]
            ref_rows = jnp.concatenate([jnp.broadcast_to(rb, (SUB, DK_A)) for rb in refs_b], axis=0)
            qt = (q * jnp.exp(b - ref_rows)).astype(BF16)
            kt = jnp.concatenate(
                [(k * jnp.exp(jnp.where(krow < (sb + 1) * SUB, refs_b[sb] - b, 0.0))).astype(BF16)
                 for sb in range(n_sub)], axis=0)
            a = jnp.where(a_keep, _dot_nt(qt, kt), 0.0).astype(BF16)
            o_ref[rows, vs] = (o + _dot(a, jnp.concatenate([v16] * n_sub, axis=0))).astype(o_ref.dtype)
            k_dec_t = (k * jnp.exp(b_last - b)).T
            decay = jnp.exp(b.T[:, CHUNK - 1:CHUNK])
            states[hh] = states[hh] * decay + _dot(k_dec_t.astype(BF16), v16)


def _gla_kernel(q_ref, k_ref, lg_ref, v_ref, s0_ref, *rest, n_extra):
    new_refs = rest[:n_extra]
    has_prev = len(rest) == 3 * n_extra + 4
    prev_refs = rest[n_extra:2 * n_extra + 1] if has_prev else ()
    o_ref = rest[-(n_extra + 3)]
    out_refs = rest[-(n_extra + 2):-1]
    st_ref = rest[-1]
    c = pl.program_id(1)

    @pl.when(c == 0)
    def _():
        st_ref[...] = s0_ref[...]

    states = [st_ref[hh] for hh in range(HA)]
    _gla_block(q_ref, k_ref, lg_ref, v_ref, o_ref, states)
    for hh in range(HA):
        st_ref[hh] = states[hh]

    @pl.when(c == pl.num_programs(1) - 1)
    def _():
        last = out_refs[0].shape[0] - 1
        for new, out in zip((st_ref,) + tuple(new_refs), out_refs):
            out[last] = new[...]
        for prev, out in zip(prev_refs, out_refs):
            out[:last] = prev[...]


def _gla(act, lg, s0, s0_layer, tb, extras, prev):
    bsz, t, _ = act.shape
    n_prev = 0 if prev is None else prev[0].shape[0]
    group = lambda name, w: pl.BlockSpec((None, tb, w), lambda b, c: (b, c, _act_block(name)))
    kspec = pl.BlockSpec((None, tb, WA_K), lambda b, c: (b, c, 0))
    vspec = pl.BlockSpec((None, tb, WA_V), lambda b, c: (b, c, 0))
    s0spec = pl.BlockSpec((None, None, HA, DK_A, DV_A), lambda b, c: (s0_layer, b, 0, 0, 0))

    def per_seq(shape, lead):
        zeros = (0,) * len(shape)
        if lead is None:
            return pl.BlockSpec((None,) + shape, lambda b, c: (b,) + zeros)
        return pl.BlockSpec((lead, None) + shape, lambda b, c: (0, b) + zeros)

    stacked_shapes = [(HA, DK_A, DV_A)] + [e.shape[1:] for e in extras]
    prev = () if prev is None else tuple(prev)
    outs = pl.pallas_call(
        functools.partial(_gla_kernel, n_extra=len(extras)),
        grid=(bsz, t // tb),
        in_specs=[group('qa', WA_K), group('ka', WA_K), kspec, group('va', WA_V), s0spec]
        + [per_seq(e.shape[1:], None) for e in extras]
        + [per_seq(s, n_prev) for s in stacked_shapes[:len(prev)]],
        out_specs=[vspec] + [per_seq(s, n_prev + 1) for s in stacked_shapes],
        out_shape=[jax.ShapeDtypeStruct((bsz, t, WA_V), BF16)]
        + [jax.ShapeDtypeStruct((n_prev + 1, bsz) + s, F32) for s in stacked_shapes],
        scratch_shapes=[pltpu.VMEM((HA, DK_A, DV_A), F32)],
        compiler_params=_params(("arbitrary", "arbitrary")),
        name="gla",
    )(act, act, lg, act, s0, *extras, *prev)
    return outs[0], outs[1:]


def _band_prompt_kernel(*refs, n_pieces):
    q_ref = refs[0]
    k_refs = refs[1:1 + n_pieces]
    v_refs = refs[1 + n_pieces:1 + 2 * n_pieces]
    bias_ref, o_ref = refs[1 + 2 * n_pieces], refs[2 + 2 * n_pieces]
    w = k_refs[0].shape[0]
    lane = lax.broadcasted_iota(jnp.int32, (w, LANES), 1)
    lane_lo = jnp.where(lane < DH_B, 1.0, 0.0)
    sel = (lane_lo.astype(BF16), (1.0 - lane_lo).astype(BF16))

    hq = w // 2
    cols = []
    for half in range(2):
        first = half * hq // CHUNK * CHUNK
        last = ((half + 1) * hq - 1) // CHUNK * CHUNK + WINDOW + CHUNK
        cols.append([slice((max(first, p * w) - p * w) // LANES * LANES,
                           -(-(min(last, (p + 1) * w) - p * w) // LANES) * LANES) for p in range(n_pieces)])

    for hp in range(HB // 2):
        ls = slice(hp * LANES, (hp + 1) * LANES)
        q2 = q_ref[:, ls]
        k2 = [kr[:, ls] for kr in k_refs]
        v2 = [vr[:, ls] for vr in v_refs]
        acc = None
        for e in range(2):
            hd = 2 * hp + e
            qm = q2 * sel[e]
            raw = [_dot_nt(qm, k2[p]) for p in range(n_pieces)]
            ex = [[None, None] for _ in range(n_pieces)]
            for half in range(2):
                rs = slice(half * hq, (half + 1) * hq)
                live = [cols[half][p] for p in range(n_pieces)]
                s = [raw[p][rs, live[p]] + bias_ref[hd, rs, p * w + live[p].start:p * w + live[p].stop]
                     for p in range(n_pieces)]
                m = functools.reduce(jnp.maximum, [jnp.max(sp, axis=-1, keepdims=True) for sp in s])
                for p in range(n_pieces):
                    pr = jnp.exp2(s[p] - m).astype(BF16)
                    if live[p].start > 0:
                        pr = jnp.concatenate([jnp.zeros((hq, live[p].start), BF16), pr], axis=1)
                    if live[p].stop < w:
                        pr = jnp.concatenate([pr, jnp.zeros((hq, w - live[p].stop), BF16)], axis=1)
                    ex[p][half] = pr
            for p in range(n_pieces):
                part = _dot(jnp.concatenate(ex[p], axis=0), jnp.concatenate([v2[p] * sel[e], sel[e]], axis=1))
                acc = part if acc is None else acc + part
        o_ref[:, ls] = (acc[:, :LANES] / acc[:, LANES:]).astype(o_ref.dtype)


def _band_sample_kernel(q_ref, kc_ref, kn_ref, vc_ref, vn_ref, kt_ref, vt_ref, bias_ref, *rest):
    o_ref, ko_ref, vo_ref = rest[-3:]
    if len(rest) == 5:
        ko_ref[:-1] = rest[0][...]
        vo_ref[:-1] = rest[1][...]
    s, lc = q_ref.shape[0], kc_ref.shape[1]
    lane = lax.broadcasted_iota(jnp.int32, (s, LANES), 1)
    lane_lo = jnp.where(lane < DH_B, 1.0, 0.0)
    lane_sel = (lane_lo.astype(BF16), (1.0 - lane_lo).astype(BF16))
    row = lax.broadcasted_iota(jnp.int32, (LANES, lc), 0)
    row_lo = jnp.where(row < DH_B, 1.0, 0.0)
    row_sel = (row_lo.astype(BF16), (1.0 - row_lo).astype(BF16))

    for hp in range(HB // 2):
        ls = slice(hp * LANES, (hp + 1) * LANES)
        q2 = q_ref[:, ls]
        kc2 = kc_ref[ls, :].astype(BF16)
        vc2 = vc_ref[ls, :].astype(BF16)
        kn2, vn2 = kn_ref[:, ls], vn_ref[:, ls]
        acc = None
        for e in range(2):
            hd = 2 * hp + e
            qm = q2 * lane_sel[e]
            s_c = _dot(qm, kc2) + bias_ref[hd, :, :lc]
            s_n = _dot_nt(qm, kn2) + bias_ref[hd, :, lc:]
            m = jnp.maximum(jnp.max(s_c, axis=-1, keepdims=True), jnp.max(s_n, axis=-1, keepdims=True))
            ex_c = jnp.exp2(s_c - m).astype(BF16)
            ex_n = jnp.exp2(s_n - m).astype(BF16)
            part = _dot_nt(ex_c, jnp.concatenate([vc2 * row_sel[e], row_sel[e]], axis=0)) \
                + _dot(ex_n, jnp.concatenate([vn2 * lane_sel[e], lane_sel[e]], axis=1))
            acc = part if acc is None else acc + part
        o_ref[:, ls] = (acc[:, :LANES] / acc[:, LANES:]).astype(o_ref.dtype)

    col = lax.broadcasted_iota(jnp.int32, (WB, lc), 1)
    pad_rows = jnp.zeros((LANES - s, WB), F32)
    pad_cols = jnp.zeros((WB, lc - LANES), F32)
    for c_ref, t_ref, out_ref in ((kc_ref, kt_ref, ko_ref), (vc_ref, vt_ref, vo_ref)):
        new_t = jnp.concatenate([t_ref[...], pad_rows], axis=0).T
        new_t = jnp.concatenate([pad_cols, pltpu.roll(new_t, LANES - s, 1)], axis=1)
        out_ref[out_ref.shape[0] - 1] = jnp.where(col >= lc - s, new_t, pltpu.roll(c_ref[...], lc - s, 1))


def _band_prompt(act, bias, layer, tq):
    bsz, t, _ = act.shape
    n_pieces = WINDOW // tq + 1
    ospec = pl.BlockSpec((None, tq, WB), lambda i, b: (b, i, 0))
    qspec = pl.BlockSpec((None, tq, WB), lambda i, b: (b, i, _act_block('qb')))

    def kspec(name, p):
        back = n_pieces - 1 - p
        return pl.BlockSpec((None, tq, WB), lambda i, b: (b, jnp.maximum(i - back, 0), _act_block(name)))

    kspecs = [kspec(name, p) for name in ('kb', 'vb') for p in range(n_pieces)]
    bspec = pl.BlockSpec((None, None) + bias.shape[2:],
                         lambda i, b: (layer, jnp.minimum(i, n_pieces - 1), 0, 0, 0))
    return pl.pallas_call(
        functools.partial(_band_prompt_kernel, n_pieces=n_pieces),
        grid=(t // tq, bsz),
        in_specs=[qspec] + kspecs + [bspec],
        out_specs=ospec,
        out_shape=jax.ShapeDtypeStruct((bsz, t, WB), BF16),
        compiler_params=_params(("arbitrary", "arbitrary")),
        name="band_prompt",
    )(*([act] * (1 + 2 * n_pieces)), bias)


def _band_sample(act, kc_all, vc_all, layer, kt, vt, bias, prev):
    bsz, s, _ = act.shape
    lc = kc_all.shape[3]
    assert s <= LANES <= lc and lc % LANES == 0
    assert (prev is None) == (layer == 0)
    nspec = pl.BlockSpec((None, s, WB), lambda b: (b, 0, 0))
    group = lambda name: pl.BlockSpec((None, s, WB), lambda b: (b, 0, _act_block(name)))
    cspec = pl.BlockSpec((None, None, WB, lc), lambda b: (layer, b, 0, 0))
    stacked = lambda n: pl.BlockSpec((n, None, WB, lc), lambda b: (0, b, 0, 0))
    buf = jax.ShapeDtypeStruct((layer + 1, bsz, WB, lc), F32)
    prev = () if prev is None else tuple(prev)
    return pl.pallas_call(
        _band_sample_kernel,
        grid=(bsz,),
        in_specs=[group('qb'), cspec, group('kb'), cspec, group('vb'), nspec, nspec, _resident(bias.shape)]
        + [stacked(layer)] * len(prev),
        out_specs=[nspec, stacked(layer + 1), stacked(layer + 1)],
        out_shape=[jax.ShapeDtypeStruct((bsz, s, WB), BF16), buf, buf],
        compiler_params=_params(("arbitrary",)),
        name="band_sample",
    )(act, kc_all, act, vc_all, act, kt, vt, bias, *prev)


def _post_kernel(oa_ref, sga_ref, ob_ref, sgb_ref, sma_ref, smb_ref, x_ref, p_ref,
                 gg_ref, wa, wb, wo, pg_ref, wpg, wp, out_ref):
    tm = oa_ref.shape[0]
    halves = (slice(0, tm // 2), slice(tm // 2, tm))

    def gla_normed(rs):
        oa = oa_ref[rs, :].astype(F32)
        segs = []
        for hh in range(HA):
            seg = oa[:, hh * DV_A:(hh + 1) * DV_A]
            ms = jnp.mean(seg * seg, axis=-1, keepdims=True)
            segs.append(seg * lax.rsqrt(ms + EPS) * gg_ref[...])
        return jnp.concatenate(segs, axis=-1)

    ya = [_dot((gla_normed(rs) * sga_ref[rs, :]).astype(BF16), wa[...]) for rs in halves]
    yb = [_dot((ob_ref[rs, :] * sgb_ref[rs, :]).astype(BF16), wb[...]) for rs in halves]
    m = [(sma_ref[rs, :] * a + smb_ref[rs, :] * b).astype(BF16) for rs, a, b in zip(halves, ya, yb)]
    x1 = [x_ref[rs, :] + _dot(mm, wo[...]) for rs, mm in zip(halves, m)]
    hn = [(x * lax.rsqrt(jnp.mean(x * x, axis=-1, keepdims=True) + EPS) * pg_ref[...]).astype(BF16) for x in x1]
    gate = [jax.nn.sigmoid(_dot(h, wpg[...])) for h in hn]
    pe = [_dot(p_ref[rs, :].astype(BF16), wp[...]) for rs in halves]
    for rs, x, g, e in zip(halves, x1, gate, pe):
        out_ref[rs, :] = x + g * e


def _post(oa, ob, act, x2d, p_all, layer, lw, tm):
    rows = x2d.shape[0]
    row = lambda w: pl.BlockSpec((tm, w), lambda i: (i, 0))
    group = lambda name: pl.BlockSpec((tm, D_MODEL), lambda i: (i, _act_block(name)))
    pspec = pl.BlockSpec((None, tm, P_DIM), lambda i: (layer, i, 0))
    weights = (lw['gg'], lw['wa'], lw['wb'], lw['wo'], lw['pg'], lw['wpg'], lw['wp'])
    return pl.pallas_call(
        _post_kernel,
        grid=(rows // tm,),
        in_specs=[row(WA_V), group('sga'), row(WB), group('sgb'), group('sma'), group('smb'), row(D_MODEL), pspec]
        + [_resident(w.shape) for w in weights],
        out_specs=row(D_MODEL),
        out_shape=jax.ShapeDtypeStruct((rows, D_MODEL), F32),
        compiler_params=_params(("parallel",)),
        name="post",
    )(oa, act, ob, act, act, act, x2d, p_all, *weights)


def _feature_major_w_in(w_in):
    assert w_in.shape[1:] == (D_MODEL, N_IN)
    return jnp.transpose(w_in, (0, 2, 1)).astype(BF16)


def _layer_weights(i, norm_g, w_gate_up, b_gate, gla_norm_g, q_norm_g, k_norm_g,
                   w_branch_a, w_branch_b, w_out, ple_norm_g, w_ple_gate, w_ple):
    lw = {}
    lw['wgu'] = jnp.pad(w_gate_up[i].astype(BF16), ((0, GATE_PAD - GATE_RANK), (0, 0)))
    lw['bg'] = b_gate[i].reshape(1, WA_K)
    lw['ng'] = norm_g[i].reshape(1, D_MODEL)
    lw['qg'] = jnp.tile(q_norm_g[i], HB).reshape(1, WB)
    lw['kg'] = jnp.tile(k_norm_g[i], HB).reshape(1, WB)
    lw['gg'] = gla_norm_g[i].reshape(1, DV_A)
    lw['wa'] = w_branch_a[i].astype(BF16)
    lw['wb'] = w_branch_b[i].astype(BF16)
    lw['wo'] = w_out[i].astype(BF16)
    lw['pg'] = ple_norm_g[i].reshape(1, D_MODEL)
    lw['wpg'] = w_ple_gate[i].astype(BF16)
    lw['wp'] = w_ple[i].astype(BF16)
    return lw


def _bias_kernel(vec_ref, o_ref, *, tq, nk):
    n_var = o_ref.shape[0]
    period = vec_ref.shape[-1]
    x = jnp.broadcast_to(vec_ref[...], (tq, period))
    toep = pltpu.roll(x, 0, 1, stride=1, stride_axis=0)[:, :nk]
    r = lax.broadcasted_iota(jnp.int32, (tq, nk), 0)
    c = lax.broadcasted_iota(jnp.int32, (tq, nk), 1)
    shift = CHUNK.bit_length() - 1
    dchunk = lax.shift_right_logical(r + WINDOW, shift) - lax.shift_right_logical(c, shift)
    visible = (dchunk >= 0) & (dchunk <= LEFT_CHUNKS)
    for j in range(n_var):
        o_ref[j] = jnp.where(visible & (c >= (n_var - 1 - j) * tq), toep, NEG)


def _band_bias(rel_bias, tq):
    depth = rel_bias.shape[0]
    nk = WINDOW + tq
    n_var = nk // tq
    period = nk + tq
    tab = rel_bias * LOG2E
    n_hi = WINDOW - MAX_REL
    far = jnp.broadcast_to(tab[..., -1:], (depth, HB, n_hi))
    near = jnp.broadcast_to(tab[..., :1], (depth, HB, max(nk - n_hi - (2 * MAX_REL + 1), 0)))
    wrap = jnp.broadcast_to(tab[..., -1:], (depth, HB, period - nk))
    body = jnp.concatenate([far, tab[..., ::-1], near], axis=-1)[..., :nk]
    vec = jnp.concatenate([body, wrap], axis=-1).reshape(depth, HB, 1, period)
    return pl.pallas_call(
        functools.partial(_bias_kernel, tq=tq, nk=nk),
        grid=(depth, HB),
        in_specs=[pl.BlockSpec((None, None, 1, period), lambda l, h: (l, h, 0, 0))],
        out_specs=pl.BlockSpec((None, n_var, None, tq, nk), lambda l, h: (l, 0, h, 0, 0)),
        out_shape=jax.ShapeDtypeStruct((depth, n_var, HB, tq, nk), F32),
        compiler_params=_params(("parallel", "parallel")),
        name="band_bias",
    )(vec)


BAND_TQ = 256


def _tiles(rows, seq):
    return dict(tm=min(rows, 512), tm_post=min(rows, 1024), tb=min(seq, 1024))


def _layer(x, p_all, s0_all, kc, vc, prev, bias, w_al, layer, lw):
    bsz, t, _ = x.shape
    x2d = x.reshape(bsz * t, D_MODEL)
    keep = min(WINDOW, t)
    tiles = _tiles(bsz * t, t)
    tm, tm_post, tb, tq = tiles['tm'], tiles['tm_post'], tiles['tb'], BAND_TQ
    act, lg, ktail, vtail = _in_proj(x2d, w_al, layer, lw, tm, t, keep)
    r3 = lambda a: a.reshape(bsz, -1, a.shape[-1])
    if s0_all is None:
        s0_all, s_layer = jnp.zeros((1, bsz, HA, DK_A, DV_A), F32), 0
    else:
        s_layer = layer
    if kc is None:
        oa, (st, kbuf, vbuf) = _gla(r3(act), r3(lg), s0_all, s_layer, tb, (ktail, vtail), prev)
        ob = _band_prompt(r3(act), bias, layer, tq)
    else:
        oa, (st,) = _gla(r3(act), r3(lg), s0_all, s_layer, tb, (), None if prev is None else prev[:1])
        lc = kc.shape[3]
        ob, kbuf, vbuf = _band_sample(r3(act), kc, vc, layer, r3(ktail), r3(vtail),
                                      bias[layer, -1, :, :t, :lc + t], None if prev is None else prev[1:])
    x_new = _post(oa.reshape(bsz * t, WA_V), ob.reshape(bsz * t, WB), act, x2d,
                  p_all.reshape(p_all.shape[0], bsz * t, P_DIM), layer, lw, tm_post)
    return x_new.reshape(bsz, t, D_MODEL), (st, kbuf, vbuf)


def kernel(x_prompt, x_sample, state_gla, cache_band_k, cache_band_v, p_prompt, p_sample,
           norm_g, w_in, w_gate_up, b_gate, gla_norm_g, q_norm_g, k_norm_g, rel_bias,
           w_branch_a, w_branch_b, w_out, ple_norm_g, w_ple_gate, w_ple):
    depth = w_in.shape[0]
    bias = _band_bias(rel_bias, BAND_TQ)
    w_al = _feature_major_w_in(w_in)
    xp, xs = x_prompt, x_sample

    def feature_major(a):
        return jnp.transpose(a, (0, 1, 3, 4, 2)).reshape(a.shape[:2] + (WB, a.shape[2]))

    def frame_major(a):
        return jnp.transpose(a.reshape(a.shape[:2] + (HB, DH_B, a.shape[3])), (0, 1, 4, 2, 3))

    kc, vc = feature_major(cache_band_k), feature_major(cache_band_v)
    prompt_out = sample_out = None
    for i in range(depth):
        lw = _layer_weights(i, norm_g, w_gate_up, b_gate, gla_norm_g, q_norm_g, k_norm_g,
                            w_branch_a, w_branch_b, w_out, ple_norm_g, w_ple_gate, w_ple)
        xp, prompt_out = _layer(xp, p_prompt, None, None, None, prompt_out, bias, w_al, i, lw)
        xs, sample_out = _layer(xs, p_sample, state_gla, kc, vc, sample_out, bias, w_al, i, lw)
    (sp, kbp, vbp), (ss, kbs, vbs) = prompt_out, sample_out
    return (xp, xs, sp, frame_major(kbp), frame_major(vbp), ss, frame_major(kbs), frame_major(vbs))
```

```python
import functools

import jax
import jax.numpy as jnp
from jax import lax
from jax.experimental import pallas as pl
from jax.experimental.pallas import tpu as pltpu

D_MODEL = 1024
CHUNK = 64
P_DIM = 256
EPS = 1e-6
NEG = -1e30
HA = 4
DK_A = 128
DV_A = 256
GATE_RANK = 16
GATE_TAU = 16.0
WA_K = HA * DK_A
WA_V = HA * DV_A
HB = 16
DH_B = 64
LEFT_CHUNKS = 8
WINDOW = LEFT_CHUNKS * CHUNK
MAX_REL = 128
WB = HB * DH_B

LANES = 128
GATE_PAD = LANES
SUB = 16
VMEM_LIMIT = 60 * 1024 * 1024
LOG2E = 1.4426950408889634

F32 = jnp.float32
BF16 = jnp.bfloat16


def _dot(a, b):
    return jnp.dot(a, b, preferred_element_type=F32)


def _dot_nt(a, b):
    return lax.dot_general(a, b, (((1,), (1,)), ((), ())), preferred_element_type=F32)


def _resident(shape):
    return pl.BlockSpec(shape, lambda *_: (0,) * len(shape), pipeline_mode=pl.Buffered(1))


def _params(sem):
    return pltpu.CompilerParams(dimension_semantics=sem, vmem_limit_bytes=VMEM_LIMIT)


_W_ROWS = {}
_off = 0
for _name, _width in (('qa', WA_K), ('ka', WA_K), ('va', WA_V), ('ra', GATE_RANK), ('ga', WA_V), ('qb', WB),
                      ('kb', WB), ('vb', WB), ('gb', WB), ('mga', D_MODEL), ('mgb', D_MODEL)):
    _W_ROWS[_name] = slice(_off, _off + (GATE_PAD if _name == 'ra' else _width))
    _off += _width
N_IN = _off


_ACT_COLS = {}
_off = 0
for _name, _width in (('qa', WA_K), ('ka', WA_K), ('va', WA_V), ('sga', WA_V), ('qb', WB), ('kb', WB), ('vb', WB),
                      ('sgb', WB), ('sma', D_MODEL), ('smb', D_MODEL)):
    assert _off % _width == 0
    _ACT_COLS[_name] = slice(_off, _off + _width)
    _off += _width
ACT_W = _off


def _act_block(name):
    cols = _ACT_COLS[name]
    return cols.start // (cols.stop - cols.start)


def _in_proj_kernel(x_ref, ng_ref, w_ref, wgu, bg_ref, qg_ref, kg_ref, act_o, lg_o, kt_o, vt_o, *,
                    feature_major_tail):
    x = x_ref[...]
    ms = jnp.mean(x * x, axis=-1, keepdims=True)
    h = (x * lax.rsqrt(ms + EPS) * ng_ref[...]).astype(BF16)
    proj = lambda name: _dot_nt(h, w_ref[_W_ROWS[name], :])

    def put(name, value):
        act_o[:, _ACT_COLS[name]] = value.astype(BF16)

    put('qa', proj('qa') * (DK_A ** -0.5))
    put('ka', proj('ka'))
    put('va', proj('va'))

    ra = proj('ra')
    gl = _dot(ra.astype(BF16), wgu[...]) + bg_ref[...]
    lg_o[...] = (jnp.minimum(gl, 0.0) - jnp.log(1.0 + jnp.exp(-jnp.abs(gl)))) * (1.0 / GATE_TAU)

    ga = proj('ga')
    put('sga', ga * jax.nn.sigmoid(ga))

    low = lax.broadcasted_iota(jnp.int32, (x.shape[0], LANES), 1) < DH_B

    def head_norm(z, g_ref):
        zz = z * z
        scales = []
        for c in range(WB // LANES):
            t = zz[:, c * LANES:(c + 1) * LANES]
            ss_lo = jnp.sum(jnp.where(low, t, 0.0), axis=-1, keepdims=True)
            ss_hi = jnp.sum(jnp.where(low, 0.0, t), axis=-1, keepdims=True)
            scales.append(jnp.where(low, lax.rsqrt(ss_lo * (1.0 / DH_B) + EPS),
                                    lax.rsqrt(ss_hi * (1.0 / DH_B) + EPS)))
        return z * jnp.concatenate(scales, axis=1) * g_ref[...]

    put('qb', head_norm(proj('qb'), qg_ref) * (DH_B ** -0.5 * LOG2E))
    kb = head_norm(proj('kb'), kg_ref)
    put('kb', kb)
    vb = proj('vb')
    put('vb', vb)
    gb = proj('gb')
    put('sgb', gb * jax.nn.sigmoid(gb))
    put('sma', jax.nn.sigmoid(proj('mga')))
    put('smb', jax.nn.sigmoid(proj('mgb')))

    kt_o[...] = kb.T if feature_major_tail else kb
    vt_o[...] = vb.T if feature_major_tail else vb


def _in_proj(x2d, w_al, layer, lw, tm, seq, keep):
    rows = x2d.shape[0]
    row = lambda w: pl.BlockSpec((tm, w), lambda i: (i, 0))
    widths = (ACT_W, WA_K)
    dtypes = (BF16, F32)
    if keep == seq:
        tail = row(WB)
        tail_shape = jax.ShapeDtypeStruct((rows, WB), F32)
    else:
        assert keep % tm == 0 and seq % tm == 0
        per_seq, per_tail = seq // tm, keep // tm
        tail = pl.BlockSpec((None, WB, tm), lambda i: (i // per_seq, 0,
                                                       jnp.maximum(i % per_seq - (per_seq - per_tail), 0)))
        tail_shape = jax.ShapeDtypeStruct((rows // seq, WB, keep), F32)
    wspec = pl.BlockSpec((None,) + w_al.shape[1:], lambda i: (layer, 0, 0), pipeline_mode=pl.Buffered(1))
    small = (lw['wgu'], lw['bg'], lw['qg'], lw['kg'])
    return pl.pallas_call(
        functools.partial(_in_proj_kernel, feature_major_tail=keep != seq),
        grid=(rows // tm,),
        in_specs=[row(D_MODEL), _resident(lw['ng'].shape), wspec] + [_resident(w.shape) for w in small],
        out_specs=[row(w) for w in widths] + [tail, tail],
        out_shape=[jax.ShapeDtypeStruct((rows, w), dt) for w, dt in zip(widths, dtypes)] + [tail_shape] * 2,
        compiler_params=_params(("arbitrary",)),
        name="in_proj",
    )(x2d, lw['ng'], w_al, *small)


def _gla_block(q_ref, k_ref, lg_ref, v_ref, o_ref, states):
    tb = q_ref.shape[0]
    n_chunks = tb // CHUNK
    n_sub = CHUNK // SUB
    shift = CHUNK.bit_length() - 1
    gw = min(tb, 2 * CHUNK)
    r = lax.broadcasted_iota(jnp.int32, (gw, gw), 0)
    cc = lax.broadcasted_iota(jnp.int32, (gw, gw), 1)
    same_chunk = lax.shift_right_logical(r, shift) == lax.shift_right_logical(cc, shift)
    ltri = jnp.where(same_chunk & (cc <= r), 1.0, 0.0).astype(BF16)
    ltri2 = jnp.concatenate([ltri, ltri], axis=1)
    b_groups = []
    for g0 in range(0, tb, gw):
        lg = lg_ref[g0:g0 + gw, :]
        lg_hi = lg.astype(BF16)
        lg_lo = (lg - lg_hi.astype(F32)).astype(BF16)
        b_groups.append(_dot(ltri2, jnp.concatenate([lg_hi, lg_lo], axis=0)))

    ar = lax.broadcasted_iota(jnp.int32, (CHUNK, n_sub * CHUNK), 0)
    ac = lax.broadcasted_iota(jnp.int32, (CHUNK, n_sub * CHUNK), 1)
    sub_shift = SUB.bit_length() - 1
    a_keep = (lax.shift_right_logical(ar, sub_shift) == lax.shift_right_logical(ac, shift)) \
        & ((ac & (CHUNK - 1)) <= ar)
    krow = lax.broadcasted_iota(jnp.int32, (CHUNK, DK_A), 0)

    for ci in range(n_chunks):
        rows = slice(ci * CHUNK, (ci + 1) * CHUNK)
        for hh in range(HA):
            ks = slice(hh * DK_A, (hh + 1) * DK_A)
            vs = slice(hh * DV_A, (hh + 1) * DV_A)
            g_row = ci * CHUNK % gw
            b = b_groups[ci * CHUNK // gw][g_row:g_row + CHUNK, ks]
            q = q_ref[rows, ks].astype(F32)
            k = k_ref[rows, ks].astype(F32)
            v16 = v_ref[rows, vs]
            b_last = b[CHUNK - 1:CHUNK, :]
            o = _dot((q * jnp.exp(b)).astype(BF16), states[hh].astype(BF16))
            refs_b = [b[sb * SUB:sb * SUB + 1, :] for sb in range(n_sub)]
            ref_rows = jnp.concatenate([jnp.broadcast_to(rb, (SUB, DK_A)) for rb in refs_b], axis=0)
            qt = (q * jnp.exp(b - ref_rows)).astype(BF16)
            kt = jnp.concatenate(
                [(k * jnp.exp(jnp.where(krow < (sb + 1) * SUB, refs_b[sb] - b, 0.0))).astype(BF16)
                 for sb in range(n_sub)], axis=0)
            a = jnp.where(a_keep, _dot_nt(qt, kt), 0.0).astype(BF16)
            o_ref[rows, vs] = (o + _dot(a, jnp.concatenate([v16] * n_sub, axis=0))).astype(o_ref.dtype)
            k_dec_t = (k * jnp.exp(b_last - b)).T
            decay = jnp.exp(b.T[:, CHUNK - 1:CHUNK])
            states[hh] = states[hh] * decay + _dot(k_dec_t.astype(BF16), v16)


def _gla_kernel(q_ref, k_ref, lg_ref, v_ref, s0_ref, *rest, n_extra):
    new_refs = rest[:n_extra]
    has_prev = len(rest) == 3 * n_extra + 4
    prev_refs = rest[n_extra:2 * n_extra + 1] if has_prev else ()
    o_ref = rest[-(n_extra + 3)]
    out_refs = rest[-(n_extra + 2):-1]
    st_ref = rest[-1]
    c = pl.program_id(1)

    @pl.when(c == 0)
    def _():
        st_ref[...] = s0_ref[...]

    states = [st_ref[hh] for hh in range(HA)]
    _gla_block(q_ref, k_ref, lg_ref, v_ref, o_ref, states)
    for hh in range(HA):
        st_ref[hh] = states[hh]

    @pl.when(c == pl.num_programs(1) - 1)
    def _():
        last = out_refs[0].shape[0] - 1
        for new, out in zip((st_ref,) + tuple(new_refs), out_refs):
            out[last] = new[...]
        for prev, out in zip(prev_refs, out_refs):
            out[:last] = prev[...]


def _gla(act, lg, s0, s0_layer, tb, extras, prev):
    bsz, t, _ = act.shape
    n_prev = 0 if prev is None else prev[0].shape[0]
    group = lambda name, w: pl.BlockSpec((None, tb, w), lambda b, c: (b, c, _act_block(name)))
    kspec = pl.BlockSpec((None, tb, WA_K), lambda b, c: (b, c, 0))
    vspec = pl.BlockSpec((None, tb, WA_V), lambda b, c: (b, c, 0))
    s0spec = pl.BlockSpec((None, None, HA, DK_A, DV_A), lambda b, c: (s0_layer, b, 0, 0, 0))

    def per_seq(shape, lead):
        zeros = (0,) * len(shape)
        if lead is None:
            return pl.BlockSpec((None,) + shape, lambda b, c: (b,) + zeros)
        return pl.BlockSpec((lead, None) + shape, lambda b, c: (0, b) + zeros)

    stacked_shapes = [(HA, DK_A, DV_A)] + [e.shape[1:] for e in extras]
    prev = () if prev is None else tuple(prev)
    outs = pl.pallas_call(
        functools.partial(_gla_kernel, n_extra=len(extras)),
        grid=(bsz, t // tb),
        in_specs=[group('qa', WA_K), group('ka', WA_K), kspec, group('va', WA_V), s0spec]
        + [per_seq(e.shape[1:], None) for e in extras]
        + [per_seq(s, n_prev) for s in stacked_shapes[:len(prev)]],
        out_specs=[vspec] + [per_seq(s, n_prev + 1) for s in stacked_shapes],
        out_shape=[jax.ShapeDtypeStruct((bsz, t, WA_V), BF16)]
        + [jax.ShapeDtypeStruct((n_prev + 1, bsz) + s, F32) for s in stacked_shapes],
        scratch_shapes=[pltpu.VMEM((HA, DK_A, DV_A), F32)],
        compiler_params=_params(("arbitrary", "arbitrary")),
        name="gla",
    )(act, act, lg, act, s0, *extras, *prev)
    return outs[0], outs[1:]


def _band_prompt_kernel(*refs, n_pieces):
    q_ref = refs[0]
    k_refs = refs[1:1 + n_pieces]
    v_refs = refs[1 + n_pieces:1 + 2 * n_pieces]
    bias_ref, o_ref = refs[1 + 2 * n_pieces], refs[2 + 2 * n_pieces]
    w = k_refs[0].shape[0]
    lane = lax.broadcasted_iota(jnp.int32, (w, LANES), 1)
    lane_lo = jnp.where(lane < DH_B, 1.0, 0.0)
    sel = (lane_lo.astype(BF16), (1.0 - lane_lo).astype(BF16))

    hq = w // 2
    cols = []
    for half in range(2):
        first = half * hq // CHUNK * CHUNK
        last = ((half + 1) * hq - 1) // CHUNK * CHUNK + WINDOW + CHUNK
        cols.append([slice((max(first, p * w) - p * w) // LANES * LANES,
                           -(-(min(last, (p + 1) * w) - p * w) // LANES) * LANES) for p in range(n_pieces)])

    for hp in range(HB // 2):
        ls = slice(hp * LANES, (hp + 1) * LANES)
        q2 = q_ref[:, ls]
        k2 = [kr[:, ls] for kr in k_refs]
        v2 = [vr[:, ls] for vr in v_refs]
        acc = None
        for e in range(2):
            hd = 2 * hp + e
            qm = q2 * sel[e]
            raw = [_dot_nt(qm, k2[p]) for p in range(n_pieces)]
            ex = [[None, None] for _ in range(n_pieces)]
            for half in range(2):
                rs = slice(half * hq, (half + 1) * hq)
                live = [cols[half][p] for p in range(n_pieces)]
                s = [raw[p][rs, live[p]] + bias_ref[hd, rs, p * w + live[p].start:p * w + live[p].stop]
                     for p in range(n_pieces)]
                m = functools.reduce(jnp.maximum, [jnp.max(sp, axis=-1, keepdims=True) for sp in s])
                for p in range(n_pieces):
                    pr = jnp.exp2(s[p] - m).astype(BF16)
                    if live[p].start > 0:
                        pr = jnp.concatenate([jnp.zeros((hq, live[p].start), BF16), pr], axis=1)
                    if live[p].stop < w:
                        pr = jnp.concatenate([pr, jnp.zeros((hq, w - live[p].stop), BF16)], axis=1)
                    ex[p][half] = pr
            for p in range(n_pieces):
                part = _dot(jnp.concatenate(ex[p], axis=0), jnp.concatenate([v2[p] * sel[e], sel[e]], axis=1))
                acc = part if acc is None else acc + part
        o_ref[:, ls] = (acc[:, :LANES] / acc[:, LANES:]).astype(o_ref.dtype)


def _band_sample_kernel(q_ref, kc_ref, kn_ref, vc_ref, vn_ref, kt_ref, vt_ref, bias_ref, *rest):
    o_ref, ko_ref, vo_ref = rest[-3:]
    if len(rest) == 5:
        ko_ref[:-1] = rest[0][...]
        vo_ref[:-1] = rest[1][...]
    s, lc = q_ref.shape[0], kc_ref.shape[1]
    lane = lax.broadcasted_iota(jnp.int32, (s, LANES), 1)
    lane_lo = jnp.where(lane < DH_B, 1.0, 0.0)
    lane_sel = (lane_lo.astype(BF16), (1.0 - lane_lo).astype(BF16))
    row = lax.broadcasted_iota(jnp.int32, (LANES, lc), 0)
    row_lo = jnp.where(row < DH_B, 1.0, 0.0)
    row_sel = (row_lo.astype(BF16), (1.0 - row_lo).astype(BF16))

    for hp in range(HB // 2):
        ls = slice(hp * LANES, (hp + 1) * LANES)
        q2 = q_ref[:, ls]
        kc2 = kc_ref[ls, :].astype(BF16)
        vc2 = vc_ref[ls, :].astype(BF16)
        kn2, vn2 = kn_ref[:, ls], vn_ref[:, ls]
        acc = None
        for e in range(2):
            hd = 2 * hp + e
            qm = q2 * lane_sel[e]
            s_c = _dot(qm, kc2) + bias_ref[hd, :, :lc]
            s_n = _dot_nt(qm, kn2) + bias_ref[hd, :, lc:]
            m = jnp.maximum(jnp.max(s_c, axis=-1, keepdims=True), jnp.max(s_n, axis=-1, keepdims=True))
            ex_c = jnp.exp2(s_c - m).astype(BF16)
            ex_n = jnp.exp2(s_n - m).astype(BF16)
            part = _dot_nt(ex_c, jnp.concatenate([vc2 * row_sel[e], row_sel[e]], axis=0)) \
                + _dot(ex_n, jnp.concatenate([vn2 * lane_sel[e], lane_sel[e]], axis=1))
            acc = part if acc is None else acc + part
        o_ref[:, ls] = (acc[:, :LANES] / acc[:, LANES:]).astype(o_ref.dtype)

    col = lax.broadcasted_iota(jnp.int32, (WB, lc), 1)
    pad_rows = jnp.zeros((LANES - s, WB), F32)
    pad_cols = jnp.zeros((WB, lc - LANES), F32)
    for c_ref, t_ref, out_ref in ((kc_ref, kt_ref, ko_ref), (vc_ref, vt_ref, vo_ref)):
        new_t = jnp.concatenate([t_ref[...], pad_rows], axis=0).T
        new_t = jnp.concatenate([pad_cols, pltpu.roll(new_t, LANES - s, 1)], axis=1)
        out_ref[out_ref.shape[0] - 1] = jnp.where(col >= lc - s, new_t, pltpu.roll(c_ref[...], lc - s, 1))


def _band_prompt(act, bias, layer, tq):
    bsz, t, _ = act.shape
    n_pieces = WINDOW // tq + 1
    ospec = pl.BlockSpec((None, tq, WB), lambda i, b: (b, i, 0))
    qspec = pl.BlockSpec((None, tq, WB), lambda i, b: (b, i, _act_block('qb')))

    def kspec(name, p):
        back = n_pieces - 1 - p
        return pl.BlockSpec((None, tq, WB), lambda i, b: (b, jnp.maximum(i - back, 0), _act_block(name)))

    kspecs = [kspec(name, p) for name in ('kb', 'vb') for p in range(n_pieces)]
    bspec = pl.BlockSpec((None, None) + bias.shape[2:],
                         lambda i, b: (layer, jnp.minimum(i, n_pieces - 1), 0, 0, 0))
    return pl.pallas_call(
        functools.partial(_band_prompt_kernel, n_pieces=n_pieces),
        grid=(t // tq, bsz),
        in_specs=[qspec] + kspecs + [bspec],
        out_specs=ospec,
        out_shape=jax.ShapeDtypeStruct((bsz, t, WB), BF16),
        compiler_params=_params(("arbitrary", "arbitrary")),
        name="band_prompt",
    )(*([act] * (1 + 2 * n_pieces)), bias)


def _band_sample(act, kc_all, vc_all, layer, kt, vt, bias, prev):
    bsz, s, _ = act.shape
    lc = kc_all.shape[3]
    assert s <= LANES <= lc and lc % LANES == 0
    assert (prev is None) == (layer == 0)
    nspec = pl.BlockSpec((None, s, WB), lambda b: (b, 0, 0))
    group = lambda name: pl.BlockSpec((None, s, WB), lambda b: (b, 0, _act_block(name)))
    cspec = pl.BlockSpec((None, None, WB, lc), lambda b: (layer, b, 0, 0))
    stacked = lambda n: pl.BlockSpec((n, None, WB, lc), lambda b: (0, b, 0, 0))
    buf = jax.ShapeDtypeStruct((layer + 1, bsz, WB, lc), F32)
    prev = () if prev is None else tuple(prev)
    return pl.pallas_call(
        _band_sample_kernel,
        grid=(bsz,),
        in_specs=[group('qb'), cspec, group('kb'), cspec, group('vb'), nspec, nspec, _resident(bias.shape)]
        + [stacked(layer)] * len(prev),
        out_specs=[nspec, stacked(layer + 1), stacked(layer + 1)],
        out_shape=[jax.ShapeDtypeStruct((bsz, s, WB), BF16), buf, buf],
        compiler_params=_params(("arbitrary",)),
        name="band_sample",
    )(act, kc_all, act, vc_all, act, kt, vt, bias, *prev)


def _post_kernel(oa_ref, sga_ref, ob_ref, sgb_ref, sma_ref, smb_ref, x_ref, p_ref,
                 gg_ref, wa, wb, wo, pg_ref, wpg, wp, out_ref):
    tm = oa_ref.shape[0]
    halves = (slice(0, tm // 2), slice(tm // 2, tm))

    def gla_normed(rs):
        oa = oa_ref[rs, :].astype(F32)
        segs = []
        for hh in range(HA):
            seg = oa[:, hh * DV_A:(hh + 1) * DV_A]
            ms = jnp.mean(seg * seg, axis=-1, keepdims=True)
            segs.append(seg * lax.rsqrt(ms + EPS) * gg_ref[...])
        return jnp.concatenate(segs, axis=-1)

    ya = [_dot((gla_normed(rs) * sga_ref[rs, :]).astype(BF16), wa[...]) for rs in halves]
    yb = [_dot((ob_ref[rs, :] * sgb_ref[rs, :]).astype(BF16), wb[...]) for rs in halves]
    m = [(sma_ref[rs, :] * a + smb_ref[rs, :] * b).astype(BF16) for rs, a, b in zip(halves, ya, yb)]
    x1 = [x_ref[rs, :] + _dot(mm, wo[...]) for rs, mm in zip(halves, m)]
    hn = [(x * lax.rsqrt(jnp.mean(x * x, axis=-1, keepdims=True) + EPS) * pg_ref[...]).astype(BF16) for x in x1]
    gate = [jax.nn.sigmoid(_dot(h, wpg[...])) for h in hn]
    pe = [_dot(p_ref[rs, :].astype(BF16), wp[...]) for rs in halves]
    for rs, x, g, e in zip(halves, x1, gate, pe):
        out_ref[rs, :] = x + g * e


def _post(oa, ob, act, x2d, p_all, layer, lw, tm):
    rows = x2d.shape[0]
    row = lambda w: pl.BlockSpec((tm, w), lambda i: (i, 0))
    group = lambda name: pl.BlockSpec((tm, D_MODEL), lambda i: (i, _act_block(name)))
    pspec = pl.BlockSpec((None, tm, P_DIM), lambda i: (layer, i, 0))
    weights = (lw['gg'], lw['wa'], lw['wb'], lw['wo'], lw['pg'], lw['wpg'], lw['wp'])
    return pl.pallas_call(
        _post_kernel,
        grid=(rows // tm,),
        in_specs=[row(WA_V), group('sga'), row(WB), group('sgb'), group('sma'), group('smb'), row(D_MODEL), pspec]
        + [_resident(w.shape) for w in weights],
        out_specs=row(D_MODEL),
        out_shape=jax.ShapeDtypeStruct((rows, D_MODEL), F32),
        compiler_params=_params(("parallel",)),
        name="post",
    )(oa, act, ob, act, act, act, x2d, p_all, *weights)


def _feature_major_w_in(w_in):
    assert w_in.shape[1:] == (D_MODEL, N_IN)
    return jnp.transpose(w_in, (0, 2, 1)).astype(BF16)


def _layer_weights(i, norm_g, w_gate_up, b_gate, gla_norm_g, q_norm_g, k_norm_g,
                   w_branch_a, w_branch_b, w_out, ple_norm_g, w_ple_gate, w_ple):
    lw = {}
    lw['wgu'] = jnp.pad(w_gate_up[i].astype(BF16), ((0, GATE_PAD - GATE_RANK), (0, 0)))
    lw['bg'] = b_gate[i].reshape(1, WA_K)
    lw['ng'] = norm_g[i].reshape(1, D_MODEL)
    lw['qg'] = jnp.tile(q_norm_g[i], HB).reshape(1, WB)
    lw['kg'] = jnp.tile(k_norm_g[i], HB).reshape(1, WB)
    lw['gg'] = gla_norm_g[i].reshape(1, DV_A)
    lw['wa'] = w_branch_a[i].astype(BF16)
    lw['wb'] = w_branch_b[i].astype(BF16)
    lw['wo'] = w_out[i].astype(BF16)
    lw['pg'] = ple_norm_g[i].reshape(1, D_MODEL)
    lw['wpg'] = w_ple_gate[i].astype(BF16)
    lw['wp'] = w_ple[i].astype(BF16)
    return lw


def _bias_kernel(vec_ref, o_ref, *, tq, nk):
    n_var = o_ref.shape[0]
    period = vec_ref.shape[-1]
    x = jnp.broadcast_to(vec_ref[...], (tq, period))
    toep = pltpu.roll(x, 0, 1, stride=1, stride_axis=0)[:, :nk]
    r = lax.broadcasted_iota(jnp.int32, (tq, nk), 0)
    c = lax.broadcasted_iota(jnp.int32, (tq, nk), 1)
    shift = CHUNK.bit_length() - 1
    dchunk = lax.shift_right_logical(r + WINDOW, shift) - lax.shift_right_logical(c, shift)
    visible = (dchunk >= 0) & (dchunk <= LEFT_CHUNKS)
    for j in range(n_var):
        o_ref[j] = jnp.where(visible & (c >= (n_var - 1 - j) * tq), toep, NEG)


def _band_bias(rel_bias, tq):
    depth = rel_bias.shape[0]
    nk = WINDOW + tq
    n_var = nk // tq
    period = nk + tq
    tab = rel_bias * LOG2E
    n_hi = WINDOW - MAX_REL
    far = jnp.broadcast_to(tab[..., -1:], (depth, HB, n_hi))
    near = jnp.broadcast_to(tab[..., :1], (depth, HB, max(nk - n_hi - (2 * MAX_REL + 1), 0)))
    wrap = jnp.broadcast_to(tab[..., -1:], (depth, HB, period - nk))
    body = jnp.concatenate([far, tab[..., ::-1], near], axis=-1)[..., :nk]
    vec = jnp.concatenate([body, wrap], axis=-1).reshape(depth, HB, 1, period)
    return pl.pallas_call(
        functools.partial(_bias_kernel, tq=tq, nk=nk),
        grid=(depth, HB),
        in_specs=[pl.BlockSpec((None, None, 1, period), lambda l, h: (l, h, 0, 0))],
        out_specs=pl.BlockSpec((None, n_var, None, tq, nk), lambda l, h: (l, 0, h, 0, 0)),
        out_shape=jax.ShapeDtypeStruct((depth, n_var, HB, tq, nk), F32),
        compiler_params=_params(("parallel", "parallel")),
        name="band_bias",
    )(vec)


BAND_TQ = 256


def _tiles(rows, seq):
    return dict(tm=min(rows, 512), tm_post=min(rows, 1024), tb=min(seq, 1024))


def _layer(x, p_all, s0_all, kc, vc, prev, bias, w_al, layer, lw):
    bsz, t, _ = x.shape
    x2d = x.reshape(bsz * t, D_MODEL)
    keep = min(WINDOW, t)
    tiles = _tiles(bsz * t, t)
    tm, tm_post, tb, tq = tiles['tm'], tiles['tm_post'], tiles['tb'], BAND_TQ
    act, lg, ktail, vtail = _in_proj(x2d, w_al, layer, lw, tm, t, keep)
    r3 = lambda a: a.reshape(bsz, -1, a.shape[-1])
    if s0_all is None:
        s0_all, s_layer = jnp.zeros((1, bsz, HA, DK_A, DV_A), F32), 0
    else:
        s_layer = layer
    if kc is None:
        oa, (st, kbuf, vbuf) = _gla(r3(act), r3(lg), s0_all, s_layer, tb, (ktail, vtail), prev)
        ob = _band_prompt(r3(act), bias, layer, tq)
    else:
        oa, (st,) = _gla(r3(act), r3(lg), s0_all, s_layer, tb, (), None if prev is None else prev[:1])
        lc = kc.shape[3]
        ob, kbuf, vbuf = _band_sample(r3(act), kc, vc, layer, r3(ktail), r3(vtail),
                                      bias[layer, -1, :, :t, :lc + t], None if prev is None else prev[1:])
    x_new = _post(oa.reshape(bsz * t, WA_V), ob.reshape(bsz * t, WB), act, x2d,
                  p_all.reshape(p_all.shape[0], bsz * t, P_DIM), layer, lw, tm_post)
    return x_new.reshape(bsz, t, D_MODEL), (st, kbuf, vbuf)


def kernel(x_prompt, x_sample, state_gla, cache_band_k, cache_band_v, p_prompt, p_sample,
           norm_g, w_in, w_gate_up, b_gate, gla_norm_g, q_norm_g, k_norm_g, rel_bias,
           w_branch_a, w_branch_b, w_out, ple_norm_g, w_ple_gate, w_ple):
    depth = w_in.shape[0]
    bias = _band_bias(rel_bias, BAND_TQ)
    w_al = _feature_major_w_in(w_in)
    xp, xs = x_prompt, x_sample

    def feature_major(a):
        return jnp.transpose(a, (0, 1, 3, 4, 2)).reshape(a.shape[:2] + (WB, a.shape[2]))

    def frame_major(a):
        return jnp.transpose(a.reshape(a.shape[:2] + (HB, DH_B, a.shape[3])), (0, 1, 4, 2, 3))

    kc, vc = feature_major(cache_band_k), feature_major(cache_band_v)
    prompt_out = sample_out = None
    for i in range(depth):
        lw = _layer_weights(i, norm_g, w_gate_up, b_gate, gla_norm_g, q_norm_g, k_norm_g,
                            w_branch_a, w_branch_b, w_out, ple_norm_g, w_ple_gate, w_ple)
        xp, prompt_out = _layer(xp, p_prompt, None, None, None, prompt_out, bias, w_al, i, lw)
        xs, sample_out = _layer(xs, p_sample, state_gla, kc, vc, sample_out, bias, w_al, i, lw)
    (sp, kbp, vbp), (ss, kbs, vbs) = prompt_out, sample_out
    return (xp, xs, sp, frame_major(kbp), frame_major(vbp), ss, frame_major(kbs), frame_major(vbs))
```

```python
import functools

import jax
import jax.numpy as jnp
from jax import lax
from jax.experimental import pallas as pl
from jax.experimental.pallas import tpu as pltpu

D_MODEL = 1024
CHUNK = 64
P_DIM = 256
EPS = 1e-6
NEG = -1e30
HA = 4
DK_A = 128
DV_A = 256
GATE_RANK = 16
GATE_TAU = 16.0
WA_K = HA * DK_A
WA_V = HA * DV_A
HB = 16
DH_B = 64
LEFT_CHUNKS = 8
WINDOW = LEFT_CHUNKS * CHUNK
MAX_REL = 128
WB = HB * DH_B

LANES = 128
GATE_PAD = LANES
SUB = 16
VMEM_LIMIT = 60 * 1024 * 1024
LOG2E = 1.4426950408889634

F32 = jnp.float32
BF16 = jnp.bfloat16


def _dot(a, b):
    return jnp.dot(a, b, preferred_element_type=F32)


def _dot_nt(a, b):
    return lax.dot_general(a, b, (((1,), (1,)), ((), ())), preferred_element_type=F32)


def _resident(shape):
    return pl.BlockSpec(shape, lambda *_: (0,) * len(shape), pipeline_mode=pl.Buffered(1))


def _params(sem):
    return pltpu.CompilerParams(dimension_semantics=sem, vmem_limit_bytes=VMEM_LIMIT)


_W_ROWS = {}
_off = 0
for _name, _width in (('qa', WA_K), ('ka', WA_K), ('va', WA_V), ('ra', GATE_RANK), ('ga', WA_V), ('qb', WB),
                      ('kb', WB), ('vb', WB), ('gb', WB), ('mga', D_MODEL), ('mgb', D_MODEL)):
    _W_ROWS[_name] = slice(_off, _off + (GATE_PAD if _name == 'ra' else _width))
    _off += _width
N_IN = _off


_ACT_COLS = {}
_off = 0
for _name, _width in (('qa', WA_K), ('ka', WA_K), ('va', WA_V), ('sga', WA_V), ('qb', WB), ('kb', WB), ('vb', WB),
                      ('sgb', WB), ('sma', D_MODEL), ('smb', D_MODEL)):
    assert _off % _width == 0
    _ACT_COLS[_name] = slice(_off, _off + _width)
    _off += _width
ACT_W = _off


def _act_block(name):
    cols = _ACT_COLS[name]
    return cols.start // (cols.stop - cols.start)


def _in_proj_kernel(x_ref, ng_ref, w_ref, wgu, bg_ref, qg_ref, kg_ref, act_o, lg_o, kt_o, vt_o, *,
                    feature_major_tail):
    x = x_ref[...]
    ms = jnp.mean(x * x, axis=-1, keepdims=True)
    h = (x * lax.rsqrt(ms + EPS) * ng_ref[...]).astype(BF16)
    proj = lambda name: _dot_nt(h, w_ref[_W_ROWS[name], :])

    def put(name, value):
        act_o[:, _ACT_COLS[name]] = value.astype(BF16)

    put('qa', proj('qa') * (DK_A ** -0.5))
    put('ka', proj('ka'))
    put('va', proj('va'))

    ra = proj('ra')
    gl = _dot(ra.astype(BF16), wgu[...]) + bg_ref[...]
    lg_o[...] = (jnp.minimum(gl, 0.0) - jnp.log(1.0 + jnp.exp(-jnp.abs(gl)))) * (1.0 / GATE_TAU)

    ga = proj('ga')
    put('sga', ga * jax.nn.sigmoid(ga))

    low = lax.broadcasted_iota(jnp.int32, (x.shape[0], LANES), 1) < DH_B

    def head_norm(z, g_ref):
        zz = z * z
        scales = []
        for c in range(WB // LANES):
            t = zz[:, c * LANES:(c + 1) * LANES]
            ss_lo = jnp.sum(jnp.where(low, t, 0.0), axis=-1, keepdims=True)
            ss_hi = jnp.sum(jnp.where(low, 0.0, t), axis=-1, keepdims=True)
            scales.append(jnp.where(low, lax.rsqrt(ss_lo * (1.0 / DH_B) + EPS),
                                    lax.rsqrt(ss_hi * (1.0 / DH_B) + EPS)))
        return z * jnp.concatenate(scales, axis=1) * g_ref[...]

    put('qb', head_norm(proj('qb'), qg_ref) * (DH_B ** -0.5 * LOG2E))
    kb = head_norm(proj('kb'), kg_ref)
    put('kb', kb)
    vb = proj('vb')
    put('vb', vb)
    gb = proj('gb')
    put('sgb', gb * jax.nn.sigmoid(gb))
    put('sma', jax.nn.sigmoid(proj('mga')))
    put('smb', jax.nn.sigmoid(proj('mgb')))

    kt_o[...] = kb.T if feature_major_tail else kb
    vt_o[...] = vb.T if feature_major_tail else vb


def _in_proj(x2d, w_al, layer, lw, tm, seq, keep):
    rows = x2d.shape[0]
    row = lambda w: pl.BlockSpec((tm, w), lambda i: (i, 0))
    widths = (ACT_W, WA_K)
    dtypes = (BF16, F32)
    if keep == seq:
        tail = row(WB)
        tail_shape = jax.ShapeDtypeStruct((rows, WB), F32)
    else:
        assert keep % tm == 0 and seq % tm == 0
        per_seq, per_tail = seq // tm, keep // tm
        tail = pl.BlockSpec((None, WB, tm), lambda i: (i // per_seq, 0,
                                                       jnp.maximum(i % per_seq - (per_seq - per_tail), 0)))
        tail_shape = jax.ShapeDtypeStruct((rows // seq, WB, keep), F32)
    wspec = pl.BlockSpec((None,) + w_al.shape[1:], lambda i: (layer, 0, 0), pipeline_mode=pl.Buffered(1))
    small = (lw['wgu'], lw['bg'], lw['qg'], lw['kg'])
    return pl.pallas_call(
        functools.partial(_in_proj_kernel, feature_major_tail=keep != seq),
        grid=(rows // tm,),
        in_specs=[row(D_MODEL), _resident(lw['ng'].shape), wspec] + [_resident(w.shape) for w in small],
        out_specs=[row(w) for w in widths] + [tail, tail],
        out_shape=[jax.ShapeDtypeStruct((rows, w), dt) for w, dt in zip(widths, dtypes)] + [tail_shape] * 2,
        compiler_params=_params(("arbitrary",)),
        name="in_proj",
    )(x2d, lw['ng'], w_al, *small)


def _gla_block(q_ref, k_ref, lg_ref, v_ref, o_ref, states):
    tb = q_ref.shape[0]
    n_chunks = tb // CHUNK
    n_sub = CHUNK // SUB
    shift = CHUNK.bit_length() - 1
    gw = min(tb, 2 * CHUNK)
    r = lax.broadcasted_iota(jnp.int32, (gw, gw), 0)
    cc = lax.broadcasted_iota(jnp.int32, (gw, gw), 1)
    same_chunk = lax.shift_right_logical(r, shift) == lax.shift_right_logical(cc, shift)
    ltri = jnp.where(same_chunk & (cc <= r), 1.0, 0.0).astype(BF16)
    ltri2 = jnp.concatenate([ltri, ltri], axis=1)
    b_groups = []
    for g0 in range(0, tb, gw):
        lg = lg_ref[g0:g0 + gw, :]
        lg_hi = lg.astype(BF16)
        lg_lo = (lg - lg_hi.astype(F32)).astype(BF16)
        b_groups.append(_dot(ltri2, jnp.concatenate([lg_hi, lg_lo], axis=0)))

    ar = lax.broadcasted_iota(jnp.int32, (CHUNK, n_sub * CHUNK), 0)
    ac = lax.broadcasted_iota(jnp.int32, (CHUNK, n_sub * CHUNK), 1)
    sub_shift = SUB.bit_length() - 1
    a_keep = (lax.shift_right_logical(ar, sub_shift) == lax.shift_right_logical(ac, shift)) \
        & ((ac & (CHUNK - 1)) <= ar)
    krow = lax.broadcasted_iota(jnp.int32, (CHUNK, DK_A), 0)

    for ci in range(n_chunks):
        rows = slice(ci * CHUNK, (ci + 1) * CHUNK)
        for hh in range(HA):
            ks = slice(hh * DK_A, (hh + 1) * DK_A)
            vs = slice(hh * DV_A, (hh + 1) * DV_A)
            g_row = ci * CHUNK % gw
            b = b_groups[ci * CHUNK // gw][g_row:g_row + CHUNK, ks]
            q = q_ref[rows, ks].astype(F32)
            k = k_ref[rows, ks].astype(F32)
            v16 = v_ref[rows, vs]
            b_last = b[CHUNK - 1:CHUNK, :]
            o = _dot((q * jnp.exp(b)).astype(BF16), states[hh].astype(BF16))
            refs_b = [b[sb * SUB:sb * SUB + 1, :] for sb in range(n_sub)]
            ref_rows = jnp.concatenate([jnp.broadcast_to(rb, (SUB, DK_A)) for rb in refs_b], axis=0)
            qt = (q * jnp.exp(b - ref_rows)).astype(BF16)
            kt = jnp.concatenate(
                [(k * jnp.exp(jnp.where(krow < (sb + 1) * SUB, refs_b[sb] - b, 0.0))).astype(BF16)
                 for sb in range(n_sub)], axis=0)
            a = jnp.where(a_keep, _dot_nt(qt, kt), 0.0).astype(BF16)
            o_ref[rows, vs] = (o + _dot(a, jnp.concatenate([v16] * n_sub, axis=0))).astype(o_ref.dtype)
            k_dec_t = (k * jnp.exp(b_last - b)).T
            decay = jnp.exp(b.T[:, CHUNK - 1:CHUNK])
            states[hh] = states[hh] * decay + _dot(k_dec_t.astype(BF16), v16)


def _gla_kernel(q_ref, k_ref, lg_ref, v_ref, s0_ref, *rest, n_extra):
    new_refs = rest[:n_extra]
    has_prev = len(rest) == 3 * n_extra + 4
    prev_refs = rest[n_extra:2 * n_extra + 1] if has_prev else ()
    o_ref = rest[-(n_extra + 3)]
    out_refs = rest[-(n_extra + 2):-1]
    st_ref = rest[-1]
    c = pl.program_id(1)

    @pl.when(c == 0)
    def _():
        st_ref[...] = s0_ref[...]

    states = [st_ref[hh] for hh in range(HA)]
    _gla_block(q_ref, k_ref, lg_ref, v_ref, o_ref, states)
    for hh in range(HA):
        st_ref[hh] = states[hh]

    @pl.when(c == pl.num_programs(1) - 1)
    def _():
        last = out_refs[0].shape[0] - 1
        for new, out in zip((st_ref,) + tuple(new_refs), out_refs):
            out[last] = new[...]
        for prev, out in zip(prev_refs, out_refs):
            out[:last] = prev[...]


def _gla(act, lg, s0, s0_layer, tb, extras, prev):
    bsz, t, _ = act.shape
    n_prev = 0 if prev is None else prev[0].shape[0]
    group = lambda name, w: pl.BlockSpec((None, tb, w), lambda b, c: (b, c, _act_block(name)))
    kspec = pl.BlockSpec((None, tb, WA_K), lambda b, c: (b, c, 0))
    vspec = pl.BlockSpec((None, tb, WA_V), lambda b, c: (b, c, 0))
    s0spec = pl.BlockSpec((None, None, HA, DK_A, DV_A), lambda b, c: (s0_layer, b, 0, 0, 0))

    def per_seq(shape, lead):
        zeros = (0,) * len(shape)
        if lead is None:
            return pl.BlockSpec((None,) + shape, lambda b, c: (b,) + zeros)
        return pl.BlockSpec((lead, None) + shape, lambda b, c: (0, b) + zeros)

    stacked_shapes = [(HA, DK_A, DV_A)] + [e.shape[1:] for e in extras]
    prev = () if prev is None else tuple(prev)
    outs = pl.pallas_call(
        functools.partial(_gla_kernel, n_extra=len(extras)),
        grid=(bsz, t // tb),
        in_specs=[group('qa', WA_K), group('ka', WA_K), kspec, group('va', WA_V), s0spec]
        + [per_seq(e.shape[1:], None) for e in extras]
        + [per_seq(s, n_prev) for s in stacked_shapes[:len(prev)]],
        out_specs=[vspec] + [per_seq(s, n_prev + 1) for s in stacked_shapes],
        out_shape=[jax.ShapeDtypeStruct((bsz, t, WA_V), BF16)]
        + [jax.ShapeDtypeStruct((n_prev + 1, bsz) + s, F32) for s in stacked_shapes],
        scratch_shapes=[pltpu.VMEM((HA, DK_A, DV_A), F32)],
        compiler_params=_params(("arbitrary", "arbitrary")),
        name="gla",
    )(act, act, lg, act, s0, *extras, *prev)
    return outs[0], outs[1:]


def _band_prompt_kernel(*refs, n_pieces):
    q_ref = refs[0]
    k_refs = refs[1:1 + n_pieces]
    v_refs = refs[1 + n_pieces:1 + 2 * n_pieces]
    bias_refs = refs[1 + 2 * n_pieces:1 + 3 * n_pieces]
    o_ref = refs[1 + 3 * n_pieces]
    w = k_refs[0].shape[0]
    lane = lax.broadcasted_iota(jnp.int32, (w, LANES), 1)
    lane_lo = jnp.where(lane < DH_B, 1.0, 0.0)
    sel = (lane_lo.astype(BF16), (1.0 - lane_lo).astype(BF16))

    hq = w // 2
    cols = []
    for half in range(2):
        first = half * hq // CHUNK * CHUNK
        last = ((half + 1) * hq - 1) // CHUNK * CHUNK + WINDOW + CHUNK
        cols.append([slice((max(first, p * w) - p * w) // LANES * LANES,
                           -(-(min(last, (p + 1) * w) - p * w) // LANES) * LANES) for p in range(n_pieces)])

    for hp in range(HB // 2):
        ls = slice(hp * LANES, (hp + 1) * LANES)
        q2 = q_ref[:, ls]
        k2 = [kr[:, ls] for kr in k_refs]
        v2 = [vr[:, ls] for vr in v_refs]
        acc = None
        for e in range(2):
            hd = 2 * hp + e
            qm = q2 * sel[e]
            raw = [_dot_nt(qm, k2[p]) for p in range(n_pieces)]
            ex = [[None, None] for _ in range(n_pieces)]
            for half in range(2):
                rs = slice(half * hq, (half + 1) * hq)
                live = [cols[half][p] for p in range(n_pieces)]
                s = [raw[p][rs, live[p]] + bias_refs[p][hd, rs, live[p]] for p in range(n_pieces)]
                m = functools.reduce(jnp.maximum, [jnp.max(sp, axis=-1, keepdims=True) for sp in s])
                for p in range(n_pieces):
                    pr = jnp.exp2(s[p] - m).astype(BF16)
                    if live[p].start > 0:
                        pr = jnp.concatenate([jnp.zeros((hq, live[p].start), BF16), pr], axis=1)
                    if live[p].stop < w:
                        pr = jnp.concatenate([pr, jnp.zeros((hq, w - live[p].stop), BF16)], axis=1)
                    ex[p][half] = pr
            for p in range(n_pieces):
                part = _dot(jnp.concatenate(ex[p], axis=0), jnp.concatenate([v2[p] * sel[e], sel[e]], axis=1))
                acc = part if acc is None else acc + part
        o_ref[:, ls] = (acc[:, :LANES] / acc[:, LANES:]).astype(o_ref.dtype)


CACHE_SLOTS = 3


def _band_sample_kernel(q_ref, kc_hbm, kn_ref, vc_hbm, vn_ref, kt_ref, vt_ref, bias_ref, *rest, layer, n_seq):
    kc_slots, vc_slots, sem = rest[-3:]
    rest = rest[:-3]
    b = pl.program_id(0)

    def fetch(seq, slot):
        return [pltpu.make_async_copy(src.at[layer, seq], dst.at[slot], sem.at[j, slot])
                for j, (src, dst) in enumerate(((kc_hbm, kc_slots), (vc_hbm, vc_slots)))]

    @pl.when(b == 0)
    def _():
        for seq in range(min(CACHE_SLOTS - 1, n_seq)):
            for cp in fetch(seq, seq):
                cp.start()

    @pl.when(b + CACHE_SLOTS - 1 < n_seq)
    def _():
        for cp in fetch(b + CACHE_SLOTS - 1, (b + CACHE_SLOTS - 1) % CACHE_SLOTS):
            cp.start()

    slot = b % CACHE_SLOTS
    for cp in fetch(b, slot):
        cp.wait()
    kc_ref, vc_ref = kc_slots.at[slot], vc_slots.at[slot]

    o_ref, ko_ref, vo_ref = rest[-3:]
    if len(rest) == 5:
        ko_ref[:-1] = rest[0][...]
        vo_ref[:-1] = rest[1][...]
    s, lc = q_ref.shape[0], kc_ref.shape[1]
    lane = lax.broadcasted_iota(jnp.int32, (s, LANES), 1)
    lane_lo = jnp.where(lane < DH_B, 1.0, 0.0)
    lane_sel = (lane_lo.astype(BF16), (1.0 - lane_lo).astype(BF16))
    row = lax.broadcasted_iota(jnp.int32, (LANES, lc), 0)
    row_lo = jnp.where(row < DH_B, 1.0, 0.0)
    row_sel = (row_lo.astype(BF16), (1.0 - row_lo).astype(BF16))

    for hp in range(HB // 2):
        ls = slice(hp * LANES, (hp + 1) * LANES)
        q2 = q_ref[:, ls]
        kc2 = kc_ref[ls, :].astype(BF16)
        vc2 = vc_ref[ls, :].astype(BF16)
        kn2, vn2 = kn_ref[:, ls], vn_ref[:, ls]
        acc = None
        for e in range(2):
            hd = 2 * hp + e
            qm = q2 * lane_sel[e]
            s_c = _dot(qm, kc2) + bias_ref[hd, :, :lc]
            s_n = _dot_nt(qm, kn2) + bias_ref[hd, :, lc:]
            m = jnp.maximum(jnp.max(s_c, axis=-1, keepdims=True), jnp.max(s_n, axis=-1, keepdims=True))
            ex_c = jnp.exp2(s_c - m).astype(BF16)
            ex_n = jnp.exp2(s_n - m).astype(BF16)
            part = _dot_nt(ex_c, jnp.concatenate([vc2 * row_sel[e], row_sel[e]], axis=0)) \
                + _dot(ex_n, jnp.concatenate([vn2 * lane_sel[e], lane_sel[e]], axis=1))
            acc = part if acc is None else acc + part
        o_ref[:, ls] = (acc[:, :LANES] / acc[:, LANES:]).astype(o_ref.dtype)

    col = lax.broadcasted_iota(jnp.int32, (WB, lc), 1)
    pad_rows = jnp.zeros((LANES - s, WB), F32)
    pad_cols = jnp.zeros((WB, lc - LANES), F32)
    for c_ref, t_ref, out_ref in ((kc_ref, kt_ref, ko_ref), (vc_ref, vt_ref, vo_ref)):
        new_t = jnp.concatenate([t_ref[...], pad_rows], axis=0).T
        new_t = jnp.concatenate([pad_cols, pltpu.roll(new_t, LANES - s, 1)], axis=1)
        out_ref[out_ref.shape[0] - 1] = jnp.where(col >= lc - s, new_t, pltpu.roll(c_ref[...], lc - s, 1))


def _band_prompt(act, bias, layer, tq):
    bsz, t, _ = act.shape
    n_pieces = WINDOW // tq + 1
    ospec = pl.BlockSpec((None, tq, WB), lambda i, b: (b, i, 0))
    qspec = pl.BlockSpec((None, tq, WB), lambda i, b: (b, i, _act_block('qb')))

    def kspec(name, p):
        back = n_pieces - 1 - p
        return pl.BlockSpec((None, tq, WB), lambda i, b: (b, jnp.maximum(i - back, 0), _act_block(name)))

    def bspec(p):
        back = n_pieces - 1 - p
        return pl.BlockSpec((None, None) + bias.shape[2:],
                            lambda i, b: (layer, jnp.where(i >= back, p, n_pieces), 0, 0, 0))

    kspecs = [kspec(name, p) for name in ('kb', 'vb') for p in range(n_pieces)]
    return pl.pallas_call(
        functools.partial(_band_prompt_kernel, n_pieces=n_pieces),
        grid=(t // tq, bsz),
        in_specs=[qspec] + kspecs + [bspec(p) for p in range(n_pieces)],
        out_specs=ospec,
        out_shape=jax.ShapeDtypeStruct((bsz, t, WB), BF16),
        compiler_params=_params(("arbitrary", "arbitrary")),
        name="band_prompt",
    )(*([act] * (1 + 2 * n_pieces)), *([bias] * n_pieces))


def _band_sample(act, kc_all, vc_all, layer, kt, vt, bias, prev):
    bsz, s, _ = act.shape
    lc = kc_all.shape[3]
    assert s <= LANES <= lc and lc % LANES == 0
    assert bias.shape == (HB, s, lc + s)
    assert (prev is None) == (layer == 0)
    nspec = pl.BlockSpec((None, s, WB), lambda b: (b, 0, 0))
    group = lambda name: pl.BlockSpec((None, s, WB), lambda b: (b, 0, _act_block(name)))
    cspec = pl.BlockSpec(memory_space=pl.ANY)
    stacked = lambda n: pl.BlockSpec((n, None, WB, lc), lambda b: (0, b, 0, 0))
    buf = jax.ShapeDtypeStruct((layer + 1, bsz, WB, lc), F32)
    prev = () if prev is None else tuple(prev)
    return pl.pallas_call(
        functools.partial(_band_sample_kernel, layer=layer, n_seq=bsz),
        grid=(bsz,),
        in_specs=[group('qb'), cspec, group('kb'), cspec, group('vb'), nspec, nspec, _resident(bias.shape)]
        + [stacked(layer)] * len(prev),
        out_specs=[nspec, stacked(layer + 1), stacked(layer + 1)],
        out_shape=[jax.ShapeDtypeStruct((bsz, s, WB), BF16), buf, buf],
        scratch_shapes=[pltpu.VMEM((CACHE_SLOTS, WB, lc), F32), pltpu.VMEM((CACHE_SLOTS, WB, lc), F32),
                        pltpu.SemaphoreType.DMA((2, CACHE_SLOTS))],
        compiler_params=_params(("arbitrary",)),
        name="band_sample",
    )(act, kc_all, act, vc_all, act, kt, vt, bias, *prev)


def _post_kernel(oa_ref, sga_ref, ob_ref, sgb_ref, sma_ref, smb_ref, x_ref, p_ref,
                 gg_ref, wa, wb, wo, pg_ref, wpg, wp, out_ref):
    tm = oa_ref.shape[0]
    halves = (slice(0, tm // 2), slice(tm // 2, tm))

    def gla_normed(rs):
        oa = oa_ref[rs, :].astype(F32)
        segs = []
        for hh in range(HA):
            seg = oa[:, hh * DV_A:(hh + 1) * DV_A]
            ms = jnp.mean(seg * seg, axis=-1, keepdims=True)
            segs.append(seg * lax.rsqrt(ms + EPS) * gg_ref[...])
        return jnp.concatenate(segs, axis=-1)

    ya = [_dot((gla_normed(rs) * sga_ref[rs, :]).astype(BF16), wa[...]) for rs in halves]
    yb = [_dot((ob_ref[rs, :] * sgb_ref[rs, :]).astype(BF16), wb[...]) for rs in halves]
    m = [(sma_ref[rs, :] * a + smb_ref[rs, :] * b).astype(BF16) for rs, a, b in zip(halves, ya, yb)]
    x1 = [x_ref[rs, :] + _dot(mm, wo[...]) for rs, mm in zip(halves, m)]
    hn = [(x * lax.rsqrt(jnp.mean(x * x, axis=-1, keepdims=True) + EPS) * pg_ref[...]).astype(BF16) for x in x1]
    gate = [jax.nn.sigmoid(_dot(h, wpg[...])) for h in hn]
    pe = [_dot(p_ref[rs, :].astype(BF16), wp[...]) for rs in halves]
    for rs, x, g, e in zip(halves, x1, gate, pe):
        out_ref[rs, :] = x + g * e


def _post(oa, ob, act, x2d, p_all, layer, lw, tm):
    rows = x2d.shape[0]
    row = lambda w: pl.BlockSpec((tm, w), lambda i: (i, 0))
    group = lambda name: pl.BlockSpec((tm, D_MODEL), lambda i: (i, _act_block(name)))
    pspec = pl.BlockSpec((None, tm, P_DIM), lambda i: (layer, i, 0))
    weights = (lw['gg'], lw['wa'], lw['wb'], lw['wo'], lw['pg'], lw['wpg'], lw['wp'])
    return pl.pallas_call(
        _post_kernel,
        grid=(rows // tm,),
        in_specs=[row(WA_V), group('sga'), row(WB), group('sgb'), group('sma'), group('smb'), row(D_MODEL), pspec]
        + [_resident(w.shape) for w in weights],
        out_specs=row(D_MODEL),
        out_shape=jax.ShapeDtypeStruct((rows, D_MODEL), F32),
        compiler_params=_params(("parallel",)),
        name="post",
    )(oa, act, ob, act, act, act, x2d, p_all, *weights)


def _feature_major_w_in(w_in):
    assert w_in.shape[1:] == (D_MODEL, N_IN)
    return jnp.transpose(w_in, (0, 2, 1)).astype(BF16)


def _layer_weights(i, norm_g, w_gate_up, b_gate, gla_norm_g, q_norm_g, k_norm_g,
                   w_branch_a, w_branch_b, w_out, ple_norm_g, w_ple_gate, w_ple):
    lw = {}
    lw['wgu'] = jnp.pad(w_gate_up[i].astype(BF16), ((0, GATE_PAD - GATE_RANK), (0, 0)))
    lw['bg'] = b_gate[i].reshape(1, WA_K)
    lw['ng'] = norm_g[i].reshape(1, D_MODEL)
    lw['qg'] = jnp.tile(q_norm_g[i], HB).reshape(1, WB)
    lw['kg'] = jnp.tile(k_norm_g[i], HB).reshape(1, WB)
    lw['gg'] = gla_norm_g[i].reshape(1, DV_A)
    lw['wa'] = w_branch_a[i].astype(BF16)
    lw['wb'] = w_branch_b[i].astype(BF16)
    lw['wo'] = w_out[i].astype(BF16)
    lw['pg'] = ple_norm_g[i].reshape(1, D_MODEL)
    lw['wpg'] = w_ple_gate[i].astype(BF16)
    lw['wp'] = w_ple[i].astype(BF16)
    return lw


def _bias_kernel(vec_ref, o_ref, *, tq, nk):
    n_pieces = o_ref.shape[0] - 1
    period = vec_ref.shape[-1]
    x = jnp.broadcast_to(vec_ref[...], (tq, period))
    toep = pltpu.roll(x, 0, 1, stride=1, stride_axis=0)[:, :nk]
    r = lax.broadcasted_iota(jnp.int32, (tq, nk), 0)
    c = lax.broadcasted_iota(jnp.int32, (tq, nk), 1)
    shift = CHUNK.bit_length() - 1
    dchunk = lax.shift_right_logical(r + WINDOW, shift) - lax.shift_right_logical(c, shift)
    table = jnp.where((dchunk >= 0) & (dchunk <= LEFT_CHUNKS), toep, NEG)
    for p in range(n_pieces):
        o_ref[p] = table[:, p * tq:(p + 1) * tq]
    o_ref[n_pieces] = jnp.full((tq, tq), NEG, F32)


def _band_bias(rel_bias, tq):
    depth = rel_bias.shape[0]
    nk = WINDOW + tq
    assert nk % tq == 0
    n_slots = nk // tq + 1
    period = nk + tq
    tab = rel_bias * LOG2E
    n_hi = WINDOW - MAX_REL
    far = jnp.broadcast_to(tab[..., -1:], (depth, HB, n_hi))
    near = jnp.broadcast_to(tab[..., :1], (depth, HB, max(nk - n_hi - (2 * MAX_REL + 1), 0)))
    wrap = jnp.broadcast_to(tab[..., -1:], (depth, HB, period - nk))
    body = jnp.concatenate([far, tab[..., ::-1], near], axis=-1)[..., :nk]
    vec = jnp.concatenate([body, wrap], axis=-1).reshape(depth, HB, 1, period)
    return pl.pallas_call(
        functools.partial(_bias_kernel, tq=tq, nk=nk),
        grid=(depth, HB),
        in_specs=[pl.BlockSpec((None, None, 1, period), lambda l, h: (l, h, 0, 0))],
        out_specs=pl.BlockSpec((None, n_slots, None, tq, tq), lambda l, h: (l, 0, h, 0, 0)),
        out_shape=jax.ShapeDtypeStruct((depth, n_slots, HB, tq, tq), F32),
        compiler_params=_params(("parallel", "parallel")),
        name="band_bias",
    )(vec)


BAND_TQ = 256


def _tiles(rows, seq):
    return dict(tm=min(rows, 512), tm_post=min(rows, 1024), tb=min(seq, 1024))


def _layer(x, p_all, s0_all, kc, vc, prev, bias, w_al, layer, lw):
    bsz, t, _ = x.shape
    x2d = x.reshape(bsz * t, D_MODEL)
    keep = min(WINDOW, t)
    tiles = _tiles(bsz * t, t)
    tm, tm_post, tb, tq = tiles['tm'], tiles['tm_post'], tiles['tb'], BAND_TQ
    act, lg, ktail, vtail = _in_proj(x2d, w_al, layer, lw, tm, t, keep)
    r3 = lambda a: a.reshape(bsz, -1, a.shape[-1])
    if s0_all is None:
        s0_all, s_layer = jnp.zeros((1, bsz, HA, DK_A, DV_A), F32), 0
    else:
        s_layer = layer
    if kc is None:
        oa, (st, kbuf, vbuf) = _gla(r3(act), r3(lg), s0_all, s_layer, tb, (ktail, vtail), prev)
        ob = _band_prompt(r3(act), bias, layer, tq)
    else:
        oa, (st,) = _gla(r3(act), r3(lg), s0_all, s_layer, tb, (), None if prev is None else prev[:1])
        lc = kc.shape[3]
        assert lc == WINDOW and t <= CHUNK
        table = jnp.concatenate([bias[layer, p, :, :t, :] for p in range(bias.shape[1] - 1)], axis=-1)
        ob, kbuf, vbuf = _band_sample(r3(act), kc, vc, layer, r3(ktail), r3(vtail), table[:, :, :lc + t],
                                      None if prev is None else prev[1:])
    x_new = _post(oa.reshape(bsz * t, WA_V), ob.reshape(bsz * t, WB), act, x2d,
                  p_all.reshape(p_all.shape[0], bsz * t, P_DIM), layer, lw, tm_post)
    return x_new.reshape(bsz, t, D_MODEL), (st, kbuf, vbuf)


def kernel(x_prompt, x_sample, state_gla, cache_band_k, cache_band_v, p_prompt, p_sample,
           norm_g, w_in, w_gate_up, b_gate, gla_norm_g, q_norm_g, k_norm_g, rel_bias,
           w_branch_a, w_branch_b, w_out, ple_norm_g, w_ple_gate, w_ple):
    depth = w_in.shape[0]
    bias = _band_bias(rel_bias, BAND_TQ)
    w_al = _feature_major_w_in(w_in)
    xp, xs = x_prompt, x_sample

    def feature_major(a):
        return jnp.transpose(a, (0, 1, 3, 4, 2)).reshape(a.shape[:2] + (WB, a.shape[2]))

    def frame_major(a):
        return jnp.transpose(a.reshape(a.shape[:2] + (HB, DH_B, a.shape[3])), (0, 1, 4, 2, 3))

    kc, vc = feature_major(cache_band_k), feature_major(cache_band_v)
    prompt_out = sample_out = None
    for i in range(depth):
        lw = _layer_weights(i, norm_g, w_gate_up, b_gate, gla_norm_g, q_norm_g, k_norm_g,
                            w_branch_a, w_branch_b, w_out, ple_norm_g, w_ple_gate, w_ple)
        xp, prompt_out = _layer(xp, p_prompt, None, None, None, prompt_out, bias, w_al, i, lw)
        xs, sample_out = _layer(xs, p_sample, state_gla, kc, vc, sample_out, bias, w_al, i, lw)
    (sp, kbp, vbp), (ss, kbs, vbs) = prompt_out, sample_out
    return (xp, xs, sp, frame_major(kbp), frame_major(vbp), ss, frame_major(kbs), frame_major(vbs))
```

```python
import functools

import jax
import jax.numpy as jnp
from jax import lax
from jax.experimental import pallas as pl
from jax.experimental.pallas import tpu as pltpu

D_MODEL = 1024
CHUNK = 64
P_DIM = 256
EPS = 1e-6
NEG = -1e30
HA = 4
DK_A = 128
DV_A = 256
GATE_RANK = 16
GATE_TAU = 16.0
WA_K = HA * DK_A
WA_V = HA * DV_A
HB = 16
DH_B = 64
LEFT_CHUNKS = 8
WINDOW = LEFT_CHUNKS * CHUNK
MAX_REL = 128
WB = HB * DH_B

LANES = 128
GATE_PAD = LANES
SUB = 16
VMEM_LIMIT = 60 * 1024 * 1024
LOG2E = 1.4426950408889634

F32 = jnp.float32
BF16 = jnp.bfloat16


def _dot(a, b):
    return jnp.dot(a, b, preferred_element_type=F32)


def _dot_nt(a, b):
    return lax.dot_general(a, b, (((1,), (1,)), ((), ())), preferred_element_type=F32)


def _resident(shape):
    return pl.BlockSpec(shape, lambda *_: (0,) * len(shape), pipeline_mode=pl.Buffered(1))


def _params(sem):
    return pltpu.CompilerParams(dimension_semantics=sem, vmem_limit_bytes=VMEM_LIMIT)


_W_ROWS = {}
_off = 0
for _name, _width in (('qa', WA_K), ('ka', WA_K), ('va', WA_V), ('ra', GATE_RANK), ('ga', WA_V), ('qb', WB),
                      ('kb', WB), ('vb', WB), ('gb', WB), ('mga', D_MODEL), ('mgb', D_MODEL)):
    _W_ROWS[_name] = slice(_off, _off + (GATE_PAD if _name == 'ra' else _width))
    _off += _width
N_IN = _off


_ACT_COLS = {}
_off = 0
for _name, _width in (('qa', WA_K), ('ka', WA_K), ('va', WA_V), ('sga', WA_V), ('qb', WB), ('kb', WB), ('vb', WB),
                      ('sgb', WB), ('sma', D_MODEL), ('smb', D_MODEL)):
    assert _off % _width == 0
    _ACT_COLS[_name] = slice(_off, _off + _width)
    _off += _width
ACT_W = _off


def _act_block(name):
    cols = _ACT_COLS[name]
    return cols.start // (cols.stop - cols.start)


def _in_proj_kernel(x_ref, ng_ref, w_ref, wgu, bg_ref, qg_ref, kg_ref, act_o, lg_o, kt_o, vt_o, *,
                    feature_major_tail):
    x = x_ref[...]
    ms = jnp.mean(x * x, axis=-1, keepdims=True)
    h = (x * lax.rsqrt(ms + EPS) * ng_ref[...]).astype(BF16)
    proj = lambda name: _dot_nt(h, w_ref[_W_ROWS[name], :])

    def put(name, value):
        act_o[:, _ACT_COLS[name]] = value.astype(BF16)

    put('qa', proj('qa') * (DK_A ** -0.5))
    put('ka', proj('ka'))
    put('va', proj('va'))

    ra = proj('ra')
    gl = _dot(ra.astype(BF16), wgu[...]) + bg_ref[...]
    lg_o[...] = (jnp.minimum(gl, 0.0) - jnp.log(1.0 + jnp.exp(-jnp.abs(gl)))) * (1.0 / GATE_TAU)

    ga = proj('ga')
    put('sga', ga * jax.nn.sigmoid(ga))

    low = lax.broadcasted_iota(jnp.int32, (x.shape[0], LANES), 1) < DH_B

    def head_norm(z, g_ref):
        zz = z * z
        scales = []
        for c in range(WB // LANES):
            t = zz[:, c * LANES:(c + 1) * LANES]
            ss_lo = jnp.sum(jnp.where(low, t, 0.0), axis=-1, keepdims=True)
            ss_hi = jnp.sum(jnp.where(low, 0.0, t), axis=-1, keepdims=True)
            scales.append(jnp.where(low, lax.rsqrt(ss_lo * (1.0 / DH_B) + EPS),
                                    lax.rsqrt(ss_hi * (1.0 / DH_B) + EPS)))
        return z * jnp.concatenate(scales, axis=1) * g_ref[...]

    put('qb', head_norm(proj('qb'), qg_ref) * (DH_B ** -0.5 * LOG2E))
    kb = head_norm(proj('kb'), kg_ref)
    put('kb', kb)
    vb = proj('vb')
    put('vb', vb)
    gb = proj('gb')
    put('sgb', gb * jax.nn.sigmoid(gb))
    put('sma', jax.nn.sigmoid(proj('mga')))
    put('smb', jax.nn.sigmoid(proj('mgb')))

    kt_o[...] = kb.T if feature_major_tail else kb
    vt_o[...] = vb.T if feature_major_tail else vb


def _in_proj(x2d, w_al, layer, lw, tm, seq, keep):
    rows = x2d.shape[0]
    row = lambda w: pl.BlockSpec((tm, w), lambda i: (i, 0))
    widths = (ACT_W, WA_K)
    dtypes = (BF16, F32)
    if keep == seq:
        tail = row(WB)
        tail_shape = jax.ShapeDtypeStruct((rows, WB), F32)
    else:
        assert keep % tm == 0 and seq % tm == 0
        per_seq, per_tail = seq // tm, keep // tm
        tail = pl.BlockSpec((None, WB, tm), lambda i: (i // per_seq, 0,
                                                       jnp.maximum(i % per_seq - (per_seq - per_tail), 0)))
        tail_shape = jax.ShapeDtypeStruct((rows // seq, WB, keep), F32)
    wspec = pl.BlockSpec((None,) + w_al.shape[1:], lambda i: (layer, 0, 0), pipeline_mode=pl.Buffered(1))
    small = (lw['wgu'], lw['bg'], lw['qg'], lw['kg'])
    return pl.pallas_call(
        functools.partial(_in_proj_kernel, feature_major_tail=keep != seq),
        grid=(rows // tm,),
        in_specs=[row(D_MODEL), _resident(lw['ng'].shape), wspec] + [_resident(w.shape) for w in small],
        out_specs=[row(w) for w in widths] + [tail, tail],
        out_shape=[jax.ShapeDtypeStruct((rows, w), dt) for w, dt in zip(widths, dtypes)] + [tail_shape] * 2,
        compiler_params=_params(("arbitrary",)),
        name="in_proj",
    )(x2d, lw['ng'], w_al, *small)


def _gla_block(q_ref, k_ref, lg_ref, v_ref, o_ref, states):
    tb = q_ref.shape[0]
    n_chunks = tb // CHUNK
    n_sub = CHUNK // SUB
    shift = CHUNK.bit_length() - 1
    gw = min(tb, 2 * CHUNK)
    r = lax.broadcasted_iota(jnp.int32, (gw, gw), 0)
    cc = lax.broadcasted_iota(jnp.int32, (gw, gw), 1)
    same_chunk = lax.shift_right_logical(r, shift) == lax.shift_right_logical(cc, shift)
    ltri = jnp.where(same_chunk & (cc <= r), 1.0, 0.0).astype(BF16)
    ltri2 = jnp.concatenate([ltri, ltri], axis=1)
    b_groups = []
    for g0 in range(0, tb, gw):
        lg = lg_ref[g0:g0 + gw, :]
        lg_hi = lg.astype(BF16)
        lg_lo = (lg - lg_hi.astype(F32)).astype(BF16)
        b_groups.append(_dot(ltri2, jnp.concatenate([lg_hi, lg_lo], axis=0)))

    ar = lax.broadcasted_iota(jnp.int32, (CHUNK, n_sub * CHUNK), 0)
    ac = lax.broadcasted_iota(jnp.int32, (CHUNK, n_sub * CHUNK), 1)
    sub_shift = SUB.bit_length() - 1
    a_keep = (lax.shift_right_logical(ar, sub_shift) == lax.shift_right_logical(ac, shift)) \
        & ((ac & (CHUNK - 1)) <= ar)
    krow = lax.broadcasted_iota(jnp.int32, (CHUNK, DK_A), 0)

    for ci in range(n_chunks):
        rows = slice(ci * CHUNK, (ci + 1) * CHUNK)
        for hh in range(HA):
            ks = slice(hh * DK_A, (hh + 1) * DK_A)
            vs = slice(hh * DV_A, (hh + 1) * DV_A)
            g_row = ci * CHUNK % gw
            b = b_groups[ci * CHUNK // gw][g_row:g_row + CHUNK, ks]
            q = q_ref[rows, ks].astype(F32)
            k = k_ref[rows, ks].astype(F32)
            v16 = v_ref[rows, vs]
            b_last = b[CHUNK - 1:CHUNK, :]
            o = _dot((q * jnp.exp(b)).astype(BF16), states[hh].astype(BF16))
            refs_b = [b[sb * SUB:sb * SUB + 1, :] for sb in range(n_sub)]
            ref_rows = jnp.concatenate([jnp.broadcast_to(rb, (SUB, DK_A)) for rb in refs_b], axis=0)
            qt = (q * jnp.exp(b - ref_rows)).astype(BF16)
            kt = jnp.concatenate(
                [(k * jnp.exp(jnp.where(krow < (sb + 1) * SUB, refs_b[sb] - b, 0.0))).astype(BF16)
                 for sb in range(n_sub)], axis=0)
            a = jnp.where(a_keep, _dot_nt(qt, kt), 0.0).astype(BF16)
            o_ref[rows, vs] = (o + _dot(a, jnp.concatenate([v16] * n_sub, axis=0))).astype(o_ref.dtype)
            k_dec_t = (k * jnp.exp(b_last - b)).T
            decay = jnp.exp(b.T[:, CHUNK - 1:CHUNK])
            states[hh] = states[hh] * decay + _dot(k_dec_t.astype(BF16), v16)


def _gla_kernel(q_ref, k_ref, lg_ref, v_ref, s0_ref, *rest, n_extra):
    new_refs = rest[:n_extra]
    has_prev = len(rest) == 3 * n_extra + 4
    prev_refs = rest[n_extra:2 * n_extra + 1] if has_prev else ()
    o_ref = rest[-(n_extra + 3)]
    out_refs = rest[-(n_extra + 2):-1]
    st_ref = rest[-1]
    c = pl.program_id(1)

    @pl.when(c == 0)
    def _():
        st_ref[...] = s0_ref[...]

    states = [st_ref[hh] for hh in range(HA)]
    _gla_block(q_ref, k_ref, lg_ref, v_ref, o_ref, states)
    for hh in range(HA):
        st_ref[hh] = states[hh]

    @pl.when(c == pl.num_programs(1) - 1)
    def _():
        last = out_refs[0].shape[0] - 1
        for new, out in zip((st_ref,) + tuple(new_refs), out_refs):
            out[last] = new[...]
        for prev, out in zip(prev_refs, out_refs):
            out[:last] = prev[...]


def _gla(act, lg, s0, s0_layer, tb, extras, prev):
    bsz, t, _ = act.shape
    n_prev = 0 if prev is None else prev[0].shape[0]
    group = lambda name, w: pl.BlockSpec((None, tb, w), lambda b, c: (b, c, _act_block(name)))
    kspec = pl.BlockSpec((None, tb, WA_K), lambda b, c: (b, c, 0))
    vspec = pl.BlockSpec((None, tb, WA_V), lambda b, c: (b, c, 0))
    s0spec = pl.BlockSpec((None, None, HA, DK_A, DV_A), lambda b, c: (s0_layer, b, 0, 0, 0))

    def per_seq(shape, lead):
        zeros = (0,) * len(shape)
        if lead is None:
            return pl.BlockSpec((None,) + shape, lambda b, c: (b,) + zeros)
        return pl.BlockSpec((lead, None) + shape, lambda b, c: (0, b) + zeros)

    stacked_shapes = [(HA, DK_A, DV_A)] + [e.shape[1:] for e in extras]
    prev = () if prev is None else tuple(prev)
    outs = pl.pallas_call(
        functools.partial(_gla_kernel, n_extra=len(extras)),
        grid=(bsz, t // tb),
        in_specs=[group('qa', WA_K), group('ka', WA_K), kspec, group('va', WA_V), s0spec]
        + [per_seq(e.shape[1:], None) for e in extras]
        + [per_seq(s, n_prev) for s in stacked_shapes[:len(prev)]],
        out_specs=[vspec] + [per_seq(s, n_prev + 1) for s in stacked_shapes],
        out_shape=[jax.ShapeDtypeStruct((bsz, t, WA_V), BF16)]
        + [jax.ShapeDtypeStruct((n_prev + 1, bsz) + s, F32) for s in stacked_shapes],
        scratch_shapes=[pltpu.VMEM((HA, DK_A, DV_A), F32)],
        compiler_params=_params(("arbitrary", "arbitrary")),
        name="gla",
    )(act, act, lg, act, s0, *extras, *prev)
    return outs[0], outs[1:]


def _band_prompt_kernel(*refs, n_pieces):
    q_ref = refs[0]
    k_refs = refs[1:1 + n_pieces]
    v_refs = refs[1 + n_pieces:1 + 2 * n_pieces]
    bias_refs = refs[1 + 2 * n_pieces:1 + 3 * n_pieces]
    o_ref = refs[1 + 3 * n_pieces]
    w = k_refs[0].shape[0]
    lane = lax.broadcasted_iota(jnp.int32, (w, LANES), 1)
    lane_lo = jnp.where(lane < DH_B, 1.0, 0.0)
    sel = (lane_lo.astype(BF16), (1.0 - lane_lo).astype(BF16))

    hq = w // 2
    cols = []
    for half in range(2):
        first = half * hq // CHUNK * CHUNK
        last = ((half + 1) * hq - 1) // CHUNK * CHUNK + WINDOW + CHUNK
        cols.append([slice((max(first, p * w) - p * w) // LANES * LANES,
                           -(-(min(last, (p + 1) * w) - p * w) // LANES) * LANES) for p in range(n_pieces)])

    for hp in range(HB // 2):
        ls = slice(hp * LANES, (hp + 1) * LANES)
        q2 = q_ref[:, ls]
        k2 = [kr[:, ls] for kr in k_refs]
        v2 = [vr[:, ls] for vr in v_refs]
        acc = None
        for e in range(2):
            hd = 2 * hp + e
            qm = q2 * sel[e]
            raw = [_dot_nt(qm, k2[p]) for p in range(n_pieces)]
            ex = [[None, None] for _ in range(n_pieces)]
            for half in range(2):
                rs = slice(half * hq, (half + 1) * hq)
                live = [cols[half][p] for p in range(n_pieces)]
                s = [raw[p][rs, live[p]] + bias_refs[p][hd, rs, live[p]] for p in range(n_pieces)]
                m = functools.reduce(jnp.maximum, [jnp.max(sp, axis=-1, keepdims=True) for sp in s])
                for p in range(n_pieces):
                    pr = jnp.exp2(s[p] - m).astype(BF16)
                    if live[p].start > 0:
                        pr = jnp.concatenate([jnp.zeros((hq, live[p].start), BF16), pr], axis=1)
                    if live[p].stop < w:
                        pr = jnp.concatenate([pr, jnp.zeros((hq, w - live[p].stop), BF16)], axis=1)
                    ex[p][half] = pr
            for p in range(n_pieces):
                part = _dot(jnp.concatenate(ex[p], axis=0), jnp.concatenate([v2[p] * sel[e], sel[e]], axis=1))
                acc = part if acc is None else acc + part
        o_ref[:, ls] = (acc[:, :LANES] / acc[:, LANES:]).astype(o_ref.dtype)


def _band_sample_kernel(q_ref, kc_ref, kn_ref, vc_ref, vn_ref, kt_ref, vt_ref, bias_ref, *rest):
    o_ref, ko_ref, vo_ref = rest[-3:]
    s, lc = q_ref.shape[0], kc_ref.shape[1]
    lane = lax.broadcasted_iota(jnp.int32, (s, LANES), 1)
    lane_lo = jnp.where(lane < DH_B, 1.0, 0.0)
    lane_sel = (lane_lo.astype(BF16), (1.0 - lane_lo).astype(BF16))
    row = lax.broadcasted_iota(jnp.int32, (LANES, lc), 0)
    row_lo = jnp.where(row < DH_B, 1.0, 0.0)
    row_sel = (row_lo.astype(BF16), (1.0 - row_lo).astype(BF16))

    for hp in range(HB // 2):
        ls = slice(hp * LANES, (hp + 1) * LANES)
        q2 = q_ref[:, ls]
        kc2 = kc_ref[ls, :].astype(BF16)
        vc2 = vc_ref[ls, :].astype(BF16)
        kn2, vn2 = kn_ref[:, ls], vn_ref[:, ls]
        acc = None
        for e in range(2):
            hd = 2 * hp + e
            qm = q2 * lane_sel[e]
            s_c = _dot(qm, kc2) + bias_ref[hd, :, :lc]
            s_n = _dot_nt(qm, kn2) + bias_ref[hd, :, lc:]
            m = jnp.maximum(jnp.max(s_c, axis=-1, keepdims=True), jnp.max(s_n, axis=-1, keepdims=True))
            ex_c = jnp.exp2(s_c - m).astype(BF16)
            ex_n = jnp.exp2(s_n - m).astype(BF16)
            part = _dot_nt(ex_c, jnp.concatenate([vc2 * row_sel[e], row_sel[e]], axis=0)) \
                + _dot(ex_n, jnp.concatenate([vn2 * lane_sel[e], lane_sel[e]], axis=1))
            acc = part if acc is None else acc + part
        o_ref[:, ls] = (acc[:, :LANES] / acc[:, LANES:]).astype(o_ref.dtype)

    col = lax.broadcasted_iota(jnp.int32, (WB, lc), 1)
    pad_rows = jnp.zeros((LANES - s, WB), F32)
    pad_cols = jnp.zeros((WB, lc - LANES), F32)
    for c_ref, t_ref, out_ref in ((kc_ref, kt_ref, ko_ref), (vc_ref, vt_ref, vo_ref)):
        new_t = jnp.concatenate([t_ref[...], pad_rows], axis=0).T
        new_t = jnp.concatenate([pad_cols, pltpu.roll(new_t, LANES - s, 1)], axis=1)
        out_ref[out_ref.shape[0] - 1] = jnp.where(col >= lc - s, new_t, pltpu.roll(c_ref[...], lc - s, 1))


def _band_prompt(act, bias, layer, tq):
    bsz, t, _ = act.shape
    n_pieces = WINDOW // tq + 1
    ospec = pl.BlockSpec((None, tq, WB), lambda i, b: (b, i, 0))
    qspec = pl.BlockSpec((None, tq, WB), lambda i, b: (b, i, _act_block('qb')))

    def kspec(name, p):
        back = n_pieces - 1 - p
        return pl.BlockSpec((None, tq, WB), lambda i, b: (b, jnp.maximum(i - back, 0), _act_block(name)))

    def bspec(p):
        back = n_pieces - 1 - p
        return pl.BlockSpec((None, None) + bias.shape[2:],
                            lambda i, b: (layer, jnp.where(i >= back, p, n_pieces), 0, 0, 0))

    kspecs = [kspec(name, p) for name in ('kb', 'vb') for p in range(n_pieces)]
    return pl.pallas_call(
        functools.partial(_band_prompt_kernel, n_pieces=n_pieces),
        grid=(t // tq, bsz),
        in_specs=[qspec] + kspecs + [bspec(p) for p in range(n_pieces)],
        out_specs=ospec,
        out_shape=jax.ShapeDtypeStruct((bsz, t, WB), BF16),
        compiler_params=_params(("arbitrary", "arbitrary")),
        name="band_prompt",
    )(*([act] * (1 + 2 * n_pieces)), *([bias] * n_pieces))


def _band_sample(act, kc_all, vc_all, layer, kt, vt, bias, prev):
    bsz, s, _ = act.shape
    lc = kc_all.shape[3]
    assert s <= LANES <= lc and lc % LANES == 0
    assert bias.shape == (HB, s, lc + s)
    assert (prev is None) == (layer == 0)
    nspec = pl.BlockSpec((None, s, WB), lambda b: (b, 0, 0))
    group = lambda name: pl.BlockSpec((None, s, WB), lambda b: (b, 0, _act_block(name)))
    cspec = pl.BlockSpec((None, None, WB, lc), lambda b: (layer, b, 0, 0))
    own = pl.BlockSpec((1, None, WB, lc), lambda b: (layer, b, 0, 0))
    buf = jax.ShapeDtypeStruct(kc_all.shape, F32)
    prev = () if prev is None else tuple(prev)
    return pl.pallas_call(
        _band_sample_kernel,
        grid=(bsz,),
        in_specs=[group('qb'), cspec, group('kb'), cspec, group('vb'), nspec, nspec, _resident(bias.shape)]
        + [pl.BlockSpec(memory_space=pl.ANY)] * len(prev),
        out_specs=[nspec, own, own],
        out_shape=[jax.ShapeDtypeStruct((bsz, s, WB), BF16), buf, buf],
        input_output_aliases={8: 1, 9: 2} if prev else {},
        compiler_params=_params(("arbitrary",)),
        name="band_sample",
    )(act, kc_all, act, vc_all, act, kt, vt, bias, *prev)


def _post_kernel(oa_ref, sga_ref, ob_ref, sgb_ref, sma_ref, smb_ref, x_ref, p_ref,
                 gg_ref, wa, wb, wo, pg_ref, wpg, wp, out_ref):
    tm = oa_ref.shape[0]
    halves = (slice(0, tm // 2), slice(tm // 2, tm))

    def gla_normed(rs):
        oa = oa_ref[rs, :].astype(F32)
        segs = []
        for hh in range(HA):
            seg = oa[:, hh * DV_A:(hh + 1) * DV_A]
            ms = jnp.mean(seg * seg, axis=-1, keepdims=True)
            segs.append(seg * lax.rsqrt(ms + EPS) * gg_ref[...])
        return jnp.concatenate(segs, axis=-1)

    ya = [_dot((gla_normed(rs) * sga_ref[rs, :]).astype(BF16), wa[...]) for rs in halves]
    yb = [_dot((ob_ref[rs, :] * sgb_ref[rs, :]).astype(BF16), wb[...]) for rs in halves]
    m = [(sma_ref[rs, :] * a + smb_ref[rs, :] * b).astype(BF16) for rs, a, b in zip(halves, ya, yb)]
    x1 = [x_ref[rs, :] + _dot(mm, wo[...]) for rs, mm in zip(halves, m)]
    hn = [(x * lax.rsqrt(jnp.mean(x * x, axis=-1, keepdims=True) + EPS) * pg_ref[...]).astype(BF16) for x in x1]
    gate = [jax.nn.sigmoid(_dot(h, wpg[...])) for h in hn]
    pe = [_dot(p_ref[rs, :].astype(BF16), wp[...]) for rs in halves]
    for rs, x, g, e in zip(halves, x1, gate, pe):
        out_ref[rs, :] = x + g * e


def _post(oa, ob, act, x2d, p_all, layer, lw, tm):
    rows = x2d.shape[0]
    row = lambda w: pl.BlockSpec((tm, w), lambda i: (i, 0))
    group = lambda name: pl.BlockSpec((tm, D_MODEL), lambda i: (i, _act_block(name)))
    pspec = pl.BlockSpec((None, tm, P_DIM), lambda i: (layer, i, 0))
    weights = (lw['gg'], lw['wa'], lw['wb'], lw['wo'], lw['pg'], lw['wpg'], lw['wp'])
    return pl.pallas_call(
        _post_kernel,
        grid=(rows // tm,),
        in_specs=[row(WA_V), group('sga'), row(WB), group('sgb'), group('sma'), group('smb'), row(D_MODEL), pspec]
        + [_resident(w.shape) for w in weights],
        out_specs=row(D_MODEL),
        out_shape=jax.ShapeDtypeStruct((rows, D_MODEL), F32),
        compiler_params=_params(("parallel",)),
        name="post",
    )(oa, act, ob, act, act, act, x2d, p_all, *weights)


def _feature_major_w_in(w_in):
    assert w_in.shape[1:] == (D_MODEL, N_IN)
    return jnp.transpose(w_in, (0, 2, 1)).astype(BF16)


def _layer_weights(i, norm_g, w_gate_up, b_gate, gla_norm_g, q_norm_g, k_norm_g,
                   w_branch_a, w_branch_b, w_out, ple_norm_g, w_ple_gate, w_ple):
    lw = {}
    lw['wgu'] = jnp.pad(w_gate_up[i].astype(BF16), ((0, GATE_PAD - GATE_RANK), (0, 0)))
    lw['bg'] = b_gate[i].reshape(1, WA_K)
    lw['ng'] = norm_g[i].reshape(1, D_MODEL)
    lw['qg'] = jnp.tile(q_norm_g[i], HB).reshape(1, WB)
    lw['kg'] = jnp.tile(k_norm_g[i], HB).reshape(1, WB)
    lw['gg'] = gla_norm_g[i].reshape(1, DV_A)
    lw['wa'] = w_branch_a[i].astype(BF16)
    lw['wb'] = w_branch_b[i].astype(BF16)
    lw['wo'] = w_out[i].astype(BF16)
    lw['pg'] = ple_norm_g[i].reshape(1, D_MODEL)
    lw['wpg'] = w_ple_gate[i].astype(BF16)
    lw['wp'] = w_ple[i].astype(BF16)
    return lw


def _bias_kernel(vec_ref, o_ref, *, tq, nk):
    n_pieces = o_ref.shape[0] - 1
    period = vec_ref.shape[-1]
    x = jnp.broadcast_to(vec_ref[...], (tq, period))
    toep = pltpu.roll(x, 0, 1, stride=1, stride_axis=0)[:, :nk]
    r = lax.broadcasted_iota(jnp.int32, (tq, nk), 0)
    c = lax.broadcasted_iota(jnp.int32, (tq, nk), 1)
    shift = CHUNK.bit_length() - 1
    dchunk = lax.shift_right_logical(r + WINDOW, shift) - lax.shift_right_logical(c, shift)
    table = jnp.where((dchunk >= 0) & (dchunk <= LEFT_CHUNKS), toep, NEG)
    for p in range(n_pieces):
        o_ref[p] = table[:, p * tq:(p + 1) * tq]
    o_ref[n_pieces] = jnp.full((tq, tq), NEG, F32)


def _band_bias(rel_bias, tq):
    depth = rel_bias.shape[0]
    nk = WINDOW + tq
    assert nk % tq == 0
    n_slots = nk // tq + 1
    period = nk + tq
    tab = rel_bias * LOG2E
    n_hi = WINDOW - MAX_REL
    far = jnp.broadcast_to(tab[..., -1:], (depth, HB, n_hi))
    near = jnp.broadcast_to(tab[..., :1], (depth, HB, max(nk - n_hi - (2 * MAX_REL + 1), 0)))
    wrap = jnp.broadcast_to(tab[..., -1:], (depth, HB, period - nk))
    body = jnp.concatenate([far, tab[..., ::-1], near], axis=-1)[..., :nk]
    vec = jnp.concatenate([body, wrap], axis=-1).reshape(depth, HB, 1, period)
    return pl.pallas_call(
        functools.partial(_bias_kernel, tq=tq, nk=nk),
        grid=(depth, HB),
        in_specs=[pl.BlockSpec((None, None, 1, period), lambda l, h: (l, h, 0, 0))],
        out_specs=pl.BlockSpec((None, n_slots, None, tq, tq), lambda l, h: (l, 0, h, 0, 0)),
        out_shape=jax.ShapeDtypeStruct((depth, n_slots, HB, tq, tq), F32),
        compiler_params=_params(("parallel", "parallel")),
        name="band_bias",
    )(vec)


BAND_TQ = 256


def _tiles(rows, seq):
    return dict(tm=min(rows, 512), tm_post=min(rows, 1024), tb=min(seq, 1024))


def _layer(x, p_all, s0_all, kc, vc, prev, bias, w_al, layer, lw):
    bsz, t, _ = x.shape
    x2d = x.reshape(bsz * t, D_MODEL)
    keep = min(WINDOW, t)
    tiles = _tiles(bsz * t, t)
    tm, tm_post, tb, tq = tiles['tm'], tiles['tm_post'], tiles['tb'], BAND_TQ
    act, lg, ktail, vtail = _in_proj(x2d, w_al, layer, lw, tm, t, keep)
    r3 = lambda a: a.reshape(bsz, -1, a.shape[-1])
    if s0_all is None:
        s0_all, s_layer = jnp.zeros((1, bsz, HA, DK_A, DV_A), F32), 0
    else:
        s_layer = layer
    if kc is None:
        oa, (st, kbuf, vbuf) = _gla(r3(act), r3(lg), s0_all, s_layer, tb, (ktail, vtail), prev)
        ob = _band_prompt(r3(act), bias, layer, tq)
    else:
        oa, (st,) = _gla(r3(act), r3(lg), s0_all, s_layer, tb, (), None if prev is None else prev[:1])
        lc = kc.shape[3]
        assert lc == WINDOW and t <= CHUNK
        table = jnp.concatenate([bias[layer, p, :, :t, :] for p in range(bias.shape[1] - 1)], axis=-1)
        ob, kbuf, vbuf = _band_sample(r3(act), kc, vc, layer, r3(ktail), r3(vtail), table[:, :, :lc + t],
                                      None if prev is None else prev[1:])
    x_new = _post(oa.reshape(bsz * t, WA_V), ob.reshape(bsz * t, WB), act, x2d,
                  p_all.reshape(p_all.shape[0], bsz * t, P_DIM), layer, lw, tm_post)
    return x_new.reshape(bsz, t, D_MODEL), (st, kbuf, vbuf)


def kernel(x_prompt, x_sample, state_gla, cache_band_k, cache_band_v, p_prompt, p_sample,
           norm_g, w_in, w_gate_up, b_gate, gla_norm_g, q_norm_g, k_norm_g, rel_bias,
           w_branch_a, w_branch_b, w_out, ple_norm_g, w_ple_gate, w_ple):
    depth = w_in.shape[0]
    bias = _band_bias(rel_bias, BAND_TQ)
    w_al = _feature_major_w_in(w_in)
    xp, xs = x_prompt, x_sample

    def feature_major(a):
        return jnp.transpose(a, (0, 1, 3, 4, 2)).reshape(a.shape[:2] + (WB, a.shape[2]))

    def frame_major(a):
        return jnp.transpose(a.reshape(a.shape[:2] + (HB, DH_B, a.shape[3])), (0, 1, 4, 2, 3))

    kc, vc = feature_major(cache_band_k), feature_major(cache_band_v)
    prompt_out = sample_out = None
    for i in range(depth):
        lw = _layer_weights(i, norm_g, w_gate_up, b_gate, gla_norm_g, q_norm_g, k_norm_g,
                            w_branch_a, w_branch_b, w_out, ple_norm_g, w_ple_gate, w_ple)
        xp, prompt_out = _layer(xp, p_prompt, None, None, None, prompt_out, bias, w_al, i, lw)
        xs, sample_out = _layer(xs, p_sample, state_gla, kc, vc, sample_out, bias, w_al, i, lw)
    (sp, kbp, vbp), (ss, kbs, vbs) = prompt_out, sample_out
    return (xp, xs, sp, frame_major(kbp), frame_major(vbp), ss, frame_major(kbs), frame_major(vbs))
```

```python
import functools

import jax
import jax.numpy as jnp
from jax import lax
from jax.experimental import pallas as pl
from jax.experimental.pallas import tpu as pltpu

D_MODEL = 1024
CHUNK = 64
P_DIM = 256
EPS = 1e-6
NEG = -1e30
HA = 4
DK_A = 128
DV_A = 256
GATE_RANK = 16
GATE_TAU = 16.0
WA_K = HA * DK_A
WA_V = HA * DV_A
HB = 16
DH_B = 64
LEFT_CHUNKS = 8
WINDOW = LEFT_CHUNKS * CHUNK
MAX_REL = 128
WB = HB * DH_B

LANES = 128
GATE_PAD = LANES
SUB = 16
VMEM_LIMIT = 60 * 1024 * 1024
LOG2E = 1.4426950408889634

F32 = jnp.float32
BF16 = jnp.bfloat16


def _dot(a, b):
    return jnp.dot(a, b, preferred_element_type=F32)


def _dot_nt(a, b):
    return lax.dot_general(a, b, (((1,), (1,)), ((), ())), preferred_element_type=F32)


def _resident(shape):
    return pl.BlockSpec(shape, lambda *_: (0,) * len(shape), pipeline_mode=pl.Buffered(1))


def _params(sem):
    return pltpu.CompilerParams(dimension_semantics=sem, vmem_limit_bytes=VMEM_LIMIT)


_W_ROWS = {}
_off = 0
for _name, _width in (('qa', WA_K), ('ka', WA_K), ('va', WA_V), ('ra', GATE_RANK), ('ga', WA_V), ('qb', WB),
                      ('kb', WB), ('vb', WB), ('gb', WB), ('mga', D_MODEL), ('mgb', D_MODEL)):
    _W_ROWS[_name] = slice(_off, _off + (GATE_PAD if _name == 'ra' else _width))
    _off += _width
N_IN = _off


_ACT_COLS = {}
_off = 0
for _name, _width in (('qa', WA_K), ('ka', WA_K), ('va', WA_V), ('sga', WA_V), ('qb', WB), ('kb', WB), ('vb', WB),
                      ('sgb', WB), ('sma', D_MODEL), ('smb', D_MODEL)):
    assert _off % _width == 0
    _ACT_COLS[_name] = slice(_off, _off + _width)
    _off += _width
ACT_W = _off


def _act_block(name):
    cols = _ACT_COLS[name]
    return cols.start // (cols.stop - cols.start)


def _in_proj_kernel(x_ref, ng_ref, w_ref, wgu, bg_ref, qg_ref, kg_ref, act_o, lg_o, kt_o, vt_o, *,
                    feature_major_tail):
    x = x_ref[...]
    ms = jnp.mean(x * x, axis=-1, keepdims=True)
    h = (x * lax.rsqrt(ms + EPS) * ng_ref[...]).astype(BF16)
    proj = lambda name: _dot_nt(h, w_ref[_W_ROWS[name], :])

    def put(name, value):
        act_o[:, _ACT_COLS[name]] = value.astype(BF16)

    put('qa', proj('qa') * (DK_A ** -0.5))
    put('ka', proj('ka'))
    put('va', proj('va'))

    ra = proj('ra')
    gl = _dot(ra.astype(BF16), wgu[...]) + bg_ref[...]
    lg_o[...] = (jnp.minimum(gl, 0.0) - jnp.log(1.0 + jnp.exp(-jnp.abs(gl)))) * (1.0 / GATE_TAU)

    ga = proj('ga')
    put('sga', ga * jax.nn.sigmoid(ga))

    low = lax.broadcasted_iota(jnp.int32, (x.shape[0], LANES), 1) < DH_B

    def head_norm(z, g_ref):
        zz = z * z
        scales = []
        for c in range(WB // LANES):
            t = zz[:, c * LANES:(c + 1) * LANES]
            ss_lo = jnp.sum(jnp.where(low, t, 0.0), axis=-1, keepdims=True)
            ss_hi = jnp.sum(jnp.where(low, 0.0, t), axis=-1, keepdims=True)
            scales.append(jnp.where(low, lax.rsqrt(ss_lo * (1.0 / DH_B) + EPS),
                                    lax.rsqrt(ss_hi * (1.0 / DH_B) + EPS)))
        return z * jnp.concatenate(scales, axis=1) * g_ref[...]

    put('qb', head_norm(proj('qb'), qg_ref) * (DH_B ** -0.5 * LOG2E))
    kb = head_norm(proj('kb'), kg_ref)
    put('kb', kb)
    vb = proj('vb')
    put('vb', vb)
    gb = proj('gb')
    put('sgb', gb * jax.nn.sigmoid(gb))
    put('sma', jax.nn.sigmoid(proj('mga')))
    put('smb', jax.nn.sigmoid(proj('mgb')))

    kt_o[...] = kb.T if feature_major_tail else kb
    vt_o[...] = vb.T if feature_major_tail else vb


def _in_proj(x2d, w_al, layer, lw, tm, seq, keep):
    rows = x2d.shape[0]
    row = lambda w: pl.BlockSpec((tm, w), lambda i: (i, 0))
    widths = (ACT_W, WA_K)
    dtypes = (BF16, F32)
    if keep == seq:
        tail = row(WB)
        tail_shape = jax.ShapeDtypeStruct((rows, WB), F32)
    else:
        assert keep % tm == 0 and seq % tm == 0
        per_seq, per_tail = seq // tm, keep // tm
        tail = pl.BlockSpec((None, WB, tm), lambda i: (i // per_seq, 0,
                                                       jnp.maximum(i % per_seq - (per_seq - per_tail), 0)))
        tail_shape = jax.ShapeDtypeStruct((rows // seq, WB, keep), F32)
    wspec = pl.BlockSpec((None,) + w_al.shape[1:], lambda i: (layer, 0, 0), pipeline_mode=pl.Buffered(1))
    small = (lw['wgu'], lw['bg'], lw['qg'], lw['kg'])
    return pl.pallas_call(
        functools.partial(_in_proj_kernel, feature_major_tail=keep != seq),
        grid=(rows // tm,),
        in_specs=[row(D_MODEL), _resident(lw['ng'].shape), wspec] + [_resident(w.shape) for w in small],
        out_specs=[row(w) for w in widths] + [tail, tail],
        out_shape=[jax.ShapeDtypeStruct((rows, w), dt) for w, dt in zip(widths, dtypes)] + [tail_shape] * 2,
        compiler_params=_params(("arbitrary",)),
        name="in_proj",
    )(x2d, lw['ng'], w_al, *small)


def _gla_block(q_ref, k_ref, lg_ref, v_ref, o_ref, states):
    tb = q_ref.shape[0]
    n_chunks = tb // CHUNK
    n_sub = CHUNK // SUB
    shift = CHUNK.bit_length() - 1
    gw = min(tb, 2 * CHUNK)
    r = lax.broadcasted_iota(jnp.int32, (gw, gw), 0)
    cc = lax.broadcasted_iota(jnp.int32, (gw, gw), 1)
    same_chunk = lax.shift_right_logical(r, shift) == lax.shift_right_logical(cc, shift)
    ltri = jnp.where(same_chunk & (cc <= r), 1.0, 0.0).astype(BF16)
    ltri2 = jnp.concatenate([ltri, ltri], axis=1)
    b_groups = []
    for g0 in range(0, tb, gw):
        lg = lg_ref[g0:g0 + gw, :]
        lg_hi = lg.astype(BF16)
        lg_lo = (lg - lg_hi.astype(F32)).astype(BF16)
        b_groups.append(_dot(ltri2, jnp.concatenate([lg_hi, lg_lo], axis=0)))

    ar = lax.broadcasted_iota(jnp.int32, (CHUNK, n_sub * CHUNK), 0)
    ac = lax.broadcasted_iota(jnp.int32, (CHUNK, n_sub * CHUNK), 1)
    sub_shift = SUB.bit_length() - 1
    a_keep = (lax.shift_right_logical(ar, sub_shift) == lax.shift_right_logical(ac, shift)) \
        & ((ac & (CHUNK - 1)) <= ar)
    krow = lax.broadcasted_iota(jnp.int32, (CHUNK, DK_A), 0)

    for ci in range(n_chunks):
        rows = slice(ci * CHUNK, (ci + 1) * CHUNK)
        for hh in range(HA):
            ks = slice(hh * DK_A, (hh + 1) * DK_A)
            vs = slice(hh * DV_A, (hh + 1) * DV_A)
            g_row = ci * CHUNK % gw
            b = b_groups[ci * CHUNK // gw][g_row:g_row + CHUNK, ks]
            q = q_ref[rows, ks].astype(F32)
            k = k_ref[rows, ks].astype(F32)
            v16 = v_ref[rows, vs]
            b_last = b[CHUNK - 1:CHUNK, :]
            o = _dot((q * jnp.exp(b)).astype(BF16), states[hh].astype(BF16))
            refs_b = [b[sb * SUB:sb * SUB + 1, :] for sb in range(n_sub)]
            ref_rows = jnp.concatenate([jnp.broadcast_to(rb, (SUB, DK_A)) for rb in refs_b], axis=0)
            qt = (q * jnp.exp(b - ref_rows)).astype(BF16)
            kt = jnp.concatenate(
                [(k * jnp.exp(jnp.where(krow < (sb + 1) * SUB, refs_b[sb] - b, 0.0))).astype(BF16)
                 for sb in range(n_sub)], axis=0)
            a = jnp.where(a_keep, _dot_nt(qt, kt), 0.0).astype(BF16)
            o_ref[rows, vs] = (o + _dot(a, jnp.concatenate([v16] * n_sub, axis=0))).astype(o_ref.dtype)
            k_dec_t = (k * jnp.exp(b_last - b)).T
            decay = jnp.exp(b.T[:, CHUNK - 1:CHUNK])
            states[hh] = states[hh] * decay + _dot(k_dec_t.astype(BF16), v16)


def _gla_kernel(q_ref, k_ref, lg_ref, v_ref, s0_ref, *rest, n_extra):
    new_refs = rest[:n_extra]
    has_prev = len(rest) == 3 * n_extra + 4
    prev_refs = rest[n_extra:2 * n_extra + 1] if has_prev else ()
    o_ref = rest[-(n_extra + 3)]
    out_refs = rest[-(n_extra + 2):-1]
    st_ref = rest[-1]
    c = pl.program_id(1)

    @pl.when(c == 0)
    def _():
        st_ref[...] = s0_ref[...]

    states = [st_ref[hh] for hh in range(HA)]
    _gla_block(q_ref, k_ref, lg_ref, v_ref, o_ref, states)
    for hh in range(HA):
        st_ref[hh] = states[hh]

    @pl.when(c == pl.num_programs(1) - 1)
    def _():
        last = out_refs[0].shape[0] - 1
        for new, out in zip((st_ref,) + tuple(new_refs), out_refs):
            out[last] = new[...]


def _gla(act, lg, s0, s0_layer, tb, extras, prev, layer, depth):
    bsz, t, _ = act.shape
    assert (prev is None) == (layer == 0)
    group = lambda name, w: pl.BlockSpec((None, tb, w), lambda b, c: (b, c, _act_block(name)))
    kspec = pl.BlockSpec((None, tb, WA_K), lambda b, c: (b, c, 0))
    vspec = pl.BlockSpec((None, tb, WA_V), lambda b, c: (b, c, 0))
    s0spec = pl.BlockSpec((None, None, HA, DK_A, DV_A), lambda b, c: (s0_layer, b, 0, 0, 0))

    def per_seq(shape, lead):
        zeros = (0,) * len(shape)
        if lead is None:
            return pl.BlockSpec((None,) + shape, lambda b, c: (b,) + zeros)
        return pl.BlockSpec((lead, None) + shape, lambda b, c: (layer, b) + zeros)

    stacked_shapes = [(HA, DK_A, DV_A)] + [e.shape[1:] for e in extras]
    prev = () if prev is None else tuple(prev)
    first_prev = 5 + len(extras)
    outs = pl.pallas_call(
        functools.partial(_gla_kernel, n_extra=len(extras)),
        grid=(bsz, t // tb),
        in_specs=[group('qa', WA_K), group('ka', WA_K), kspec, group('va', WA_V), s0spec]
        + [per_seq(e.shape[1:], None) for e in extras]
        + [pl.BlockSpec(memory_space=pl.ANY)] * len(prev),
        out_specs=[vspec] + [per_seq(s, 1) for s in stacked_shapes],
        out_shape=[jax.ShapeDtypeStruct((bsz, t, WA_V), BF16)]
        + [jax.ShapeDtypeStruct((depth, bsz) + s, F32) for s in stacked_shapes],
        input_output_aliases={first_prev + j: 1 + j for j in range(len(prev))},
        scratch_shapes=[pltpu.VMEM((HA, DK_A, DV_A), F32)],
        compiler_params=_params(("arbitrary", "arbitrary")),
        name="gla",
    )(act, act, lg, act, s0, *extras, *prev)
    return outs[0], outs[1:]


def _band_prompt_kernel(*refs, n_pieces):
    q_ref = refs[0]
    k_refs = refs[1:1 + n_pieces]
    v_refs = refs[1 + n_pieces:1 + 2 * n_pieces]
    bias_refs = refs[1 + 2 * n_pieces:1 + 3 * n_pieces]
    o_ref = refs[1 + 3 * n_pieces]
    w = k_refs[0].shape[0]
    lane = lax.broadcasted_iota(jnp.int32, (w, LANES), 1)
    lane_lo = jnp.where(lane < DH_B, 1.0, 0.0)
    sel = (lane_lo.astype(BF16), (1.0 - lane_lo).astype(BF16))

    hq = w // 2
    cols = []
    for half in range(2):
        first = half * hq // CHUNK * CHUNK
        last = ((half + 1) * hq - 1) // CHUNK * CHUNK + WINDOW + CHUNK
        cols.append([slice((max(first, p * w) - p * w) // LANES * LANES,
                           -(-(min(last, (p + 1) * w) - p * w) // LANES) * LANES) for p in range(n_pieces)])

    for hp in range(HB // 2):
        ls = slice(hp * LANES, (hp + 1) * LANES)
        q2 = q_ref[:, ls]
        k2 = [kr[:, ls] for kr in k_refs]
        v2 = [vr[:, ls] for vr in v_refs]
        acc = None
        for e in range(2):
            hd = 2 * hp + e
            qm = q2 * sel[e]
            raw = [_dot_nt(qm, k2[p]) for p in range(n_pieces)]
            ex = [[None, None] for _ in range(n_pieces)]
            for half in range(2):
                rs = slice(half * hq, (half + 1) * hq)
                live = [cols[half][p] for p in range(n_pieces)]
                s = [raw[p][rs, live[p]] + bias_refs[p][hd, rs, live[p]] for p in range(n_pieces)]
                m = functools.reduce(jnp.maximum, [jnp.max(sp, axis=-1, keepdims=True) for sp in s])
                for p in range(n_pieces):
                    pr = jnp.exp2(s[p] - m).astype(BF16)
                    if live[p].start > 0:
                        pr = jnp.concatenate([jnp.zeros((hq, live[p].start), BF16), pr], axis=1)
                    if live[p].stop < w:
                        pr = jnp.concatenate([pr, jnp.zeros((hq, w - live[p].stop), BF16)], axis=1)
                    ex[p][half] = pr
            for p in range(n_pieces):
                part = _dot(jnp.concatenate(ex[p], axis=0), jnp.concatenate([v2[p] * sel[e], sel[e]], axis=1))
                acc = part if acc is None else acc + part
        o_ref[:, ls] = (acc[:, :LANES] / acc[:, LANES:]).astype(o_ref.dtype)


def _band_sample_kernel(q_ref, kc_ref, kn_ref, vc_ref, vn_ref, kt_ref, vt_ref, bias_ref, *rest):
    o_ref, ko_ref, vo_ref = rest[-3:]
    s, lc = q_ref.shape[0], kc_ref.shape[1]
    lane = lax.broadcasted_iota(jnp.int32, (s, LANES), 1)
    lane_lo = jnp.where(lane < DH_B, 1.0, 0.0)
    lane_sel = (lane_lo.astype(BF16), (1.0 - lane_lo).astype(BF16))
    row = lax.broadcasted_iota(jnp.int32, (LANES, lc), 0)
    row_lo = jnp.where(row < DH_B, 1.0, 0.0)
    row_sel = (row_lo.astype(BF16), (1.0 - row_lo).astype(BF16))

    for hp in range(HB // 2):
        ls = slice(hp * LANES, (hp + 1) * LANES)
        q2 = q_ref[:, ls]
        kc2 = kc_ref[ls, :].astype(BF16)
        vc2 = vc_ref[ls, :].astype(BF16)
        kn2, vn2 = kn_ref[:, ls], vn_ref[:, ls]
        acc = None
        for e in range(2):
            hd = 2 * hp + e
            qm = q2 * lane_sel[e]
            s_c = _dot(qm, kc2) + bias_ref[hd, :, :lc]
            s_n = _dot_nt(qm, kn2) + bias_ref[hd, :, lc:]
            m = jnp.maximum(jnp.max(s_c, axis=-1, keepdims=True), jnp.max(s_n, axis=-1, keepdims=True))
            ex_c = jnp.exp2(s_c - m).astype(BF16)
            ex_n = jnp.exp2(s_n - m).astype(BF16)
            part = _dot_nt(ex_c, jnp.concatenate([vc2 * row_sel[e], row_sel[e]], axis=0)) \
                + _dot(ex_n, jnp.concatenate([vn2 * lane_sel[e], lane_sel[e]], axis=1))
            acc = part if acc is None else acc + part
        o_ref[:, ls] = (acc[:, :LANES] / acc[:, LANES:]).astype(o_ref.dtype)

    col = lax.broadcasted_iota(jnp.int32, (WB, lc), 1)
    pad_rows = jnp.zeros((LANES - s, WB), F32)
    pad_cols = jnp.zeros((WB, lc - LANES), F32)
    for c_ref, t_ref, out_ref in ((kc_ref, kt_ref, ko_ref), (vc_ref, vt_ref, vo_ref)):
        new_t = jnp.concatenate([t_ref[...], pad_rows], axis=0).T
        new_t = jnp.concatenate([pad_cols, pltpu.roll(new_t, LANES - s, 1)], axis=1)
        out_ref[out_ref.shape[0] - 1] = jnp.where(col >= lc - s, new_t, pltpu.roll(c_ref[...], lc - s, 1))


def _band_prompt(act, bias, layer, tq):
    bsz, t, _ = act.shape
    n_pieces = WINDOW // tq + 1
    ospec = pl.BlockSpec((None, tq, WB), lambda i, b: (b, i, 0))
    qspec = pl.BlockSpec((None, tq, WB), lambda i, b: (b, i, _act_block('qb')))

    def kspec(name, p):
        back = n_pieces - 1 - p
        return pl.BlockSpec((None, tq, WB), lambda i, b: (b, jnp.maximum(i - back, 0), _act_block(name)))

    def bspec(p):
        back = n_pieces - 1 - p
        return pl.BlockSpec((None, None) + bias.shape[2:],
                            lambda i, b: (layer, jnp.where(i >= back, p, n_pieces), 0, 0, 0))

    kspecs = [kspec(name, p) for name in ('kb', 'vb') for p in range(n_pieces)]
    return pl.pallas_call(
        functools.partial(_band_prompt_kernel, n_pieces=n_pieces),
        grid=(t // tq, bsz),
        in_specs=[qspec] + kspecs + [bspec(p) for p in range(n_pieces)],
        out_specs=ospec,
        out_shape=jax.ShapeDtypeStruct((bsz, t, WB), BF16),
        compiler_params=_params(("arbitrary", "arbitrary")),
        name="band_prompt",
    )(*([act] * (1 + 2 * n_pieces)), *([bias] * n_pieces))


def _band_sample(act, kc_all, vc_all, layer, kt, vt, bias, prev):
    bsz, s, _ = act.shape
    lc = kc_all.shape[3]
    assert s <= LANES <= lc and lc % LANES == 0
    assert bias.shape == (HB, s, lc + s)
    assert (prev is None) == (layer == 0)
    nspec = pl.BlockSpec((None, s, WB), lambda b: (b, 0, 0))
    group = lambda name: pl.BlockSpec((None, s, WB), lambda b: (b, 0, _act_block(name)))
    cspec = pl.BlockSpec((None, None, WB, lc), lambda b: (layer, b, 0, 0))
    own = pl.BlockSpec((1, None, WB, lc), lambda b: (layer, b, 0, 0))
    buf = jax.ShapeDtypeStruct(kc_all.shape, F32)
    prev = () if prev is None else tuple(prev)
    return pl.pallas_call(
        _band_sample_kernel,
        grid=(bsz,),
        in_specs=[group('qb'), cspec, group('kb'), cspec, group('vb'), nspec, nspec, _resident(bias.shape)]
        + [pl.BlockSpec(memory_space=pl.ANY)] * len(prev),
        out_specs=[nspec, own, own],
        out_shape=[jax.ShapeDtypeStruct((bsz, s, WB), BF16), buf, buf],
        input_output_aliases={8: 1, 9: 2} if prev else {},
        compiler_params=_params(("arbitrary",)),
        name="band_sample",
    )(act, kc_all, act, vc_all, act, kt, vt, bias, *prev)


def _post_kernel(oa_ref, sga_ref, ob_ref, sgb_ref, sma_ref, smb_ref, x_ref, p_ref,
                 gg_ref, wa, wb, wo, pg_ref, wpg, wp, out_ref):
    tm = oa_ref.shape[0]
    halves = (slice(0, tm // 2), slice(tm // 2, tm))

    def gla_normed(rs):
        oa = oa_ref[rs, :].astype(F32)
        segs = []
        for hh in range(HA):
            seg = oa[:, hh * DV_A:(hh + 1) * DV_A]
            ms = jnp.mean(seg * seg, axis=-1, keepdims=True)
            segs.append(seg * lax.rsqrt(ms + EPS) * gg_ref[...])
        return jnp.concatenate(segs, axis=-1)

    ya = [_dot((gla_normed(rs) * sga_ref[rs, :]).astype(BF16), wa[...]) for rs in halves]
    yb = [_dot((ob_ref[rs, :] * sgb_ref[rs, :]).astype(BF16), wb[...]) for rs in halves]
    m = [(sma_ref[rs, :] * a + smb_ref[rs, :] * b).astype(BF16) for rs, a, b in zip(halves, ya, yb)]
    x1 = [x_ref[rs, :] + _dot(mm, wo[...]) for rs, mm in zip(halves, m)]
    hn = [(x * lax.rsqrt(jnp.mean(x * x, axis=-1, keepdims=True) + EPS) * pg_ref[...]).astype(BF16) for x in x1]
    gate = [jax.nn.sigmoid(_dot(h, wpg[...])) for h in hn]
    pe = [_dot(p_ref[rs, :].astype(BF16), wp[...]) for rs in halves]
    for rs, x, g, e in zip(halves, x1, gate, pe):
        out_ref[rs, :] = x + g * e


def _post(oa, ob, act, x2d, p_all, layer, lw, tm):
    rows = x2d.shape[0]
    row = lambda w: pl.BlockSpec((tm, w), lambda i: (i, 0))
    group = lambda name: pl.BlockSpec((tm, D_MODEL), lambda i: (i, _act_block(name)))
    pspec = pl.BlockSpec((None, tm, P_DIM), lambda i: (layer, i, 0))
    weights = (lw['gg'], lw['wa'], lw['wb'], lw['wo'], lw['pg'], lw['wpg'], lw['wp'])
    return pl.pallas_call(
        _post_kernel,
        grid=(rows // tm,),
        in_specs=[row(WA_V), group('sga'), row(WB), group('sgb'), group('sma'), group('smb'), row(D_MODEL), pspec]
        + [_resident(w.shape) for w in weights],
        out_specs=row(D_MODEL),
        out_shape=jax.ShapeDtypeStruct((rows, D_MODEL), F32),
        compiler_params=_params(("parallel",)),
        name="post",
    )(oa, act, ob, act, act, act, x2d, p_all, *weights)


def _feature_major_w_in(w_in):
    assert w_in.shape[1:] == (D_MODEL, N_IN)
    return jnp.transpose(w_in, (0, 2, 1)).astype(BF16)


def _layer_weights(i, norm_g, w_gate_up, b_gate, gla_norm_g, q_norm_g, k_norm_g,
                   w_branch_a, w_branch_b, w_out, ple_norm_g, w_ple_gate, w_ple):
    lw = {}
    lw['wgu'] = jnp.pad(w_gate_up[i].astype(BF16), ((0, GATE_PAD - GATE_RANK), (0, 0)))
    lw['bg'] = b_gate[i].reshape(1, WA_K)
    lw['ng'] = norm_g[i].reshape(1, D_MODEL)
    lw['qg'] = jnp.tile(q_norm_g[i], HB).reshape(1, WB)
    lw['kg'] = jnp.tile(k_norm_g[i], HB).reshape(1, WB)
    lw['gg'] = gla_norm_g[i].reshape(1, DV_A)
    lw['wa'] = w_branch_a[i].astype(BF16)
    lw['wb'] = w_branch_b[i].astype(BF16)
    lw['wo'] = w_out[i].astype(BF16)
    lw['pg'] = ple_norm_g[i].reshape(1, D_MODEL)
    lw['wpg'] = w_ple_gate[i].astype(BF16)
    lw['wp'] = w_ple[i].astype(BF16)
    return lw


def _bias_kernel(vec_ref, o_ref, *, tq, nk):
    n_pieces = o_ref.shape[0] - 1
    period = vec_ref.shape[-1]
    x = jnp.broadcast_to(vec_ref[...], (tq, period))
    toep = pltpu.roll(x, 0, 1, stride=1, stride_axis=0)[:, :nk]
    r = lax.broadcasted_iota(jnp.int32, (tq, nk), 0)
    c = lax.broadcasted_iota(jnp.int32, (tq, nk), 1)
    shift = CHUNK.bit_length() - 1
    dchunk = lax.shift_right_logical(r + WINDOW, shift) - lax.shift_right_logical(c, shift)
    table = jnp.where((dchunk >= 0) & (dchunk <= LEFT_CHUNKS), toep, NEG)
    for p in range(n_pieces):
        o_ref[p] = table[:, p * tq:(p + 1) * tq]
    o_ref[n_pieces] = jnp.full((tq, tq), NEG, F32)


def _band_bias(rel_bias, tq):
    depth = rel_bias.shape[0]
    nk = WINDOW + tq
    assert nk % tq == 0
    n_slots = nk // tq + 1
    period = nk + tq
    tab = rel_bias * LOG2E
    n_hi = WINDOW - MAX_REL
    far = jnp.broadcast_to(tab[..., -1:], (depth, HB, n_hi))
    near = jnp.broadcast_to(tab[..., :1], (depth, HB, max(nk - n_hi - (2 * MAX_REL + 1), 0)))
    wrap = jnp.broadcast_to(tab[..., -1:], (depth, HB, period - nk))
    body = jnp.concatenate([far, tab[..., ::-1], near], axis=-1)[..., :nk]
    vec = jnp.concatenate([body, wrap], axis=-1).reshape(depth, HB, 1, period)
    return pl.pallas_call(
        functools.partial(_bias_kernel, tq=tq, nk=nk),
        grid=(depth, HB),
        in_specs=[pl.BlockSpec((None, None, 1, period), lambda l, h: (l, h, 0, 0))],
        out_specs=pl.BlockSpec((None, n_slots, None, tq, tq), lambda l, h: (l, 0, h, 0, 0)),
        out_shape=jax.ShapeDtypeStruct((depth, n_slots, HB, tq, tq), F32),
        compiler_params=_params(("parallel", "parallel")),
        name="band_bias",
    )(vec)


BAND_TQ = 256


def _tiles(rows, seq):
    return dict(tm=min(rows, 512), tm_post=min(rows, 1024), tb=min(seq, 1024))


def _layer(x, p_all, s0_all, kc, vc, prev, bias, w_al, layer, lw):
    bsz, t, _ = x.shape
    x2d = x.reshape(bsz * t, D_MODEL)
    keep = min(WINDOW, t)
    tiles = _tiles(bsz * t, t)
    tm, tm_post, tb, tq = tiles['tm'], tiles['tm_post'], tiles['tb'], BAND_TQ
    act, lg, ktail, vtail = _in_proj(x2d, w_al, layer, lw, tm, t, keep)
    r3 = lambda a: a.reshape(bsz, -1, a.shape[-1])
    if s0_all is None:
        s0_all, s_layer = jnp.zeros((1, bsz, HA, DK_A, DV_A), F32), 0
    else:
        s_layer = layer
    if kc is None:
        oa, (st, kbuf, vbuf) = _gla(r3(act), r3(lg), s0_all, s_layer, tb, (ktail, vtail), prev,
                                    layer, p_all.shape[0])
        ob = _band_prompt(r3(act), bias, layer, tq)
    else:
        oa, (st,) = _gla(r3(act), r3(lg), s0_all, s_layer, tb, (), None if prev is None else prev[:1],
                         layer, p_all.shape[0])
        lc = kc.shape[3]
        assert lc == WINDOW and t <= CHUNK
        table = jnp.concatenate([bias[layer, p, :, :t, :] for p in range(bias.shape[1] - 1)], axis=-1)
        ob, kbuf, vbuf = _band_sample(r3(act), kc, vc, layer, r3(ktail), r3(vtail), table[:, :, :lc + t],
                                      None if prev is None else prev[1:])
    x_new = _post(oa.reshape(bsz * t, WA_V), ob.reshape(bsz * t, WB), act, x2d,
                  p_all.reshape(p_all.shape[0], bsz * t, P_DIM), layer, lw, tm_post)
    return x_new.reshape(bsz, t, D_MODEL), (st, kbuf, vbuf)


def kernel(x_prompt, x_sample, state_gla, cache_band_k, cache_band_v, p_prompt, p_sample,
           norm_g, w_in, w_gate_up, b_gate, gla_norm_g, q_norm_g, k_norm_g, rel_bias,
           w_branch_a, w_branch_b, w_out, ple_norm_g, w_ple_gate, w_ple):
    depth = w_in.shape[0]
    bias = _band_bias(rel_bias, BAND_TQ)
    w_al = _feature_major_w_in(w_in)
    xp, xs = x_prompt, x_sample

    def feature_major(a):
        return jnp.transpose(a, (0, 1, 3, 4, 2)).reshape(a.shape[:2] + (WB, a.shape[2]))

    def frame_major(a):
        return jnp.transpose(a.reshape(a.shape[:2] + (HB, DH_B, a.shape[3])), (0, 1, 4, 2, 3))

    kc, vc = feature_major(cache_band_k), feature_major(cache_band_v)
    prompt_out = sample_out = None
    for i in range(depth):
        lw = _layer_weights(i, norm_g, w_gate_up, b_gate, gla_norm_g, q_norm_g, k_norm_g,
                            w_branch_a, w_branch_b, w_out, ple_norm_g, w_ple_gate, w_ple)
        xp, prompt_out = _layer(xp, p_prompt, None, None, None, prompt_out, bias, w_al, i, lw)
        xs, sample_out = _layer(xs, p_sample, state_gla, kc, vc, sample_out, bias, w_al, i, lw)
    (sp, kbp, vbp), (ss, kbs, vbs) = prompt_out, sample_out
    return (xp, xs, sp, frame_major(kbp), frame_major(vbp), ss, frame_major(kbs), frame_major(vbs))
```
